```python
import math
import jax, jax.numpy as jnp
from jax import lax
import numpy as np

D_MODEL = 1024
BATCH = 2
SEQ = 8192
DEPTH = 4

N_MEM = 256
HEAD_DIM = 64
ROPE_THETA = 500000.0
ROPE_DIM = HEAD_DIM // 4
NORM_EPS = 1e-6
NEG_INF = -1e30
ATTN_Q_BLOCK = 128

A_HEADS = 6
MOBA_BLOCK = 256
MOBA_TOPK = 3
MOBA_Q_BLOCK = 64
B_HEADS = 6
B_LORA_W = 32
B_LORA_A = 32
RWKV_GN_EPS = 64e-5
C_HEADS = 4
C_VDIM = 2 * HEAD_DIM
D_HEADS = 4
D_CHUNK = 64
M_HEADS = 4

A_W = A_HEADS * HEAD_DIM
B_W = B_HEADS * HEAD_DIM
C_W = C_HEADS * C_VDIM
D_W = D_HEADS * HEAD_DIM
M_W = M_HEADS * HEAD_DIM
MIX_W_EVEN = A_W + B_W + M_W
MIX_W_ODD = C_W + D_W + M_W
B_SHIFT_W = 3 * B_W + B_LORA_W + B_LORA_A
EVEN_SIZES = (A_W, A_W, A_W, A_W, B_SHIFT_W, B_W, M_W, M_W)
ODD_SIZES = (2 * C_HEADS * HEAD_DIM, 2 * C_HEADS * HEAD_DIM, C_W, C_W, D_W, D_W, D_W, D_W, M_W, M_W)
EVEN_IN = sum(EVEN_SIZES)
ODD_IN = sum(ODD_SIZES)
N_EVEN = (DEPTH + 1) // 2
N_ODD = DEPTH // 2

kernel_name = 'hybrid_moba_rwkv7_diffattn_hgrn2'

F32 = jnp.float32


def rms_norm(x, g, eps=NORM_EPS):
    xf = x.astype(F32)
    y = xf * lax.rsqrt(jnp.mean(xf * xf, axis=-1, keepdims=True) + eps)
    return (y * g.astype(F32)).astype(x.dtype)


def split_last(t, sizes):
    return jnp.split(t, np.cumsum(sizes)[:-1].tolist(), axis=-1)


def split_heads(t, n):
    b, s, w = t.shape
    return t.reshape(b, s, n, w // n).transpose(0, 2, 1, 3)


def merge_heads(t):
    b, h, s, d = t.shape
    return t.transpose(0, 2, 1, 3).reshape(b, s, h * d)


def rope_tables(seq):
    pos = jnp.arange(seq, dtype=F32)
    inv = 1.0 / (ROPE_THETA ** (jnp.arange(0, ROPE_DIM, 2, dtype=F32) / ROPE_DIM))
    ang = pos[:, None] * inv[None, :]
    return jnp.cos(ang), jnp.sin(ang)


def partial_rope(x, cos, sin):
    xr = x[..., :ROPE_DIM].astype(F32)
    x1, x2 = xr[..., :ROPE_DIM // 2], xr[..., ROPE_DIM // 2:]
    rot = jnp.concatenate([x1 * cos - x2 * sin, x1 * sin + x2 * cos], axis=-1).astype(x.dtype)
    return jnp.concatenate([rot, x[..., ROPE_DIM:]], axis=-1)


def token_shift(t, mu):
    prev = jnp.pad(t, ((0, 0), (1, 0), (0, 0)))[:, :-1]
    return t + (prev - t) * mu


def moba_attention(q, k, v):
    bn, h, s, dh = q.shape
    nb = -(-s // MOBA_BLOCK)
    pad = nb * MOBA_BLOCK - s
    padw = ((0, 0), (0, 0), (0, pad), (0, 0))
    kp, vp = jnp.pad(k, padw), jnp.pad(v, padw)
    kb = kp.reshape(bn, h, nb, MOBA_BLOCK, dh)
    vb = vp.reshape(bn, h, nb, MOBA_BLOCK, dh)
    kmean = jnp.mean(kb.astype(F32), axis=3)
    topk = min(MOBA_TOPK, nb)
    scale = dh ** -0.5
    bidx = jnp.arange(bn)[:, None, None, None]
    hidx = jnp.arange(h)[None, :, None, None]
    blk_ids = jnp.arange(nb)
    own_off = jnp.arange(MOBA_BLOCK)

    def block(qi):
        q0 = qi * MOBA_Q_BLOCK
        qb = lax.dynamic_slice_in_dim(q, q0, MOBA_Q_BLOCK, axis=2)
        qpos = q0 + jnp.arange(MOBA_Q_BLOCK)
        own = q0 // MOBA_BLOCK
        gate = jnp.einsum('bhqd,bhnd->bhqn', qb.astype(F32), kmean)
        gate = jnp.where(blk_ids < own, gate, NEG_INF)
        _, sel = lax.top_k(gate, topk)
        valid = (sel < own)[..., None]
        ksel = kb[bidx, hidx, sel]
        vsel = vb[bidx, hidx, sel]
        s_sel = jnp.einsum('bhqd,bhqtld->bhqtl', qb, ksel).astype(F32) * scale
        s_sel = jnp.where(valid, s_sel, NEG_INF).reshape(bn, h, MOBA_Q_BLOCK, topk * MOBA_BLOCK)
        k_own = lax.dynamic_slice_in_dim(kp, own * MOBA_BLOCK, MOBA_BLOCK, axis=2)
        v_own = lax.dynamic_slice_in_dim(vp, own * MOBA_BLOCK, MOBA_BLOCK, axis=2)
        s_own = jnp.einsum('bhqd,bhld->bhql', qb, k_own).astype(F32) * scale
        s_own = jnp.where(own * MOBA_BLOCK + own_off[None, :] <= qpos[:, None], s_own, NEG_INF)
        p = jax.nn.softmax(jnp.concatenate([s_sel, s_own], axis=-1), axis=-1).astype(v.dtype)
        p_sel = p[..., :topk * MOBA_BLOCK].reshape(bn, h, MOBA_Q_BLOCK, topk, MOBA_BLOCK)
        p_own = p[..., topk * MOBA_BLOCK:]
        return (jnp.einsum('bhqtl,bhqtld->bhqd', p_sel, vsel)
                + jnp.einsum('bhql,bhld->bhqd', p_own, v_own))

    out = lax.map(block, jnp.arange(s // MOBA_Q_BLOCK))
    return jnp.moveaxis(out, 0, 2).reshape(bn, h, s, dh)


def rwkv7_scan(r, w, k, v, a, b):
    bn, s, h, dh = r.shape
    xs = tuple(jnp.moveaxis(t, 1, 0) for t in (r, w, k, v, a, b))

    def step(st, inp):
        r_t, w_t, k_t, v_t, a_t, b_t = inp
        sa = jnp.einsum('bhvk,bhk->bhv', st, a_t)
        st = st * w_t[:, :, None, :] + sa[..., None] * b_t[:, :, None, :] + v_t[..., None] * k_t[:, :, None, :]
        return st, jnp.einsum('bhvk,bhk->bhv', st, r_t)

    _, ys = lax.scan(step, jnp.zeros((bn, h, dh, dh), F32), xs)
    return jnp.moveaxis(ys, 0, 1)


def rwkv7_time_mix(bs, mu, w0, w2, a0, a2, k_k, k_a, r_k, lnx_g, lnx_b):
    bn, s, _ = bs.shape
    bs = token_shift(bs, mu)
    r, k, v, wl, al = (t.astype(F32) for t in split_last(bs, (B_W, B_W, B_W, B_LORA_W, B_LORA_A)))
    w_log = -jax.nn.softplus(-(w0.astype(F32) + jnp.tanh(wl) @ w2.astype(F32))) - 0.5
    decay = jnp.exp(-jnp.exp(w_log))
    a = jax.nn.sigmoid(a0.astype(F32) + al @ a2.astype(F32))
    kk = k * k_k.astype(F32)
    k = k * (1.0 + (a - 1.0) * k_a.astype(F32))
    hs = lambda t: t.reshape(bn, s, B_HEADS, HEAD_DIM)
    r, k, v, kk, a, decay = map(hs, (r, k, v, kk, a, decay))
    kk = kk * lax.rsqrt(jnp.maximum(jnp.sum(kk * kk, axis=-1, keepdims=True), 1e-24))
    y = rwkv7_scan(r, decay, k, v, -kk, kk * a)
    mean = jnp.mean(y, axis=-1, keepdims=True)
    var = jnp.mean(jnp.square(y - mean), axis=-1, keepdims=True)
    y = ((y - mean) * lax.rsqrt(var + RWKV_GN_EPS)).reshape(bn, s, B_W) * lnx_g.astype(F32) + lnx_b.astype(F32)
    bonus = jnp.sum(r * k * r_k.astype(F32), axis=-1, keepdims=True) * v
    return (y + bonus.reshape(bn, s, B_W)).astype(bs.dtype)


def diff_attention(q, k, v, lam):
    bn, h, _, s, dh = q.shape
    scale = dh ** -0.5
    kpos = jnp.arange(s)

    def block(qi):
        q0 = qi * ATTN_Q_BLOCK
        qb = lax.dynamic_slice_in_dim(q, q0, ATTN_Q_BLOCK, axis=3)
        qpos = q0 + jnp.arange(ATTN_Q_BLOCK)
        sc = jnp.einsum('bhmqd,bhmkd->bhmqk', qb, k).astype(F32) * scale
        sc = jnp.where(kpos[None, :] <= qpos[:, None], sc, NEG_INF)
        p = jax.nn.softmax(sc, axis=-1)
        pd = p[:, :, 0] - lam * p[:, :, 1]
        return jnp.einsum('bhqk,bhkd->bhqd', pd.astype(v.dtype), v)

    out = lax.map(block, jnp.arange(s // ATTN_Q_BLOCK))
    return jnp.moveaxis(out, 0, 2).reshape(bn, h, s, v.shape[-1])


def hgrn2_chunked(q, k, v, log_f):
    bn, h, s, dk = q.shape
    dv = v.shape[-1]
    nc = s // D_CHUNK
    chunks = lambda t: jnp.moveaxis(t.reshape(bn, h, nc, D_CHUNK, t.shape[-1]), 2, 0)
    causal = jnp.tril(jnp.ones((D_CHUNK, D_CHUNK), dtype=bool))[:, :, None]

    def step(st, inp):
        qt, kt, vt, gt = inp
        bcum = jnp.cumsum(gt, axis=2)
        inter = jnp.einsum('bhtd,bhde->bhte', qt * jnp.exp(bcum), st)
        diff = jnp.where(causal, bcum[:, :, :, None, :] - bcum[:, :, None, :, :], NEG_INF)
        scores = jnp.einsum('bhtd,bhsd,bhtsd->bhts', qt, kt, jnp.exp(diff))
        intra = jnp.einsum('bhts,bhse->bhte', scores, vt)
        b_last = bcum[:, :, -1:, :]
        st = st * jnp.exp(b_last[:, :, 0, :])[..., None] + jnp.einsum('bhsd,bhse->bhde', kt * jnp.exp(b_last - bcum), vt)
        return st, inter + intra

    _, o = lax.scan(step, jnp.zeros((bn, h, dk, dv), F32), tuple(map(chunks, (q, k, v, log_f))))
    return jnp.moveaxis(o, 0, 2).reshape(bn, h, s, dv)


def memory_kv(mem, g, w_kv, kn_g):
    km, vm = split_last(rms_norm(mem, g) @ w_kv, (M_W, M_W))
    return rms_norm(split_heads(km, M_HEADS), kn_g), split_heads(vm, M_HEADS)


def memory_attention(q, km, vm):
    sc = jnp.einsum('bhsd,bhmd->bhsm', q, km).astype(F32) * HEAD_DIM ** -0.5
    p = jax.nn.softmax(sc, axis=-1).astype(vm.dtype)
    return merge_heads(jnp.einsum('bhsm,bhmd->bhsd', p, vm))


def even_layer(h, km, vm, cos, sin, w_in, w_out, a_qn_g, a_kn_g, m_qn_g,
               b_mu, b_w0, b_w2, b_a0, b_a2, b_k_k, b_k_a, b_r_k, b_lnx_g, b_lnx_b):
    qa, ka, va, ga, bs, gb, qm, gm = split_last(h @ w_in, EVEN_SIZES)
    qa = partial_rope(rms_norm(split_heads(qa, A_HEADS), a_qn_g), cos, sin)
    ka = partial_rope(rms_norm(split_heads(ka, A_HEADS), a_kn_g), cos, sin)
    oa = merge_heads(moba_attention(qa, ka, split_heads(va, A_HEADS)))
    ob = rwkv7_time_mix(bs, b_mu, b_w0, b_w2, b_a0, b_a2, b_k_k, b_k_a, b_r_k, b_lnx_g, b_lnx_b)
    om = memory_attention(rms_norm(split_heads(qm, M_HEADS), m_qn_g), km, vm)
    mixed = jnp.concatenate([oa * jax.nn.silu(ga), ob * jax.nn.silu(gb), om * jax.nn.silu(gm)], axis=-1)
    return mixed @ w_out


def odd_layer(h, km, vm, cos, sin, li, lb, w_in, w_out, c_qn_g, c_kn_g,
              c_lq1, c_lk1, c_lq2, c_lk2, c_subln_g, d_gn_g, m_qn_g):
    bn, s, _ = h.shape
    qc, kc, vc, gc, qd, fd, idd, gd, qm, gm = split_last(h @ w_in, ODD_SIZES)
    pair_heads = lambda t: t.reshape(bn, s, C_HEADS, 2, HEAD_DIM).transpose(0, 2, 3, 1, 4)
    qc = partial_rope(rms_norm(pair_heads(qc), c_qn_g), cos, sin)
    kc = partial_rope(rms_norm(pair_heads(kc), c_kn_g), cos, sin)
    lam_init = 0.8 - 0.6 * math.exp(-0.3 * li)
    lam = (jnp.exp(jnp.sum(c_lq1.astype(F32) * c_lk1.astype(F32)))
           - jnp.exp(jnp.sum(c_lq2.astype(F32) * c_lk2.astype(F32))) + lam_init)
    oc = diff_attention(qc, kc, split_heads(vc, C_HEADS), lam)
    oc = merge_heads(rms_norm(oc, c_subln_g) * (1.0 - lam_init))
    fr = fd.astype(F32)
    log_f = jnp.logaddexp(jnp.log(lb), jnp.log1p(-lb) + jax.nn.log_sigmoid(fr))
    ig = (1.0 - lb) * jax.nn.sigmoid(-fr)
    od = hgrn2_chunked(split_heads(qd.astype(F32), D_HEADS), split_heads(ig, D_HEADS),
                       split_heads(idd.astype(F32), D_HEADS), split_heads(log_f, D_HEADS))
    od = merge_heads(rms_norm(od, d_gn_g)).astype(h.dtype)
    om = memory_attention(rms_norm(split_heads(qm, M_HEADS), m_qn_g), km, vm)
    mixed = jnp.concatenate([oc * jax.nn.silu(gc), od * jax.nn.silu(gd), om * jax.nn.silu(gm)], axis=-1)
    return mixed @ w_out


def setup_inputs(seed: int = 0) -> dict:
    key = jax.random.key(seed)
    ks = iter(jax.random.split(key, 40))
    nrm = lambda shape, scale: scale * jax.random.normal(next(ks), shape, F32)
    gain = lambda shape: 1.0 + 0.02 * jax.random.normal(next(ks), shape, F32)
    return {
        'x': nrm((BATCH, SEQ, D_MODEL), 1.0),
        'mem': nrm((BATCH, N_MEM, D_MODEL), 1.0),
        'ln_g': gain((DEPTH, D_MODEL)),
        'mem_ln_g': gain((DEPTH, D_MODEL)),
        'w_mem_kv': nrm((DEPTH, D_MODEL, 2 * M_W), D_MODEL ** -0.5),
        'm_qn_g': gain((DEPTH, HEAD_DIM)),
        'm_kn_g': gain((DEPTH, HEAD_DIM)),
        'e_w_in': nrm((N_EVEN, D_MODEL, EVEN_IN), D_MODEL ** -0.5),
        'e_w_out': nrm((N_EVEN, MIX_W_EVEN, D_MODEL), MIX_W_EVEN ** -0.5),
        'a_qn_g': gain((N_EVEN, HEAD_DIM)),
        'a_kn_g': gain((N_EVEN, HEAD_DIM)),
        'b_mu': jax.random.uniform(next(ks), (N_EVEN, B_SHIFT_W), F32),
        'b_w0': -2.0 + nrm((N_EVEN, B_W), 1.0),
        'b_w2': nrm((N_EVEN, B_LORA_W, B_W), 0.5 * B_LORA_W ** -0.5),
        'b_a0': nrm((N_EVEN, B_W), 0.1),
        'b_a2': nrm((N_EVEN, B_LORA_A, B_W), 0.5 * B_LORA_A ** -0.5),
        'b_k_k': 0.85 + nrm((N_EVEN, B_W), 0.05),
        'b_k_a': 1.0 + nrm((N_EVEN, B_W), 0.05),
        'b_r_k': nrm((N_EVEN, B_HEADS, HEAD_DIM), 0.1),
        'b_lnx_g': gain((N_EVEN, B_W)),
        'b_lnx_b': nrm((N_EVEN, B_W), 0.02),
        'o_w_in': nrm((N_ODD, D_MODEL, ODD_IN), D_MODEL ** -0.5),
        'o_w_out': nrm((N_ODD, MIX_W_ODD, D_MODEL), MIX_W_ODD ** -0.5),
        'c_qn_g': gain((N_ODD, HEAD_DIM)),
        'c_kn_g': gain((N_ODD, HEAD_DIM)),
        'c_lq1': nrm((N_ODD, HEAD_DIM), 0.1),
        'c_lk1': nrm((N_ODD, HEAD_DIM), 0.1),
        'c_lq2': nrm((N_ODD, HEAD_DIM), 0.1),
        'c_lk2': nrm((N_ODD, HEAD_DIM), 0.1),
        'c_subln_g': gain((N_ODD, C_VDIM)),
        'd_lb': nrm((N_ODD, D_W), 0.1),
        'd_gn_g': gain((N_ODD, HEAD_DIM)),
    }


def reference(x, mem, ln_g, mem_ln_g, w_mem_kv, m_qn_g, m_kn_g, e_w_in, e_w_out, a_qn_g, a_kn_g,
              b_mu, b_w0, b_w2, b_a0, b_a2, b_k_k, b_k_a, b_r_k, b_lnx_g, b_lnx_b,
              o_w_in, o_w_out, c_qn_g, c_kn_g, c_lq1, c_lk1, c_lq2, c_lk2, c_subln_g, d_lb, d_gn_g):
    cos, sin = rope_tables(x.shape[1])
    lbs = jax.nn.softmax(d_lb.astype(F32), axis=0)
    lbs = jnp.cumsum(lbs, axis=0) - lbs[0:1]
    for li in range(DEPTH):
        j = li // 2
        h = rms_norm(x, ln_g[li])
        km, vm = memory_kv(mem, mem_ln_g[li], w_mem_kv[li], m_kn_g[li])
        if li % 2 == 0:
            y = even_layer(h, km, vm, cos, sin, e_w_in[j], e_w_out[j], a_qn_g[j], a_kn_g[j], m_qn_g[li],
                           b_mu[j], b_w0[j], b_w2[j], b_a0[j], b_a2[j], b_k_k[j], b_k_a[j], b_r_k[j],
                           b_lnx_g[j], b_lnx_b[j])
        else:
            lb = jnp.maximum(lbs[j], 0.0)
            y = odd_layer(h, km, vm, cos, sin, li, lb, o_w_in[j], o_w_out[j], c_qn_g[j], c_kn_g[j],
                          c_lq1[j], c_lk1[j], c_lq2[j], c_lk2[j], c_subln_g[j], d_gn_g[j], m_qn_g[li])
        x = x + y
    return x
```

```python
import functools
import math

import jax
import jax.numpy as jnp
from jax import lax
from jax.experimental import pallas as pl
from jax.experimental.pallas import tpu as pltpu

F32 = jnp.float32
BF16 = jnp.bfloat16
HIGHEST = lax.Precision.HIGHEST

N_MEM = 256
HEAD_DIM = 64
ROPE_THETA = 500000.0
ROPE_DIM = HEAD_DIM // 4
NORM_EPS = 1e-6
NEG_INF = -1e30
A_HEADS = 6
MOBA_BLOCK = 256
MOBA_TOPK = 3
B_HEADS = 6
B_LORA = 32
RWKV_GN_EPS = 64e-5
C_HEADS = 4
D_HEADS = 4
M_HEADS = 4
A_W = A_HEADS * HEAD_DIM
B_W = B_HEADS * HEAD_DIM
C_W = C_HEADS * 2 * HEAD_DIM
D_W = D_HEADS * HEAD_DIM
M_W = M_HEADS * HEAD_DIM
ATTN_SCALE = HEAD_DIM ** -0.5

LANES = 128
VMEM_LIMIT = 48 * 1024 * 1024

ROW_TILE = 256
ATTN_TILE = 256
CHUNK = 64
SUB = 16
SCAN_ROWS = 512

_NT = (((1,), (1,)), ((), ()))
_TN = (((0,), (0,)), ((), ()))


def _cparams(sem):
    return pltpu.CompilerParams(dimension_semantics=sem, vmem_limit_bytes=VMEM_LIMIT)


def _mm(a, b):
    return jnp.dot(a.astype(BF16), b.astype(BF16), preferred_element_type=F32)


def _mm_nt(a, b):
    return lax.dot_general(a.astype(BF16), b.astype(BF16), _NT, preferred_element_type=F32)


def _mm_tn(a, b):
    return lax.dot_general(a.astype(BF16), b.astype(BF16), _TN, preferred_element_type=F32)


def _mm_hi(a, b):
    return jnp.dot(a, b, precision=HIGHEST, preferred_element_type=F32)


def _silu(x):
    return x * jax.nn.sigmoid(x)


def _lane_lo(width=LANES):
    lane = lax.broadcasted_iota(jnp.int32, (1, width), 1)
    return (lane % LANES) < HEAD_DIM


def _head_mean_matrix(width):
    r = lax.broadcasted_iota(jnp.int32, (width, width), 0) // HEAD_DIM
    c = lax.broadcasted_iota(jnp.int32, (width, width), 1) // HEAD_DIM
    return jnp.where(r == c, 1.0 / HEAD_DIM, 0.0).astype(F32)


def _proj_body(x_ref, g_ref, *refs, n_out):
    x = x_ref[...]
    ms = jnp.mean(x * x, axis=-1, keepdims=True)
    h = (x * lax.rsqrt(ms + NORM_EPS) * g_ref[...]).astype(BF16)
    for w_ref, o_ref in zip(refs[:n_out], refs[n_out:]):
        o_ref[...] = jnp.dot(h, w_ref[...], preferred_element_type=F32).astype(o_ref.dtype)


def rms_proj(x2d, g, ws, out_dtypes):
    n, d = x2d.shape
    tm = min(ROW_TILE, n)
    in_specs = [pl.BlockSpec((tm, d), lambda i: (i, 0)), pl.BlockSpec((1, d), lambda i: (0, 0))]
    in_specs += [pl.BlockSpec(w.shape, lambda i: (0, 0)) for w in ws]
    out_specs = [pl.BlockSpec((tm, w.shape[1]), lambda i: (i, 0)) for w in ws]
    out_shape = [jax.ShapeDtypeStruct((n, w.shape[1]), dt) for w, dt in zip(ws, out_dtypes)]
    return pl.pallas_call(
        functools.partial(_proj_body, n_out=len(ws)),
        grid=(n // tm,), in_specs=in_specs, out_specs=out_specs, out_shape=out_shape,
        compiler_params=_cparams(("parallel",)), name="rms_proj",
    )(x2d, g.reshape(1, d).astype(F32), *ws)


def _out_proj_body(x_ref, *refs, n_in):
    o_ref = refs[-1]
    acc = x_ref[...]
    for m_ref, w_ref in zip(refs[:n_in], refs[n_in:2 * n_in]):
        acc = acc + jnp.dot(m_ref[...], w_ref[...], preferred_element_type=F32)
    o_ref[...] = acc


def out_proj(x2d, parts, ws):
    n, d = x2d.shape
    tm = min(ROW_TILE, n)
    in_specs = [pl.BlockSpec((tm, d), lambda i: (i, 0))]
    in_specs += [pl.BlockSpec((tm, p.shape[1]), lambda i: (i, 0)) for p in parts]
    in_specs += [pl.BlockSpec(w.shape, lambda i: (0, 0)) for w in ws]
    return pl.pallas_call(
        functools.partial(_out_proj_body, n_in=len(parts)),
        grid=(n // tm,), in_specs=in_specs, out_specs=pl.BlockSpec((tm, d), lambda i: (i, 0)),
        out_shape=jax.ShapeDtypeStruct((n, d), F32),
        compiler_params=_cparams(("parallel",)), name="out_proj",
    )(x2d, *parts, *ws)


def _qk_prep_body(x_ref, g_ref, c_ref, s1_ref, s2_ref, o_ref, *maybe_mean_ref):
    x = x_ref[0]
    ms = _mm_hi(x * x, _head_mean_matrix(LANES))
    y = x * lax.rsqrt(ms + NORM_EPS) * g_ref[...]
    yr = (y * c_ref[...] + pltpu.roll(y, LANES - ROPE_DIM // 2, 1) * s1_ref[...]
          + pltpu.roll(y, ROPE_DIM // 2, 1) * s2_ref[...])
    o_ref[0] = yr.astype(o_ref.dtype)
    if maybe_mean_ref:
        maybe_mean_ref[0][0] = jnp.mean(yr, axis=0, keepdims=True)


def qk_prep(x, gain, tables, out_dtype, want_block_mean=False):
    b, s, w = x.shape
    tm = ATTN_TILE
    cos_t, sin1_t, sin2_t = tables
    g = jnp.tile(gain.astype(F32), LANES // HEAD_DIM).reshape(1, LANES)
    tile_spec = pl.BlockSpec((1, tm, LANES), lambda bi, i, c: (bi, i, c))
    tab_spec = pl.BlockSpec((tm, LANES), lambda bi, i, c: (i, 0))
    out_specs = [tile_spec]
    out_shape = [jax.ShapeDtypeStruct((b, s, w), out_dtype)]
    if want_block_mean:
        nb = s // tm
        out_specs.append(pl.BlockSpec((1, 1, LANES), lambda bi, i, c: (bi * nb + i, 0, c)))
        out_shape.append(jax.ShapeDtypeStruct((b * nb, 1, w), F32))
    outs = pl.pallas_call(
        _qk_prep_body, grid=(b, s // tm, w // LANES),
        in_specs=[tile_spec, pl.BlockSpec((1, LANES), lambda bi, i, c: (0, 0)), tab_spec, tab_spec, tab_spec],
        out_specs=out_specs, out_shape=out_shape,
        compiler_params=_cparams(("parallel", "parallel", "parallel")), name="qk_prep",
    )(x, g, cos_t, sin1_t, sin2_t)
    if want_block_mean:
        return outs[0], outs[1].reshape(b, s // tm, w)
    return outs[0]


def rope_tables_lanes(seq):
    pos = jnp.arange(seq, dtype=F32)
    inv = 1.0 / (ROPE_THETA ** (jnp.arange(0, ROPE_DIM, 2, dtype=F32) / ROPE_DIM))
    ang = pos[:, None] * inv[None, :]
    cos, sin = jnp.cos(ang), jnp.sin(ang)
    half = ROPE_DIM // 2
    ones = jnp.ones((seq, HEAD_DIM - ROPE_DIM), F32)
    zeros_h = jnp.zeros((seq, half), F32)
    zeros_r = jnp.zeros((seq, HEAD_DIM - ROPE_DIM), F32)
    c = jnp.concatenate([cos, cos, ones], axis=1)
    s1 = jnp.concatenate([-sin, zeros_h, zeros_r], axis=1)
    s2 = jnp.concatenate([zeros_h, sin, zeros_r], axis=1)
    rep = LANES // HEAD_DIM
    return jnp.tile(c, (1, rep)), jnp.tile(s1, (1, rep)), jnp.tile(s2, (1, rep))


def _softmax_first(s, v):
    m = jnp.max(s, axis=1, keepdims=True)
    p = jnp.exp(s - m)
    return m, jnp.sum(p, axis=1, keepdims=True), _mm(p, v)


def _softmax_update(s, v, m_ref, l_ref, acc_ref, h):
    m_old = m_ref[h]
    m_new = jnp.maximum(m_old, jnp.max(s, axis=1, keepdims=True))
    alpha = jnp.exp(m_old - m_new)
    p = jnp.exp(s - m_new)
    m_ref[h] = m_new
    l_ref[h] = alpha * l_ref[h] + jnp.sum(p, axis=1, keepdims=True)
    acc_ref[h] = alpha * acc_ref[h] + _mm(p, v)


def _moba_body(q_ref, k_ref, v_ref, km_ref, g_ref, o_ref, bias_ref, m_ref, l_ref, acc_ref, *, nb):
    i = pl.program_id(2)
    t = ATTN_TILE
    q = q_ref[0]
    lo = _lane_lo()
    masks = (lo, jnp.logical_not(lo))
    km = km_ref[0]
    blk = lax.broadcasted_iota(jnp.int32, (t, nb), 1).astype(F32)
    own = i.astype(F32)
    qb = []
    for h in range(2):
        qh = jnp.where(masks[h], q, 0.0)
        qb.append((qh * ATTN_SCALE).astype(BF16))
        gate = lax.dot_general(qh, km, _NT, precision=HIGHEST, preferred_element_type=F32)
        gate = jnp.where(blk < own, gate, NEG_INF)
        sel = jnp.zeros((t, nb), F32)
        for _ in range(MOBA_TOPK):
            mx = jnp.max(gate, axis=1, keepdims=True)
            first = jnp.min(jnp.where(gate == mx, blk, float(nb)), axis=1, keepdims=True)
            hit = blk == first
            sel = jnp.where(jnp.logical_and(hit, first < own), 1.0, sel)
            gate = jnp.where(hit, -jnp.inf, gate)
        bias_ref[h] = jnp.where(sel > 0.0, 0.0, NEG_INF)

    row = lax.broadcasted_iota(jnp.int32, (t, t), 0)
    col = lax.broadcasted_iota(jnp.int32, (t, t), 1)
    d0 = pl.multiple_of(i * t, t)
    kd = k_ref[0, pl.ds(d0, t), :]
    vd = v_ref[0, pl.ds(d0, t), :]
    for h in range(2):
        s = jnp.where(col <= row, _mm_nt(qb[h], kd), NEG_INF)
        m, l, acc = _softmax_first(s, vd)
        m_ref[h] = m
        l_ref[h] = l
        acc_ref[h] = acc

    def body(j, carry):
        j0 = pl.multiple_of(j * t, t)
        kj = k_ref[0, pl.ds(j0, t), :]
        vj = v_ref[0, pl.ds(j0, t), :]
        jf = j.astype(F32)
        for h in range(2):
            bcol = jnp.sum(jnp.where(blk == jf, bias_ref[h], 0.0), axis=1, keepdims=True)
            _softmax_update(_mm_nt(qb[h], kj) + bcol, vj, m_ref, l_ref, acc_ref, h)
        return carry

    lax.fori_loop(0, i, body, 0)
    out = jnp.where(lo, acc_ref[0] / l_ref[0], acc_ref[1] / l_ref[1])
    o_ref[0] = (out * _silu(g_ref[0])).astype(o_ref.dtype)


def moba_attention(q, k, v, kmean, gate):
    b, s, w = q.shape
    t = ATTN_TILE
    nb = s // t
    tile = pl.BlockSpec((1, t, LANES), lambda bi, p, i: (bi, i, p))
    full = pl.BlockSpec((1, s, LANES), lambda bi, p, i: (bi, 0, p))
    return pl.pallas_call(
        functools.partial(_moba_body, nb=nb), grid=(b, w // LANES, nb),
        in_specs=[tile, full, full, pl.BlockSpec((1, nb, LANES), lambda bi, p, i: (bi, 0, p)), tile],
        out_specs=tile, out_shape=jax.ShapeDtypeStruct((b, s, w), BF16),
        scratch_shapes=[pltpu.VMEM((2, t, nb), F32), pltpu.VMEM((2, t, 1), F32), pltpu.VMEM((2, t, 1), F32),
                        pltpu.VMEM((2, t, LANES), F32)],
        compiler_params=_cparams(("parallel", "parallel", "arbitrary")), name="moba_attention",
    )(q, k, v, kmean, gate)


def _diff_body(q_ref, k_ref, v_ref, g_ref, lqk_ref, sg_ref, o_ref, m_ref, l_ref, acc_ref, *, lam_init):
    i = pl.program_id(2)
    t = ATTN_TILE
    lo = _lane_lo()
    masks = (lo, jnp.logical_not(lo))
    q = q_ref[0]
    qb = [(jnp.where(masks[h], q, 0) * ATTN_SCALE).astype(BF16) for h in range(2)]
    row = lax.broadcasted_iota(jnp.int32, (t, t), 0)
    col = lax.broadcasted_iota(jnp.int32, (t, t), 1)
    d0 = pl.multiple_of(i * t, t)
    kd = k_ref[0, pl.ds(d0, t), :]
    vd = v_ref[0, pl.ds(d0, t), :]
    for h in range(2):
        s = jnp.where(col <= row, _mm_nt(qb[h], kd), NEG_INF)
        m, l, acc = _softmax_first(s, vd)
        m_ref[h] = m
        l_ref[h] = l
        acc_ref[h] = acc

    def body(j, carry):
        j0 = pl.multiple_of(j * t, t)
        kj = k_ref[0, pl.ds(j0, t), :]
        vj = v_ref[0, pl.ds(j0, t), :]
        for h in range(2):
            _softmax_update(_mm_nt(qb[h], kj), vj, m_ref, l_ref, acc_ref, h)
        return carry

    lax.fori_loop(0, i, body, 0)
    lqk = lqk_ref[...]
    lam = (jnp.exp(jnp.sum(lqk[0:1] * lqk[1:2], axis=1, keepdims=True))
           - jnp.exp(jnp.sum(lqk[2:3] * lqk[3:4], axis=1, keepdims=True)) + lam_init)
    o = acc_ref[0] / l_ref[0] - lam * (acc_ref[1] / l_ref[1])
    ms = jnp.mean(o * o, axis=1, keepdims=True)
    o = o * lax.rsqrt(ms + NORM_EPS) * sg_ref[...] * (1.0 - lam_init)
    o_ref[0] = (o * _silu(g_ref[0])).astype(o_ref.dtype)


def diff_attention(q, k, v, gate, lqk, subln_g, lam_init):
    b, s, w = q.shape
    t = ATTN_TILE
    tile = pl.BlockSpec((1, t, LANES), lambda bi, h, i: (bi, i, h))
    full = pl.BlockSpec((1, s, LANES), lambda bi, h, i: (bi, 0, h))
    return pl.pallas_call(
        functools.partial(_diff_body, lam_init=lam_init), grid=(b, w // LANES, s // t),
        in_specs=[tile, full, full, tile, pl.BlockSpec((4, HEAD_DIM), lambda bi, h, i: (0, 0)),
                  pl.BlockSpec((1, LANES), lambda bi, h, i: (0, 0))],
        out_specs=tile, out_shape=jax.ShapeDtypeStruct((b, s, w), BF16),
        scratch_shapes=[pltpu.VMEM((2, t, 1), F32), pltpu.VMEM((2, t, 1), F32), pltpu.VMEM((2, t, LANES), F32)],
        compiler_params=_cparams(("parallel", "parallel", "arbitrary")), name="diff_attention",
    )(q, k, v, gate, lqk, subln_g.reshape(1, LANES).astype(F32))


def _mem_body(q_ref, km_ref, vm_ref, g_ref, qg_ref, kg_ref, o_ref):
    gm = _head_mean_matrix(LANES)
    lo = _lane_lo()
    masks = (lo, jnp.logical_not(lo))
    q = q_ref[0]
    q = q * lax.rsqrt(_mm_hi(q * q, gm) + NORM_EPS) * qg_ref[...]
    k = km_ref[0]
    k = (k * lax.rsqrt(_mm_hi(k * k, gm) + NORM_EPS) * kg_ref[...]).astype(BF16)
    v = vm_ref[0]
    outs = []
    for h in range(2):
        qh = (jnp.where(masks[h], q, 0.0) * ATTN_SCALE).astype(BF16)
        m, l, acc = _softmax_first(_mm_nt(qh, k), v)
        outs.append(acc / l)
    out = jnp.where(lo, outs[0], outs[1])
    o_ref[0] = (out * _silu(g_ref[0])).astype(o_ref.dtype)


def mem_attention(q, km, vm, gate, q_gain, k_gain):
    b, s, w = q.shape
    t = ROW_TILE
    tile = pl.BlockSpec((1, t, LANES), lambda bi, p, i: (bi, i, p))
    mem = pl.BlockSpec((1, N_MEM, LANES), lambda bi, p, i: (bi, 0, p))
    gain = pl.BlockSpec((1, LANES), lambda bi, p, i: (0, 0))
    rep = LANES // HEAD_DIM
    return pl.pallas_call(
        _mem_body, grid=(b, w // LANES, s // t),
        in_specs=[tile, mem, mem, tile, gain, gain],
        out_specs=tile, out_shape=jax.ShapeDtypeStruct((b, s, w), BF16),
        compiler_params=_cparams(("parallel", "parallel", "parallel")), name="mem_attention",
    )(q, km, vm, gate, jnp.tile(q_gain.astype(F32), rep).reshape(1, LANES),
      jnp.tile(k_gain.astype(F32), rep).reshape(1, LANES))


def _shift(cur, prev8, mu, first):
    rows = lax.broadcasted_iota(jnp.int32, cur.shape, 0)
    before = jnp.where(first, 0.0, prev8[7:8, :])
    prev = jnp.where(rows == 0, before, pltpu.roll(cur, 1, 0))
    return cur + (prev - cur) * mu


def _rwkv_prep_body(r_ref, k_ref, v_ref, lo_ref, rp_ref, kp_ref, vp_ref, lp_ref, mu_ref, mul_ref, w0_ref, w2_ref,
                    a0_ref, a2_ref, kk_ref, ka_ref, rk_ref,
                    ro_ref, lw_ref, ko_ref, vo_ref, kn_ref, ao_ref, bo_ref):
    first = pl.program_id(1) == 0
    mu = mu_ref[...]
    r = _shift(r_ref[0], rp_ref[0], mu[0:1], first)
    k = _shift(k_ref[0], kp_ref[0], mu[1:2], first)
    v = _shift(v_ref[0], vp_ref[0], mu[2:3], first)
    lora = _shift(lo_ref[0], lp_ref[0], mul_ref[...], first)
    z = w0_ref[...] + _mm_hi(jnp.tanh(lora), w2_ref[...])
    w_log = -(jnp.maximum(-z, 0.0) + jnp.log1p(jnp.exp(-jnp.abs(z)))) - 0.5
    a = jax.nn.sigmoid(a0_ref[...] + _mm_hi(lora, a2_ref[...]))
    kk = k * kk_ref[...]
    k2 = k * (1.0 + (a - 1.0) * ka_ref[...])
    ones = _head_mean_matrix(B_W) * float(HEAD_DIM)
    kk = kk * lax.rsqrt(jnp.maximum(_mm_hi(kk * kk, ones), 1e-24))
    ro_ref[0] = r
    lw_ref[0] = -jnp.exp(w_log)
    ko_ref[0] = k2
    vo_ref[0] = v
    kn_ref[0] = kk
    ao_ref[0] = a
    bo_ref[0] = _mm_hi(r * k2 * rk_ref[...], ones) * v


def rwkv_prep(r, k, v, lora, mu, w0, w2, a0, a2, k_k, k_a, r_k):
    b, s, w = r.shape
    t = ROW_TILE
    tile = pl.BlockSpec((1, t, w), lambda bi, i: (bi, i, 0))
    ltile = pl.BlockSpec((1, t, LANES), lambda bi, i: (bi, i, 0))
    prev_idx = lambda bi, i: (bi, jnp.maximum(i * (t // 8) - 1, 0), 0)
    ptile = pl.BlockSpec((1, 8, w), prev_idx)
    pltile = pl.BlockSpec((1, 8, LANES), prev_idx)
    const = lambda shape: pl.BlockSpec(shape, lambda bi, i: (0, 0))
    mu3 = jnp.stack([mu[:w], mu[w:2 * w], mu[2 * w:3 * w]]).astype(F32)
    mul = jnp.zeros((1, LANES), F32).at[0, :2 * B_LORA].set(mu[3 * w:])
    w2p = jnp.zeros((LANES, w), F32).at[:B_LORA].set(w2)
    a2p = jnp.zeros((LANES, w), F32).at[B_LORA:2 * B_LORA].set(a2)
    row = lambda p: p.reshape(1, w).astype(F32)
    return pl.pallas_call(
        _rwkv_prep_body, grid=(b, s // t),
        in_specs=[tile, tile, tile, ltile, ptile, ptile, ptile, pltile, const((3, w)), const((1, LANES)),
                  const((1, w)), const((LANES, w)), const((1, w)), const((LANES, w)), const((1, w)), const((1, w)),
                  const((1, w))],
        out_specs=[tile] * 7, out_shape=[jax.ShapeDtypeStruct((b, s, w), F32)] * 7,
        compiler_params=_cparams(("parallel", "parallel")), name="rwkv_prep",
    )(r, k, v, lora, r, k, v, lora, mu3, mul, row(w0), w2p, row(a0), a2p, row(k_k), row(k_a), row(r_k))


def _tri(n, strict):
    r = lax.broadcasted_iota(jnp.int32, (n, n), 0)
    c = lax.broadcasted_iota(jnp.int32, (n, n), 1)
    return (c < r) if strict else (c <= r)


def _block_diag_mask():
    r = lax.broadcasted_iota(jnp.int32, (LANES, LANES), 0) // HEAD_DIM
    c = lax.broadcasted_iota(jnp.int32, (LANES, LANES), 1) // HEAD_DIM
    return r == c


def _rwkv_scan_body(r_ref, lw_ref, k_ref, v_ref, kn_ref, a_ref, bo_ref, g_ref, lg_ref, lb_ref, o_ref, s_ref):
    @pl.when(pl.program_id(2) == 0)
    def _():
        s_ref[...] = jnp.zeros_like(s_ref)

    c = CHUNK
    lo = _lane_lo()
    masks = (lo, jnp.logical_not(lo))
    strict = _tri(c, True)
    incl = _tri(c, False)
    ltri = incl.astype(F32)
    eye = (lax.broadcasted_iota(jnp.int32, (c, c), 0) == lax.broadcasted_iota(jnp.int32, (c, c), 1)).astype(F32)
    bd = _block_diag_mask()
    gmean = _head_mean_matrix(LANES)

    def chunk(ci, carry):
        sl = pl.ds(pl.multiple_of(ci * c, c), c)
        r, lw, k, v, kn, a = (ref[0, sl, :] for ref in (r_ref, lw_ref, k_ref, v_ref, kn_ref, a_ref))
        cw = _mm_hi(ltri, lw)
        cl = cw[c - 1:c, :]
        at = -kn * jnp.exp(cw - lw)
        e_neg = jnp.exp(-cw)
        bt = kn * a * e_neg
        kt = k * e_neg
        rt = r * jnp.exp(cw)
        e_end = jnp.exp(cl - cw)
        s0 = s_ref[...]
        rhs = _mm_nt(at, s0)
        xs, arb, ark, ts = [], [], [], []
        for h in range(2):
            ah = jnp.where(masks[h], at, 0.0)
            rh = jnp.where(masks[h], rt, 0.0)
            n = jnp.where(strict, _mm_nt(ah, bt), 0.0)
            aak = jnp.where(strict, _mm_nt(ah, kt), 0.0)
            tinv = eye + n
            p = n
            for _ in range(5):
                p = _mm(p, p)
                tinv = tinv + _mm(tinv, p)
            ts.append(tinv)
            xs.append(rhs + _mm(aak, v))
            arb.append(jnp.where(incl, _mm_nt(rh, bt), 0.0))
            ark.append(jnp.where(incl, _mm_nt(rh, kt), 0.0))
        u = jnp.where(lo, _mm(ts[0], xs[0]), _mm(ts[1], xs[1]))
        y = _mm_nt(rt, s0) + jnp.where(lo, _mm(arb[0], u) + _mm(ark[0], v), _mm(arb[1], u) + _mm(ark[1], v))
        upd = _mm_tn(u, kn * a * e_end) + _mm_tn(v, k * e_end)
        s_ref[...] = s0 * jnp.exp(cl) + jnp.where(bd, upd, 0.0)
        mean = _mm_hi(y, gmean)
        d = y - mean
        var = _mm_hi(d * d, gmean)
        yn = d * lax.rsqrt(var + RWKV_GN_EPS) * lg_ref[...] + lb_ref[...] + bo_ref[0, sl, :]
        o_ref[0, sl, :] = (yn * _silu(g_ref[0, sl, :])).astype(o_ref.dtype)
        return carry

    lax.fori_loop(0, SCAN_ROWS // c, chunk, 0)


def rwkv_scan(r, lw, k, v, kn, a, bonus, gate, lnx_g, lnx_b):
    b, s, w = r.shape
    t = SCAN_ROWS
    tile = pl.BlockSpec((1, t, LANES), lambda bi, p, i: (bi, i, p))
    vec = pl.BlockSpec((1, LANES), lambda bi, p, i: (0, p))
    return pl.pallas_call(
        _rwkv_scan_body, grid=(b, w // LANES, s // t),
        in_specs=[tile] * 8 + [vec, vec],
        out_specs=tile, out_shape=jax.ShapeDtypeStruct((b, s, w), BF16),
        scratch_shapes=[pltpu.VMEM((LANES, LANES), F32)],
        compiler_params=_cparams(("parallel", "parallel", "arbitrary")), name="rwkv_scan",
    )(r, lw, k, v, kn, a, bonus, gate, lnx_g.reshape(1, w).astype(F32), lnx_b.reshape(1, w).astype(F32))


def _hgrn_body(q_ref, f_ref, i_ref, g_ref, lb_ref, gn_ref, o_ref, s_ref):
    @pl.when(pl.program_id(2) == 0)
    def _():
        s_ref[...] = jnp.zeros_like(s_ref)

    c = CHUNK
    lo = _lane_lo()
    masks = (lo, jnp.logical_not(lo))
    ltri = _tri(c, False).astype(F32)
    bd = _block_diag_mask()
    gmean = _head_mean_matrix(LANES)
    lb = lb_ref[...]
    log_lb = jnp.log(lb)
    log_1m = jnp.log1p(-lb)
    rows = lax.broadcasted_iota(jnp.int32, (SUB, 1), 0)

    def chunk(ci, carry):
        sl = pl.ds(pl.multiple_of(ci * c, c), c)
        q, fr, v = q_ref[0, sl, :], f_ref[0, sl, :], i_ref[0, sl, :]
        log_sig = jnp.minimum(fr, 0.0) - jnp.log1p(jnp.exp(-jnp.abs(fr)))
        z = log_1m + log_sig
        hi = jnp.maximum(log_lb, z)
        log_f = hi + jnp.log1p(jnp.exp(-jnp.abs(log_lb - z)))
        k = (1.0 - lb) * jax.nn.sigmoid(-fr)
        bc = _mm_hi(ltri, log_f)
        bl = bc[c - 1:c, :]
        s0 = s_ref[...]
        outs = []
        for bi in range(c // SUB):
            r0 = bi * SUB
            qs, bs, ks, vs = (x[r0:r0 + SUB] for x in (q, bc, k, v))
            o = _mm_nt(qs * jnp.exp(bs), s0)
            if bi > 0:
                ref = bc[r0 - 1:r0, :]
                qh = qs * jnp.exp(bs - ref)
                kh = k[:r0] * jnp.exp(ref - bc[:r0])
                vh = v[:r0]
                o = o + jnp.where(lo, _mm(_mm_nt(jnp.where(masks[0], qh, 0.0), kh), vh),
                                  _mm(_mm_nt(jnp.where(masks[1], qh, 0.0), kh), vh))
            for si in range(SUB):
                d = jnp.where(rows >= si, bs - bs[si:si + 1, :], NEG_INF)
                w = qs * ks[si:si + 1, :] * jnp.exp(d)
                w0 = jnp.sum(jnp.where(lo, w, 0.0), axis=1, keepdims=True)
                w1 = jnp.sum(jnp.where(lo, 0.0, w), axis=1, keepdims=True)
                o = o + jnp.where(lo, w0, w1) * vs[si:si + 1, :]
            outs.append(o)
        od = jnp.concatenate(outs, axis=0)
        s_ref[...] = s0 * jnp.exp(bl) + jnp.where(bd, _mm_tn(v, k * jnp.exp(bl - bc)), 0.0)
        ms = _mm_hi(od * od, gmean)
        on = od * lax.rsqrt(ms + NORM_EPS) * gn_ref[...]
        o_ref[0, sl, :] = (on * _silu(g_ref[0, sl, :])).astype(o_ref.dtype)
        return carry

    lax.fori_loop(0, SCAN_ROWS // c, chunk, 0)


def hgrn2(q, f, iv, gate, lb, gn_g):
    b, s, w = q.shape
    t = SCAN_ROWS
    tile = pl.BlockSpec((1, t, LANES), lambda bi, p, i: (bi, i, p))
    rep = LANES // HEAD_DIM
    return pl.pallas_call(
        _hgrn_body, grid=(b, w // LANES, s // t),
        in_specs=[tile] * 4 + [pl.BlockSpec((1, LANES), lambda bi, p, i: (0, p)),
                               pl.BlockSpec((1, LANES), lambda bi, p, i: (0, 0))],
        out_specs=tile, out_shape=jax.ShapeDtypeStruct((b, s, w), BF16),
        scratch_shapes=[pltpu.VMEM((LANES, LANES), F32)],
        compiler_params=_cparams(("parallel", "parallel", "arbitrary")), name="hgrn2",
    )(q, f, iv, gate, lb.reshape(1, w).astype(F32), jnp.tile(gn_g.astype(F32), rep).reshape(1, LANES))


def _memory_kv(memf, b, g, w_kv):
    wb = w_kv.astype(BF16)
    km, vm = rms_proj(memf, g, [wb[:, :M_W], wb[:, M_W:]], [F32, BF16])
    return km.reshape(b, N_MEM, M_W), vm.reshape(b, N_MEM, M_W)


def _even_layer(xf, b, s, km, vm, tables, ln_g, w_in, w_out, a_qn_g, a_kn_g, m_qn_g, m_kn_g,
                mu, w0, w2, a0, a2, k_k, k_a, r_k, lnx_g, lnx_b):
    wb = w_in.astype(BF16)
    edges = [0]
    for width in (A_W, A_W, A_W, A_W, B_W, B_W, B_W, 2 * B_LORA, B_W, M_W, M_W):
        edges.append(edges[-1] + width)
    ws = [wb[:, edges[n]:edges[n + 1]] for n in range(11)]
    ws[7] = jnp.pad(ws[7], ((0, 0), (0, LANES - 2 * B_LORA)))
    dts = [F32, F32, BF16, F32, F32, F32, F32, F32, F32, F32, F32]
    qa, ka, va, ga, rr, rk, rv, lora, gb, qm, gm = rms_proj(xf, ln_g, ws, dts)
    sh = lambda t: t.reshape(b, s, t.shape[-1])
    q = qk_prep(sh(qa), a_qn_g, tables, F32)
    k, kmean = qk_prep(sh(ka), a_kn_g, tables, BF16, want_block_mean=True)
    oa = moba_attention(q, k, sh(va), kmean, sh(ga))
    pre = rwkv_prep(sh(rr), sh(rk), sh(rv), sh(lora), mu, w0, w2, a0, a2, k_k, k_a, r_k.reshape(-1))
    ob = rwkv_scan(*pre, sh(gb), lnx_g, lnx_b)
    om = mem_attention(sh(qm), km, vm, sh(gm), m_qn_g, m_kn_g)
    wo = w_out.astype(BF16)
    fl = lambda t: t.reshape(b * s, t.shape[-1])
    return out_proj(xf, [fl(oa), fl(ob), fl(om)], [wo[:A_W], wo[A_W:A_W + B_W], wo[A_W + B_W:]])


def _odd_layer(xf, b, s, km, vm, tables, li, lb, ln_g, w_in, w_out, c_qn_g, c_kn_g, lqk, subln_g, d_gn_g,
               m_qn_g, m_kn_g):
    wb = w_in.astype(BF16)
    edges = [0]
    for width in (C_W, C_W, C_W, C_W, D_W, D_W, D_W, D_W, M_W, M_W):
        edges.append(edges[-1] + width)
    ws = [wb[:, edges[n]:edges[n + 1]] for n in range(10)]
    dts = [F32, F32, BF16, F32, F32, F32, F32, F32, F32, F32]
    qc, kc, vc, gc, qd, fd, idd, gd, qm, gm = rms_proj(xf, ln_g, ws, dts)
    sh = lambda t: t.reshape(b, s, t.shape[-1])
    q = qk_prep(sh(qc), c_qn_g, tables, BF16)
    k = qk_prep(sh(kc), c_kn_g, tables, BF16)
    lam_init = 0.8 - 0.6 * math.exp(-0.3 * li)
    oc = diff_attention(q, k, sh(vc), sh(gc), lqk, subln_g, lam_init)
    od = hgrn2(sh(qd), sh(fd), sh(idd), sh(gd), lb, d_gn_g)
    om = mem_attention(sh(qm), km, vm, sh(gm), m_qn_g, m_kn_g)
    wo = w_out.astype(BF16)
    fl = lambda t: t.reshape(b * s, t.shape[-1])
    return out_proj(xf, [fl(oc), fl(od), fl(om)], [wo[:C_W], wo[C_W:C_W + D_W], wo[C_W + D_W:]])


def kernel(x, mem, ln_g, mem_ln_g, w_mem_kv, m_qn_g, m_kn_g, e_w_in, e_w_out, a_qn_g, a_kn_g, b_mu, b_w0, b_w2, b_a0, b_a2, b_k_k, b_k_a, b_r_k, b_lnx_g, b_lnx_b, o_w_in, o_w_out, c_qn_g, c_kn_g, c_lq1, c_lk1, c_lq2, c_lk2, c_subln_g, d_lb, d_gn_g):
    b, s, d = x.shape
    depth = ln_g.shape[0]
    tables = rope_tables_lanes(s)
    lbs = jax.nn.softmax(d_lb.astype(F32), axis=0)
    lbs = jnp.cumsum(lbs, axis=0) - lbs[0:1]
    xf = x.reshape(b * s, d)
    memf = mem.reshape(b * N_MEM, d)
    for li in range(depth):
        j = li // 2
        km, vm = _memory_kv(memf, b, mem_ln_g[li], w_mem_kv[li])
        if li % 2 == 0:
            xf = _even_layer(xf, b, s, km, vm, tables, ln_g[li], e_w_in[j], e_w_out[j], a_qn_g[j], a_kn_g[j],
                             m_qn_g[li], m_kn_g[li], b_mu[j], b_w0[j], b_w2[j], b_a0[j], b_a2[j], b_k_k[j],
                             b_k_a[j], b_r_k[j], b_lnx_g[j], b_lnx_b[j])
        else:
            lqk = jnp.stack([c_lq1[j], c_lk1[j], c_lq2[j], c_lk2[j]]).astype(F32)
            xf = _odd_layer(xf, b, s, km, vm, tables, li, jnp.maximum(lbs[j], 0.0), ln_g[li], o_w_in[j],
                            o_w_out[j], c_qn_g[j], c_kn_g[j], lqk, c_subln_g[j], d_gn_g[j], m_qn_g[li], m_kn_g[li])
    return xf.reshape(b, s, d)
```

```python
import functools
import itertools
import math

import jax
import jax.numpy as jnp
from jax import lax
from jax.experimental import pallas as pl
from jax.experimental.pallas import tpu as pltpu

F32 = jnp.float32
BF16 = jnp.bfloat16
HIGHEST = lax.Precision.HIGHEST

N_MEM = 256
HEAD_DIM = 64
ROPE_THETA = 500000.0
ROPE_DIM = HEAD_DIM // 4
NORM_EPS = 1e-6
NEG_INF = -1e30
A_HEADS = 6
MOBA_BLOCK = 256
MOBA_TOPK = 3
B_HEADS = 6
B_LORA = 32
RWKV_GN_EPS = 64e-5
C_HEADS = 4
D_HEADS = 4
M_HEADS = 4
A_W = A_HEADS * HEAD_DIM
B_W = B_HEADS * HEAD_DIM
C_W = C_HEADS * 2 * HEAD_DIM
D_W = D_HEADS * HEAD_DIM
M_W = M_HEADS * HEAD_DIM
ATTN_SCALE = HEAD_DIM ** -0.5

LANES = 128
VMEM_LIMIT = 48 * 1024 * 1024

ROW_TILE = 256
ATTN_TILE = 256
CHUNK = 64
SUB = 16
SCAN_ROWS = 256

_NT = (((1,), (1,)), ((), ()))
_TN = (((0,), (0,)), ((), ()))


def _cparams(sem):
    return pltpu.CompilerParams(dimension_semantics=sem, vmem_limit_bytes=VMEM_LIMIT)


def _mm(a, b):
    return jnp.dot(a.astype(BF16), b.astype(BF16), preferred_element_type=F32)


def _mm_nt(a, b):
    return lax.dot_general(a.astype(BF16), b.astype(BF16), _NT, preferred_element_type=F32)


def _mm_tn(a, b):
    return lax.dot_general(a.astype(BF16), b.astype(BF16), _TN, preferred_element_type=F32)


def _mm_hi(a, b):
    return jnp.dot(a, b, precision=HIGHEST, preferred_element_type=F32)


def _silu(x):
    return x * jax.nn.sigmoid(x)


def _lane_lo(width=LANES):
    lane = lax.broadcasted_iota(jnp.int32, (1, width), 1)
    return (lane % LANES) < HEAD_DIM


def _head_mean_matrix(width):
    r = lax.broadcasted_iota(jnp.int32, (width, width), 0) // HEAD_DIM
    c = lax.broadcasted_iota(jnp.int32, (width, width), 1) // HEAD_DIM
    return jnp.where(r == c, 1.0 / HEAD_DIM, 0.0).astype(F32)


def _proj_body(x_ref, g_ref, *refs, n_out):
    x = x_ref[...]
    ms = jnp.mean(x * x, axis=-1, keepdims=True)
    h = (x * lax.rsqrt(ms + NORM_EPS) * g_ref[...]).astype(BF16)
    for w_ref, o_ref in zip(refs[:n_out], refs[n_out:]):
        o_ref[...] = jnp.dot(h, w_ref[...], preferred_element_type=F32).astype(o_ref.dtype)


def rms_proj(x2d, g, ws, out_dtypes):
    n, d = x2d.shape
    tm = min(ROW_TILE, n)
    in_specs = [pl.BlockSpec((tm, d), lambda i: (i, 0)), pl.BlockSpec((1, d), lambda i: (0, 0))]
    in_specs += [pl.BlockSpec(w.shape, lambda i: (0, 0)) for w in ws]
    out_specs = [pl.BlockSpec((tm, w.shape[1]), lambda i: (i, 0)) for w in ws]
    out_shape = [jax.ShapeDtypeStruct((n, w.shape[1]), dt) for w, dt in zip(ws, out_dtypes)]
    return pl.pallas_call(
        functools.partial(_proj_body, n_out=len(ws)),
        grid=(n // tm,), in_specs=in_specs, out_specs=out_specs, out_shape=out_shape,
        compiler_params=_cparams(("parallel",)), name="rms_proj",
    )(x2d, g.reshape(1, d).astype(F32), *ws)


def _out_proj_body(x_ref, *refs, n_in):
    o_ref = refs[-1]
    acc = x_ref[...]
    for m_ref, w_ref in zip(refs[:n_in], refs[n_in:2 * n_in]):
        acc = acc + jnp.dot(m_ref[...], w_ref[...], preferred_element_type=F32)
    o_ref[...] = acc


def out_proj(x2d, parts, ws):
    n, d = x2d.shape
    tm = min(ROW_TILE, n)
    in_specs = [pl.BlockSpec((tm, d), lambda i: (i, 0))]
    in_specs += [pl.BlockSpec((tm, p.shape[1]), lambda i: (i, 0)) for p in parts]
    in_specs += [pl.BlockSpec(w.shape, lambda i: (0, 0)) for w in ws]
    return pl.pallas_call(
        functools.partial(_out_proj_body, n_in=len(parts)),
        grid=(n // tm,), in_specs=in_specs, out_specs=pl.BlockSpec((tm, d), lambda i: (i, 0)),
        out_shape=jax.ShapeDtypeStruct((n, d), F32),
        compiler_params=_cparams(("parallel",)), name="out_proj",
    )(x2d, *parts, *ws)


def _qk_prep_body(x_ref, g_ref, c_ref, s1_ref, s2_ref, o_ref, *maybe_mean_ref):
    x = x_ref[0]
    ms = _mm_hi(x * x, _head_mean_matrix(LANES))
    y = x * lax.rsqrt(ms + NORM_EPS) * g_ref[...]
    yr = (y * c_ref[...] + pltpu.roll(y, LANES - ROPE_DIM // 2, 1) * s1_ref[...]
          + pltpu.roll(y, ROPE_DIM // 2, 1) * s2_ref[...])
    o_ref[0] = yr.astype(o_ref.dtype)
    if maybe_mean_ref:
        maybe_mean_ref[0][0] = jnp.mean(yr, axis=0, keepdims=True)


def qk_prep(x, gain, tables, out_dtype, want_block_mean=False):
    b, s, w = x.shape
    tm = ATTN_TILE
    cos_t, sin1_t, sin2_t = tables
    g = jnp.tile(gain.astype(F32), LANES // HEAD_DIM).reshape(1, LANES)
    tile_spec = pl.BlockSpec((1, tm, LANES), lambda bi, i, c: (bi, i, c))
    tab_spec = pl.BlockSpec((tm, LANES), lambda bi, i, c: (i, 0))
    out_specs = [tile_spec]
    out_shape = [jax.ShapeDtypeStruct((b, s, w), out_dtype)]
    if want_block_mean:
        nb = s // tm
        out_specs.append(pl.BlockSpec((1, 1, LANES), lambda bi, i, c: (bi * nb + i, 0, c)))
        out_shape.append(jax.ShapeDtypeStruct((b * nb, 1, w), F32))
    outs = pl.pallas_call(
        _qk_prep_body, grid=(b, s // tm, w // LANES),
        in_specs=[tile_spec, pl.BlockSpec((1, LANES), lambda bi, i, c: (0, 0)), tab_spec, tab_spec, tab_spec],
        out_specs=out_specs, out_shape=out_shape,
        compiler_params=_cparams(("parallel", "parallel", "parallel")), name="qk_prep",
    )(x, g, cos_t, sin1_t, sin2_t)
    if want_block_mean:
        return outs[0], outs[1].reshape(b, s // tm, w)
    return outs[0]


def rope_tables_lanes(seq):
    pos = jnp.arange(seq, dtype=F32)
    inv = 1.0 / (ROPE_THETA ** (jnp.arange(0, ROPE_DIM, 2, dtype=F32) / ROPE_DIM))
    ang = pos[:, None] * inv[None, :]
    cos, sin = jnp.cos(ang), jnp.sin(ang)
    half = ROPE_DIM // 2
    ones = jnp.ones((seq, HEAD_DIM - ROPE_DIM), F32)
    zeros_h = jnp.zeros((seq, half), F32)
    zeros_r = jnp.zeros((seq, HEAD_DIM - ROPE_DIM), F32)
    c = jnp.concatenate([cos, cos, ones], axis=1)
    s1 = jnp.concatenate([-sin, zeros_h, zeros_r], axis=1)
    s2 = jnp.concatenate([zeros_h, sin, zeros_r], axis=1)
    rep = LANES // HEAD_DIM
    return jnp.tile(c, (1, rep)), jnp.tile(s1, (1, rep)), jnp.tile(s2, (1, rep))


def _softmax_first(s, v):
    m = jnp.max(s, axis=1, keepdims=True)
    p = jnp.exp(s - m)
    return m, jnp.sum(p, axis=1, keepdims=True), _mm(p, v)


def _flash_t(i, k_ref, vt_ref, qm_ref, bufs, m_ref, l_ref, acc_ref, v_rows, bias_row):
    t = ATTN_TILE
    (s0, s1), (p0, p1), (a0, a1) = bufs
    last = jnp.maximum(i - 1, 0)

    def stage_a(x, s_w):
        kb = jnp.minimum(x - 1, last)
        k_blk = k_ref[0, pl.ds(pl.multiple_of(kb * t, t), t), :]
        for h in range(2):
            s = jnp.dot(k_blk, qm_ref[h], preferred_element_type=F32)
            if bias_row is not None:
                s = s + bias_row(h, kb)
            s_w[h] = s

    def stage_b(x, s_r, p_w, a_w):
        real = x <= i
        for h in range(2):
            for c in range(t // LANES):
                cs = slice(c * LANES, (c + 1) * LANES)
                s = s_r[h, :, cs]
                m_old = m_ref[h, :, cs]
                m_top = jnp.maximum(m_old, jnp.max(s, axis=0, keepdims=True))
                m_new = jnp.where(real, m_top, m_old)
                alpha = jnp.exp(m_old - m_new)
                p = jnp.exp(s - m_top)
                m_ref[h, :, cs] = m_new
                l_ref[h, :, cs] = alpha * l_ref[h, :, cs] + jnp.where(real, jnp.sum(p, axis=0, keepdims=True), 0.0)
                a_w[h, :, cs] = alpha
                p_w[h, :, cs] = p.astype(BF16)

    def stage_c(x, p_r, a_r):
        v_blk = vt_ref[0, 0, jnp.where(x == 0, i, jnp.minimum(x - 1, last))]
        v_blk = jnp.where(x <= i, v_blk, jnp.zeros_like(v_blk))
        for h in range(2):
            pv = jnp.dot(v_blk[v_rows(h)], p_r[h], preferred_element_type=F32)
            acc_ref[h] = a_r[h] * acc_ref[h] + pv

    m_ref[...] = jnp.full(m_ref.shape, NEG_INF, F32)
    l_ref[...] = jnp.zeros(l_ref.shape, F32)
    acc_ref[...] = jnp.zeros(acc_ref.shape, F32)
    k_own = k_ref[0, pl.ds(pl.multiple_of(i * t, t), t), :]
    causal = lax.broadcasted_iota(jnp.int32, (t, t), 0) <= lax.broadcasted_iota(jnp.int32, (t, t), 1)
    for h in range(2):
        s0[h] = jnp.where(causal, jnp.dot(k_own, qm_ref[h], preferred_element_type=F32), NEG_INF)
    stage_a(1, s1)
    stage_b(0, s0, p0, a0)

    def two_steps(u, carry):
        x = 2 * u + 2
        stage_c(x - 2, p0, a0)
        stage_b(x - 1, s1, p1, a1)
        stage_a(x, s0)
        stage_c(x - 1, p1, a1)
        stage_b(x, s0, p0, a0)
        stage_a(x + 1, s1)
        return carry

    lax.fori_loop(0, (i + 2) // 2, two_steps, 0)


def _flash_scratch(v_rows):
    t = ATTN_TILE
    return [pltpu.VMEM((2, LANES, t), BF16),
            pltpu.VMEM((2, t, t), F32), pltpu.VMEM((2, t, t), F32),
            pltpu.VMEM((2, t, t), BF16), pltpu.VMEM((2, t, t), BF16),
            pltpu.VMEM((2, 1, t), F32), pltpu.VMEM((2, 1, t), F32),
            pltpu.VMEM((2, 1, t), F32), pltpu.VMEM((2, 1, t), F32), pltpu.VMEM((2, v_rows, t), F32)]


def _blocked_t(v):
    b, s, w = v.shape
    t = ATTN_TILE
    return v.reshape(b, s // t, t, w // LANES, LANES).transpose(0, 3, 1, 4, 2)


def _moba_body(qt_ref, k_ref, vt_ref, km_ref, g_ref, o_ref, bias_ref, qm_ref, s0, s1, p0, p1, a0, a1,
               m_ref, l_ref, acc_ref, *, nb):
    i = pl.program_id(2)
    t = ATTN_TILE
    bufs = ((s0, s1), (p0, p1), (a0, a1))
    q_t = qt_ref[0]
    dim_lo = lax.broadcasted_iota(jnp.int32, (LANES, 1), 0) < HEAD_DIM
    lo = _lane_lo()
    km = km_ref[0]
    blk = lax.broadcasted_iota(jnp.int32, (nb, t), 0).astype(F32)
    own = i.astype(F32)
    for h in range(2):
        qm_ref[h] = (jnp.where(dim_lo if h == 0 else jnp.logical_not(dim_lo), q_t, 0.0) * ATTN_SCALE).astype(BF16)
        kmh = jnp.where(lo if h == 0 else jnp.logical_not(lo), km, 0.0)
        gate = jnp.dot(kmh, q_t, precision=HIGHEST, preferred_element_type=F32)
        gate = jnp.where(blk < own, gate, NEG_INF)
        sel = jnp.zeros((nb, t), F32)
        for _ in range(MOBA_TOPK):
            mx = jnp.max(gate, axis=0, keepdims=True)
            first = jnp.min(jnp.where(gate == mx, blk, float(nb)), axis=0, keepdims=True)
            hit = blk == first
            sel = jnp.where(jnp.logical_and(hit, first < own), 1.0, sel)
            gate = jnp.where(hit, -jnp.inf, gate)
        bias_ref[h] = jnp.where(sel > 0.0, 0.0, NEG_INF)

    _flash_t(i, k_ref, vt_ref, qm_ref, bufs, m_ref, l_ref, acc_ref,
             v_rows=lambda h: slice(h * HEAD_DIM, (h + 1) * HEAD_DIM),
             bias_row=lambda h, kb: bias_ref[h, pl.ds(kb, 1), :])
    o_t = jnp.concatenate([acc_ref[0] / l_ref[0], acc_ref[1] / l_ref[1]], axis=0)
    o_ref[0] = (o_t.T * _silu(g_ref[0])).astype(o_ref.dtype)


def moba_attention(q, k, v, kmean, gate):
    b, s, w = q.shape
    t = ATTN_TILE
    nb = s // t
    tile = pl.BlockSpec((1, t, LANES), lambda bi, p, i: (bi, i, p))
    return pl.pallas_call(
        functools.partial(_moba_body, nb=nb), grid=(b, w // LANES, nb),
        in_specs=[pl.BlockSpec((1, LANES, t), lambda bi, p, i: (bi, p, i)),
                  pl.BlockSpec((1, s, LANES), lambda bi, p, i: (bi, 0, p)),
                  pl.BlockSpec((1, 1, nb, LANES, t), lambda bi, p, i: (bi, p, 0, 0, 0)),
                  pl.BlockSpec((1, nb, LANES), lambda bi, p, i: (bi, 0, p)), tile],
        out_specs=tile, out_shape=jax.ShapeDtypeStruct((b, s, w), BF16),
        scratch_shapes=[pltpu.VMEM((2, nb, t), F32)] + _flash_scratch(HEAD_DIM),
        compiler_params=_cparams(("parallel", "parallel", "arbitrary")), name="moba_attention",
    )(jnp.swapaxes(q, 1, 2), k, _blocked_t(v), kmean, gate)


def _diff_body(qt_ref, k_ref, vt_ref, g_ref, lqk_ref, sg_ref, o_ref, qm_ref, s0, s1, p0, p1, a0, a1,
               m_ref, l_ref, acc_ref, *, lam_init):
    i = pl.program_id(2)
    q_t = qt_ref[0]
    dim_lo = lax.broadcasted_iota(jnp.int32, (LANES, 1), 0) < HEAD_DIM
    for h in range(2):
        qm_ref[h] = jnp.where(dim_lo if h == 0 else jnp.logical_not(dim_lo), q_t, 0) * ATTN_SCALE
    _flash_t(i, k_ref, vt_ref, qm_ref, ((s0, s1), (p0, p1), (a0, a1)), m_ref, l_ref, acc_ref,
             v_rows=lambda h: slice(0, LANES), bias_row=None)
    lqk = lqk_ref[...]
    lam = (jnp.exp(jnp.sum(lqk[0:1] * lqk[1:2], axis=1, keepdims=True))
           - jnp.exp(jnp.sum(lqk[2:3] * lqk[3:4], axis=1, keepdims=True)) + lam_init)
    o = (acc_ref[0] / l_ref[0] - lam * (acc_ref[1] / l_ref[1])).T
    ms = jnp.mean(o * o, axis=1, keepdims=True)
    o = o * lax.rsqrt(ms + NORM_EPS) * sg_ref[...] * (1.0 - lam_init)
    o_ref[0] = (o * _silu(g_ref[0])).astype(o_ref.dtype)


def diff_attention(q, k, v, gate, lqk, subln_g, lam_init):
    b, s, w = q.shape
    t = ATTN_TILE
    nb = s // t
    tile = pl.BlockSpec((1, t, LANES), lambda bi, h, i: (bi, i, h))
    return pl.pallas_call(
        functools.partial(_diff_body, lam_init=lam_init), grid=(b, w // LANES, nb),
        in_specs=[pl.BlockSpec((1, LANES, t), lambda bi, h, i: (bi, h, i)),
                  pl.BlockSpec((1, s, LANES), lambda bi, h, i: (bi, 0, h)),
                  pl.BlockSpec((1, 1, nb, LANES, t), lambda bi, h, i: (bi, h, 0, 0, 0)),
                  tile, pl.BlockSpec((4, HEAD_DIM), lambda bi, h, i: (0, 0)),
                  pl.BlockSpec((1, LANES), lambda bi, h, i: (0, 0))],
        out_specs=tile, out_shape=jax.ShapeDtypeStruct((b, s, w), BF16),
        scratch_shapes=_flash_scratch(LANES),
        compiler_params=_cparams(("parallel", "parallel", "arbitrary")), name="diff_attention",
    )(jnp.swapaxes(q, 1, 2), k, _blocked_t(v), gate, lqk, subln_g.reshape(1, LANES).astype(F32))


def _mem_body(q_ref, km_ref, vm_ref, g_ref, qg_ref, kg_ref, o_ref):
    gm = _head_mean_matrix(LANES)
    lo = _lane_lo()
    masks = (lo, jnp.logical_not(lo))
    q = q_ref[0]
    q = q * lax.rsqrt(_mm_hi(q * q, gm) + NORM_EPS) * qg_ref[...]
    k = km_ref[0]
    k = (k * lax.rsqrt(_mm_hi(k * k, gm) + NORM_EPS) * kg_ref[...]).astype(BF16)
    v = vm_ref[0]
    outs = []
    for h in range(2):
        qh = (jnp.where(masks[h], q, 0.0) * ATTN_SCALE).astype(BF16)
        m, l, acc = _softmax_first(_mm_nt(qh, k), v)
        outs.append(acc / l)
    out = jnp.where(lo, outs[0], outs[1])
    o_ref[0] = (out * _silu(g_ref[0])).astype(o_ref.dtype)


def mem_attention(q, km, vm, gate, q_gain, k_gain):
    b, s, w = q.shape
    t = ROW_TILE
    tile = pl.BlockSpec((1, t, LANES), lambda bi, p, i: (bi, i, p))
    mem = pl.BlockSpec((1, N_MEM, LANES), lambda bi, p, i: (bi, 0, p))
    gain = pl.BlockSpec((1, LANES), lambda bi, p, i: (0, 0))
    rep = LANES // HEAD_DIM
    return pl.pallas_call(
        _mem_body, grid=(b, w // LANES, s // t),
        in_specs=[tile, mem, mem, tile, gain, gain],
        out_specs=tile, out_shape=jax.ShapeDtypeStruct((b, s, w), BF16),
        compiler_params=_cparams(("parallel", "parallel", "parallel")), name="mem_attention",
    )(q, km, vm, gate, jnp.tile(q_gain.astype(F32), rep).reshape(1, LANES),
      jnp.tile(k_gain.astype(F32), rep).reshape(1, LANES))


def _shift(cur, prev8, mu, first):
    rows = lax.broadcasted_iota(jnp.int32, cur.shape, 0)
    before = jnp.where(first, 0.0, prev8[7:8, :])
    prev = jnp.where(rows == 0, before, pltpu.roll(cur, 1, 0))
    return cur + (prev - cur) * mu


def _rwkv_prep_body(r_ref, k_ref, v_ref, lo_ref, rp_ref, kp_ref, vp_ref, lp_ref, mu_ref, mul_ref, w0_ref, w2_ref,
                    a0_ref, a2_ref, kk_ref, ka_ref, rk_ref,
                    ro_ref, lw_ref, ko_ref, vo_ref, kn_ref, ao_ref, bo_ref):
    first = pl.program_id(1) == 0
    mu = mu_ref[...]
    r = _shift(r_ref[0], rp_ref[0], mu[0:1], first)
    k = _shift(k_ref[0], kp_ref[0], mu[1:2], first)
    v = _shift(v_ref[0], vp_ref[0], mu[2:3], first)
    lora = _shift(lo_ref[0], lp_ref[0], mul_ref[...], first)
    z = w0_ref[...] + _mm_hi(jnp.tanh(lora), w2_ref[...])
    w_log = -(jnp.maximum(-z, 0.0) + jnp.log1p(jnp.exp(-jnp.abs(z)))) - 0.5
    a = jax.nn.sigmoid(a0_ref[...] + _mm_hi(lora, a2_ref[...]))
    kk = k * kk_ref[...]
    k2 = k * (1.0 + (a - 1.0) * ka_ref[...])
    ones = _head_mean_matrix(B_W) * float(HEAD_DIM)
    kk = kk * lax.rsqrt(jnp.maximum(_mm_hi(kk * kk, ones), 1e-24))
    ro_ref[0] = r
    lw_ref[0] = -jnp.exp(w_log)
    ko_ref[0] = k2
    vo_ref[0] = v
    kn_ref[0] = kk
    ao_ref[0] = a
    bo_ref[0] = _mm_hi(r * k2 * rk_ref[...], ones) * v


def rwkv_prep(r, k, v, lora, mu, w0, w2, a0, a2, k_k, k_a, r_k):
    b, s, w = r.shape
    t = ROW_TILE
    tile = pl.BlockSpec((1, t, w), lambda bi, i: (bi, i, 0))
    ltile = pl.BlockSpec((1, t, LANES), lambda bi, i: (bi, i, 0))
    prev_idx = lambda bi, i: (bi, jnp.maximum(i * (t // 8) - 1, 0), 0)
    ptile = pl.BlockSpec((1, 8, w), prev_idx)
    pltile = pl.BlockSpec((1, 8, LANES), prev_idx)
    const = lambda shape: pl.BlockSpec(shape, lambda bi, i: (0, 0))
    mu3 = jnp.stack([mu[:w], mu[w:2 * w], mu[2 * w:3 * w]]).astype(F32)
    mul = jnp.zeros((1, LANES), F32).at[0, :2 * B_LORA].set(mu[3 * w:])
    w2p = jnp.zeros((LANES, w), F32).at[:B_LORA].set(w2)
    a2p = jnp.zeros((LANES, w), F32).at[B_LORA:2 * B_LORA].set(a2)
    row = lambda p: p.reshape(1, w).astype(F32)
    return pl.pallas_call(
        _rwkv_prep_body, grid=(b, s // t),
        in_specs=[tile, tile, tile, ltile, ptile, ptile, ptile, pltile, const((3, w)), const((1, LANES)),
                  const((1, w)), const((LANES, w)), const((1, w)), const((LANES, w)), const((1, w)), const((1, w)),
                  const((1, w))],
        out_specs=[tile] * 7, out_shape=[jax.ShapeDtypeStruct((b, s, w), F32)] * 7,
        compiler_params=_cparams(("parallel", "parallel")), name="rwkv_prep",
    )(r, k, v, lora, r, k, v, lora, mu3, mul, row(w0), w2p, row(a0), a2p, row(k_k), row(k_a), row(r_k))


def _tri(n, strict):
    r = lax.broadcasted_iota(jnp.int32, (n, n), 0)
    c = lax.broadcasted_iota(jnp.int32, (n, n), 1)
    return (c < r) if strict else (c <= r)


def _block_diag_mask():
    r = lax.broadcasted_iota(jnp.int32, (LANES, LANES), 0) // HEAD_DIM
    c = lax.broadcasted_iota(jnp.int32, (LANES, LANES), 1) // HEAD_DIM
    return r == c


def _rwkv_scan_body(r_ref, lw_ref, k_ref, v_ref, kn_ref, a_ref, bo_ref, g_ref, lg_ref, lb_ref, o_ref, s_ref, *,
                    nbatch, npair):
    @pl.when(pl.program_id(0) == 0)
    def _():
        s_ref[...] = jnp.zeros_like(s_ref)

    c = CHUNK
    lo = _lane_lo()
    masks = (lo, jnp.logical_not(lo))
    strict = _tri(c, True)
    incl = _tri(c, False)
    ltri = incl.astype(F32)
    eye = (lax.broadcasted_iota(jnp.int32, (c, c), 0) == lax.broadcasted_iota(jnp.int32, (c, c), 1)).astype(F32)
    bd = _block_diag_mask()
    gmean = _head_mean_matrix(LANES)

    def chain(sl, bi, pi):
        cols = slice(pi * LANES, (pi + 1) * LANES)
        idx = bi * npair + pi
        r, lw, k, v, kn, a = (ref[bi, sl, cols] for ref in (r_ref, lw_ref, k_ref, v_ref, kn_ref, a_ref))
        cw = _mm_hi(ltri, lw)
        yield
        cl = cw[c - 1:c, :]
        at = -kn * jnp.exp(cw - lw)
        e_neg = jnp.exp(-cw)
        bt = kn * a * e_neg
        kt = k * e_neg
        rt = r * jnp.exp(cw)
        e_end = jnp.exp(cl - cw)
        s0 = s_ref[idx]
        rhs = _mm_nt(at, s0)
        ys0 = _mm_nt(rt, s0)
        yield
        ah = [jnp.where(mk, at, 0.0) for mk in masks]
        rh = [jnp.where(mk, rt, 0.0) for mk in masks]
        n = [jnp.where(strict, _mm_nt(x, bt), 0.0) for x in ah]
        yield
        aak = [jnp.where(strict, _mm_nt(x, kt), 0.0) for x in ah]
        yield
        arb = [jnp.where(incl, _mm_nt(x, bt), 0.0) for x in rh]
        yield
        ark = [jnp.where(incl, _mm_nt(x, kt), 0.0) for x in rh]
        yield
        xs = [rhs + _mm(x, v) for x in aak]
        yv = [_mm(x, v) for x in ark]
        yield
        tinv = [eye + x for x in n]
        p = n
        for _ in range(5):
            p = [_mm(x, x) for x in p]
            yield
            tinv = [x + _mm(x, y) for x, y in zip(tinv, p)]
            yield
        u = jnp.where(lo, _mm(tinv[0], xs[0]), _mm(tinv[1], xs[1]))
        yield
        y = ys0 + jnp.where(lo, _mm(arb[0], u) + yv[0], _mm(arb[1], u) + yv[1])
        upd = _mm_tn(u, kn * a * e_end) + _mm_tn(v, k * e_end)
        yield
        s_ref[idx] = s0 * jnp.exp(cl) + jnp.where(bd, upd, 0.0)
        mean = _mm_hi(y, gmean)
        yield
        d = y - mean
        var = _mm_hi(d * d, gmean)
        yield
        yn = d * lax.rsqrt(var + RWKV_GN_EPS) * lg_ref[:, cols] + lb_ref[:, cols] + bo_ref[bi, sl, cols]
        o_ref[bi, sl, cols] = (yn * _silu(g_ref[bi, sl, cols])).astype(o_ref.dtype)

    def chunk(ci, carry):
        sl = pl.ds(pl.multiple_of(ci * c, c), c)
        gens = [chain(sl, bi, pi) for bi in range(nbatch) for pi in range(npair)]
        for _ in itertools.zip_longest(*gens):
            pass
        return carry

    lax.fori_loop(0, SCAN_ROWS // c, chunk, 0)


def rwkv_scan(r, lw, k, v, kn, a, bonus, gate, lnx_g, lnx_b):
    b, s, w = r.shape
    t = SCAN_ROWS
    tile = pl.BlockSpec((b, t, w), lambda i: (0, i, 0))
    vec = pl.BlockSpec((1, w), lambda i: (0, 0))
    return pl.pallas_call(
        functools.partial(_rwkv_scan_body, nbatch=b, npair=w // LANES), grid=(s // t,),
        in_specs=[tile] * 8 + [vec, vec],
        out_specs=tile, out_shape=jax.ShapeDtypeStruct((b, s, w), BF16),
        scratch_shapes=[pltpu.VMEM((b * (w // LANES), LANES, LANES), F32)],
        compiler_params=_cparams(("arbitrary",)), name="rwkv_scan",
    )(r, lw, k, v, kn, a, bonus, gate, lnx_g.reshape(1, w).astype(F32), lnx_b.reshape(1, w).astype(F32))


def _hgrn_body(q_ref, f_ref, i_ref, g_ref, lb_ref, gn_ref, o_ref, s_ref, *, nbatch, npair):
    @pl.when(pl.program_id(0) == 0)
    def _():
        s_ref[...] = jnp.zeros_like(s_ref)

    c = CHUNK
    lo = _lane_lo()
    masks = (lo, jnp.logical_not(lo))
    ltri = _tri(c, False).astype(F32)
    bd = _block_diag_mask()
    gmean = _head_mean_matrix(LANES)
    head_ones = gmean * float(HEAD_DIM)
    rows = lax.broadcasted_iota(jnp.int32, (SUB, 1), 0)

    def chain(sl, bi, pi):
        cols = slice(pi * LANES, (pi + 1) * LANES)
        idx = bi * npair + pi
        lb = lb_ref[:, cols]
        log_lb = jnp.log(lb)
        log_1m = jnp.log1p(-lb)
        q, fr, v = q_ref[bi, sl, cols], f_ref[bi, sl, cols], i_ref[bi, sl, cols]
        log_sig = jnp.minimum(fr, 0.0) - jnp.log1p(jnp.exp(-jnp.abs(fr)))
        z = log_1m + log_sig
        hi = jnp.maximum(log_lb, z)
        log_f = hi + jnp.log1p(jnp.exp(-jnp.abs(log_lb - z)))
        k = (1.0 - lb) * jax.nn.sigmoid(-fr)
        bc = _mm_hi(ltri, log_f)
        yield
        bl = bc[c - 1:c, :]
        s0 = s_ref[idx]
        outs = []
        for sb in range(c // SUB):
            r0 = sb * SUB
            qs, bs, ks, vs = (x[r0:r0 + SUB] for x in (q, bc, k, v))
            o = _mm_nt(qs * jnp.exp(bs), s0)
            if sb > 0:
                ref = bc[r0 - 1:r0, :]
                qh = qs * jnp.exp(bs - ref)
                kh = k[:r0] * jnp.exp(ref - bc[:r0])
                vh = v[:r0]
                sc = [_mm_nt(jnp.where(mk, qh, 0.0), kh) for mk in masks]
                yield
                o = o + jnp.where(lo, _mm(sc[0], vh), _mm(sc[1], vh))
            pair = jnp.concatenate(
                [qs * ks[si:si + 1, :] * jnp.exp(jnp.where(rows >= si, bs - bs[si:si + 1, :], NEG_INF))
                 for si in range(SUB)], axis=0)
            score = _mm_hi(pair, head_ones)
            yield
            for si in range(SUB):
                o = o + score[si * SUB:(si + 1) * SUB] * vs[si:si + 1, :]
            outs.append(o)
        od = jnp.concatenate(outs, axis=0)
        s_ref[idx] = s0 * jnp.exp(bl) + jnp.where(bd, _mm_tn(v, k * jnp.exp(bl - bc)), 0.0)
        ms = _mm_hi(od * od, gmean)
        yield
        on = od * lax.rsqrt(ms + NORM_EPS) * gn_ref[...]
        o_ref[bi, sl, cols] = (on * _silu(g_ref[bi, sl, cols])).astype(o_ref.dtype)

    def chunk(ci, carry):
        sl = pl.ds(pl.multiple_of(ci * c, c), c)
        gens = [chain(sl, bi, pi) for bi in range(nbatch) for pi in range(npair)]
        for _ in itertools.zip_longest(*gens):
            pass
        return carry

    lax.fori_loop(0, SCAN_ROWS // c, chunk, 0)


def hgrn2(q, f, iv, gate, lb, gn_g):
    b, s, w = q.shape
    t = SCAN_ROWS
    tile = pl.BlockSpec((b, t, w), lambda i: (0, i, 0))
    rep = LANES // HEAD_DIM
    return pl.pallas_call(
        functools.partial(_hgrn_body, nbatch=b, npair=w // LANES), grid=(s // t,),
        in_specs=[tile] * 4 + [pl.BlockSpec((1, w), lambda i: (0, 0)), pl.BlockSpec((1, LANES), lambda i: (0, 0))],
        out_specs=tile, out_shape=jax.ShapeDtypeStruct((b, s, w), BF16),
        scratch_shapes=[pltpu.VMEM((b * (w // LANES), LANES, LANES), F32)],
        compiler_params=_cparams(("arbitrary",)), name="hgrn2",
    )(q, f, iv, gate, lb.reshape(1, w).astype(F32), jnp.tile(gn_g.astype(F32), rep).reshape(1, LANES))


def _memory_kv(memf, b, g, w_kv):
    wb = w_kv.astype(BF16)
    km, vm = rms_proj(memf, g, [wb[:, :M_W], wb[:, M_W:]], [F32, BF16])
    return km.reshape(b, N_MEM, M_W), vm.reshape(b, N_MEM, M_W)


def _even_layer(xf, b, s, km, vm, tables, ln_g, w_in, w_out, a_qn_g, a_kn_g, m_qn_g, m_kn_g,
                mu, w0, w2, a0, a2, k_k, k_a, r_k, lnx_g, lnx_b):
    wb = w_in.astype(BF16)
    edges = [0]
    for width in (A_W, A_W, A_W, A_W, B_W, B_W, B_W, 2 * B_LORA, B_W, M_W, M_W):
        edges.append(edges[-1] + width)
    ws = [wb[:, edges[n]:edges[n + 1]] for n in range(11)]
    ws[7] = jnp.pad(ws[7], ((0, 0), (0, LANES - 2 * B_LORA)))
    dts = [F32, F32, BF16, F32, F32, F32, F32, F32, F32, F32, F32]
    qa, ka, va, ga, rr, rk, rv, lora, gb, qm, gm = rms_proj(xf, ln_g, ws, dts)
    sh = lambda t: t.reshape(b, s, t.shape[-1])
    q = qk_prep(sh(qa), a_qn_g, tables, F32)
    k, kmean = qk_prep(sh(ka), a_kn_g, tables, BF16, want_block_mean=True)
    oa = moba_attention(q, k, sh(va), kmean, sh(ga))
    pre = rwkv_prep(sh(rr), sh(rk), sh(rv), sh(lora), mu, w0, w2, a0, a2, k_k, k_a, r_k.reshape(-1))
    ob = rwkv_scan(*pre, sh(gb), lnx_g, lnx_b)
    om = mem_attention(sh(qm), km, vm, sh(gm), m_qn_g, m_kn_g)
    wo = w_out.astype(BF16)
    fl = lambda t: t.reshape(b * s, t.shape[-1])
    return out_proj(xf, [fl(oa), fl(ob), fl(om)], [wo[:A_W], wo[A_W:A_W + B_W], wo[A_W + B_W:]])


def _odd_layer(xf, b, s, km, vm, tables, li, lb, ln_g, w_in, w_out, c_qn_g, c_kn_g, lqk, subln_g, d_gn_g,
               m_qn_g, m_kn_g):
    wb = w_in.astype(BF16)
    edges = [0]
    for width in (C_W, C_W, C_W, C_W, D_W, D_W, D_W, D_W, M_W, M_W):
        edges.append(edges[-1] + width)
    ws = [wb[:, edges[n]:edges[n + 1]] for n in range(10)]
    dts = [F32, F32, BF16, F32, F32, F32, F32, F32, F32, F32]
    qc, kc, vc, gc, qd, fd, idd, gd, qm, gm = rms_proj(xf, ln_g, ws, dts)
    sh = lambda t: t.reshape(b, s, t.shape[-1])
    q = qk_prep(sh(qc), c_qn_g, tables, BF16)
    k = qk_prep(sh(kc), c_kn_g, tables, BF16)
    lam_init = 0.8 - 0.6 * math.exp(-0.3 * li)
    oc = diff_attention(q, k, sh(vc), sh(gc), lqk, subln_g, lam_init)
    od = hgrn2(sh(qd), sh(fd), sh(idd), sh(gd), lb, d_gn_g)
    om = mem_attention(sh(qm), km, vm, sh(gm), m_qn_g, m_kn_g)
    wo = w_out.astype(BF16)
    fl = lambda t: t.reshape(b * s, t.shape[-1])
    return out_proj(xf, [fl(oc), fl(od), fl(om)], [wo[:C_W], wo[C_W:C_W + D_W], wo[C_W + D_W:]])


def kernel(x, mem, ln_g, mem_ln_g, w_mem_kv, m_qn_g, m_kn_g, e_w_in, e_w_out, a_qn_g, a_kn_g, b_mu, b_w0, b_w2, b_a0, b_a2, b_k_k, b_k_a, b_r_k, b_lnx_g, b_lnx_b, o_w_in, o_w_out, c_qn_g, c_kn_g, c_lq1, c_lk1, c_lq2, c_lk2, c_subln_g, d_lb, d_gn_g):
    b, s, d = x.shape
    depth = ln_g.shape[0]
    tables = rope_tables_lanes(s)
    lbs = jax.nn.softmax(d_lb.astype(F32), axis=0)
    lbs = jnp.cumsum(lbs, axis=0) - lbs[0:1]
    xf = x.reshape(b * s, d)
    memf = mem.reshape(b * N_MEM, d)
    for li in range(depth):
        j = li // 2
        km, vm = _memory_kv(memf, b, mem_ln_g[li], w_mem_kv[li])
        if li % 2 == 0:
            xf = _even_layer(xf, b, s, km, vm, tables, ln_g[li], e_w_in[j], e_w_out[j], a_qn_g[j], a_kn_g[j],
                             m_qn_g[li], m_kn_g[li], b_mu[j], b_w0[j], b_w2[j], b_a0[j], b_a2[j], b_k_k[j],
                             b_k_a[j], b_r_k[j], b_lnx_g[j], b_lnx_b[j])
        else:
            lqk = jnp.stack([c_lq1[j], c_lk1[j], c_lq2[j], c_lk2[j]]).astype(F32)
            xf = _odd_layer(xf, b, s, km, vm, tables, li, jnp.maximum(lbs[j], 0.0), ln_g[li], o_w_in[j],
                            o_w_out[j], c_qn_g[j], c_kn_g[j], lqk, c_subln_g[j], d_gn_g[j], m_qn_g[li], m_kn_g[li])
    return xf.reshape(b, s, d)
```

```python
import functools
import itertools
import math

import jax
import jax.numpy as jnp
from jax import lax
from jax.experimental import pallas as pl
from jax.experimental.pallas import tpu as pltpu

F32 = jnp.float32
BF16 = jnp.bfloat16
HIGHEST = lax.Precision.HIGHEST

N_MEM = 256
HEAD_DIM = 64
ROPE_THETA = 500000.0
ROPE_DIM = HEAD_DIM // 4
NORM_EPS = 1e-6
NEG_INF = -1e30
A_HEADS = 6
MOBA_BLOCK = 256
MOBA_TOPK = 3
B_HEADS = 6
B_LORA = 32
RWKV_GN_EPS = 64e-5
C_HEADS = 4
D_HEADS = 4
M_HEADS = 4
A_W = A_HEADS * HEAD_DIM
B_W = B_HEADS * HEAD_DIM
C_W = C_HEADS * 2 * HEAD_DIM
D_W = D_HEADS * HEAD_DIM
M_W = M_HEADS * HEAD_DIM
ATTN_SCALE = HEAD_DIM ** -0.5

LANES = 128
VMEM_LIMIT = 48 * 1024 * 1024

ROW_TILE = 256
PREP_ROWS = 512
LOG2E = math.log2(math.e)
ONES_ROWS = 16
ATTN_TILE = 256
CHUNK = 64
SUB = 16
SCAN_ROWS = 256

_NT = (((1,), (1,)), ((), ()))
_TN = (((0,), (0,)), ((), ()))


def _cparams(sem):
    return pltpu.CompilerParams(dimension_semantics=sem, vmem_limit_bytes=VMEM_LIMIT)


def _mm(a, b):
    return jnp.dot(a.astype(BF16), b.astype(BF16), preferred_element_type=F32)


def _mm_nt(a, b):
    return lax.dot_general(a.astype(BF16), b.astype(BF16), _NT, preferred_element_type=F32)


def _mm_tn(a, b):
    return lax.dot_general(a.astype(BF16), b.astype(BF16), _TN, preferred_element_type=F32)


def _mm_hi(a, b):
    return jnp.dot(a, b, precision=HIGHEST, preferred_element_type=F32)


def _split3(x):
    x1 = x.astype(BF16)
    r1 = x - x1.astype(F32)
    x2 = r1.astype(BF16)
    x3 = (r1 - x2.astype(F32)).astype(BF16)
    return x1, x2, x3


def _mm_sel_r(x, sel):
    sb = sel.astype(BF16)
    x1, x2, x3 = _split3(x)
    return (jnp.dot(x3, sb, preferred_element_type=F32) + jnp.dot(x2, sb, preferred_element_type=F32)
            + jnp.dot(x1, sb, preferred_element_type=F32))


def _mm_sel_l(sel, x):
    sb = sel.astype(BF16)
    x1, x2, x3 = _split3(x)
    return (jnp.dot(sb, x3, preferred_element_type=F32) + jnp.dot(sb, x2, preferred_element_type=F32)
            + jnp.dot(sb, x1, preferred_element_type=F32))


def _silu(x):
    return x * jax.nn.sigmoid(x)


def _lane_lo(width=LANES):
    lane = lax.broadcasted_iota(jnp.int32, (1, width), 1)
    return (lane % LANES) < HEAD_DIM


def _head_mean_matrix(width):
    r = lax.broadcasted_iota(jnp.int32, (width, width), 0) // HEAD_DIM
    c = lax.broadcasted_iota(jnp.int32, (width, width), 1) // HEAD_DIM
    return jnp.where(r == c, 1.0 / HEAD_DIM, 0.0).astype(F32)


def _proj_body(x_ref, g_ref, *refs, n_out):
    x = x_ref[...]
    ms = jnp.mean(x * x, axis=-1, keepdims=True)
    h = (x * lax.rsqrt(ms + NORM_EPS) * g_ref[...]).astype(BF16)
    for w_ref, o_ref in zip(refs[:n_out], refs[n_out:]):
        o_ref[...] = jnp.dot(h, w_ref[...], preferred_element_type=F32).astype(o_ref.dtype)


def rms_proj(x2d, g, ws, out_dtypes):
    n, d = x2d.shape
    tm = min(ROW_TILE, n)
    in_specs = [pl.BlockSpec((tm, d), lambda i: (i, 0)), pl.BlockSpec((1, d), lambda i: (0, 0))]
    in_specs += [pl.BlockSpec(w.shape, lambda i: (0, 0)) for w in ws]
    out_specs = [pl.BlockSpec((tm, w.shape[1]), lambda i: (i, 0)) for w in ws]
    out_shape = [jax.ShapeDtypeStruct((n, w.shape[1]), dt) for w, dt in zip(ws, out_dtypes)]
    return pl.pallas_call(
        functools.partial(_proj_body, n_out=len(ws)),
        grid=(n // tm,), in_specs=in_specs, out_specs=out_specs, out_shape=out_shape,
        compiler_params=_cparams(("parallel",)), name="rms_proj",
    )(x2d, g.reshape(1, d).astype(F32), *ws)


def _out_proj_body(x_ref, *refs, n_in):
    o_ref = refs[-1]
    acc = x_ref[...]
    for m_ref, w_ref in zip(refs[:n_in], refs[n_in:2 * n_in]):
        acc = acc + jnp.dot(m_ref[...], w_ref[...], preferred_element_type=F32)
    o_ref[...] = acc


def out_proj(x2d, parts, ws):
    n, d = x2d.shape
    tm = min(ROW_TILE, n)
    in_specs = [pl.BlockSpec((tm, d), lambda i: (i, 0))]
    in_specs += [pl.BlockSpec((tm, p.shape[1]), lambda i: (i, 0)) for p in parts]
    in_specs += [pl.BlockSpec(w.shape, lambda i: (0, 0)) for w in ws]
    return pl.pallas_call(
        functools.partial(_out_proj_body, n_in=len(parts)),
        grid=(n // tm,), in_specs=in_specs, out_specs=pl.BlockSpec((tm, d), lambda i: (i, 0)),
        out_shape=jax.ShapeDtypeStruct((n, d), F32),
        compiler_params=_cparams(("parallel",)), name="out_proj",
    )(x2d, *parts, *ws)


def _qk_prep_body(x_ref, g_ref, c_ref, s1_ref, s2_ref, o_ref, *maybe_mean_ref, scale):
    gm = _head_mean_matrix(LANES)
    cos, sin1, sin2 = c_ref[...], s1_ref[...], s2_ref[...]
    for c in range(x_ref.shape[2] // LANES):
        cols = slice(c * LANES, (c + 1) * LANES)
        x = x_ref[0, :, cols]
        y = x * lax.rsqrt(_mm_sel_r(x * x, gm) + NORM_EPS) * g_ref[...]
        yr = y * cos + pltpu.roll(y, LANES - ROPE_DIM // 2, 1) * sin1 + pltpu.roll(y, ROPE_DIM // 2, 1) * sin2
        o_ref[0, :, cols] = (yr * scale).astype(o_ref.dtype)
        if maybe_mean_ref:
            for rb in range(PREP_ROWS // MOBA_BLOCK):
                blk = yr[rb * MOBA_BLOCK:(rb + 1) * MOBA_BLOCK]
                maybe_mean_ref[0][rb, :, cols] = jnp.mean(blk, axis=0, keepdims=True)


def qk_prep(x, gain, tables, out_dtype, scale=1.0, want_block_mean=False):
    b, s, w = x.shape
    tm = PREP_ROWS
    per = tm // MOBA_BLOCK
    cos_t, sin1_t, sin2_t = tables
    g = jnp.tile(gain.astype(F32), LANES // HEAD_DIM).reshape(1, LANES)
    tile_spec = pl.BlockSpec((1, tm, w), lambda bi, i: (bi, i, 0))
    tab_spec = pl.BlockSpec((tm, LANES), lambda bi, i: (i, 0))
    out_specs = [tile_spec]
    out_shape = [jax.ShapeDtypeStruct((b, s, w), out_dtype)]
    if want_block_mean:
        nt = s // tm
        out_specs.append(pl.BlockSpec((per, 1, w), lambda bi, i: (bi * nt + i, 0, 0)))
        out_shape.append(jax.ShapeDtypeStruct((b * nt * per, 1, w), F32))
    outs = pl.pallas_call(
        functools.partial(_qk_prep_body, scale=scale), grid=(b, s // tm),
        in_specs=[tile_spec, pl.BlockSpec((1, LANES), lambda bi, i: (0, 0)), tab_spec, tab_spec, tab_spec],
        out_specs=out_specs, out_shape=out_shape,
        compiler_params=_cparams(("parallel", "parallel")), name="qk_prep",
    )(x, g, cos_t, sin1_t, sin2_t)
    if want_block_mean:
        return outs[0], outs[1].reshape(b, s // MOBA_BLOCK, w)
    return outs[0]


def rope_tables_lanes(seq):
    pos = jnp.arange(seq, dtype=F32)
    inv = 1.0 / (ROPE_THETA ** (jnp.arange(0, ROPE_DIM, 2, dtype=F32) / ROPE_DIM))
    ang = pos[:, None] * inv[None, :]
    cos, sin = jnp.cos(ang), jnp.sin(ang)
    half = ROPE_DIM // 2
    ones = jnp.ones((seq, HEAD_DIM - ROPE_DIM), F32)
    zeros_h = jnp.zeros((seq, half), F32)
    zeros_r = jnp.zeros((seq, HEAD_DIM - ROPE_DIM), F32)
    c = jnp.concatenate([cos, cos, ones], axis=1)
    s1 = jnp.concatenate([-sin, zeros_h, zeros_r], axis=1)
    s2 = jnp.concatenate([zeros_h, sin, zeros_r], axis=1)
    rep = LANES // HEAD_DIM
    return jnp.tile(c, (1, rep)), jnp.tile(s1, (1, rep)), jnp.tile(s2, (1, rep))


def _softmax_first(s, v):
    m = jnp.max(s, axis=1, keepdims=True)
    p = jnp.exp(s - m)
    return m, jnp.sum(p, axis=1, keepdims=True), _mm(p, v)


def _flash_t(i, k_ref, vt_ref, qm_ref, bufs, m_ref, acc_ref, v_rows, bias_row):
    t = ATTN_TILE
    (s0, s1), (p0, p1), (a0, a1) = bufs
    last = jnp.maximum(i - 1, 0)

    def stage_a(x, s_w):
        kb = jnp.minimum(x - 1, last)
        k_blk = k_ref[0, pl.ds(pl.multiple_of(kb * t, t), t), :]
        for h in range(2):
            s = jnp.dot(k_blk, qm_ref[h], preferred_element_type=F32)
            if bias_row is not None:
                s = s + bias_row(h, kb)
            s_w[h] = s

    def stage_b(x, s_r, p_w, a_w):
        real = x <= i
        for h in range(2):
            for c in range(t // LANES):
                cs = slice(c * LANES, (c + 1) * LANES)
                s = s_r[h, :, cs]
                m_old = m_ref[h, :, cs]
                m_top = jnp.maximum(m_old, jnp.max(s, axis=0, keepdims=True))
                m_new = jnp.where(real, m_top, m_old)
                m_ref[h, :, cs] = m_new
                a_w[h, :, cs] = jnp.exp2(m_old - m_new)
                p_w[h, :, cs] = jnp.exp2(s - m_top).astype(BF16)

    def stage_c(x, p_r, a_r):
        v_blk = vt_ref[0, 0, jnp.where(x == 0, i, jnp.minimum(x - 1, last))]
        v_blk = jnp.where(x <= i, v_blk, jnp.zeros_like(v_blk))
        for h in range(2):
            pv = jnp.dot(v_blk[v_rows(h)], p_r[h], preferred_element_type=F32)
            acc_ref[h] = a_r[h] * acc_ref[h] + pv

    m_ref[...] = jnp.full(m_ref.shape, NEG_INF, F32)
    acc_ref[...] = jnp.zeros(acc_ref.shape, F32)
    k_own = k_ref[0, pl.ds(pl.multiple_of(i * t, t), t), :]
    causal = lax.broadcasted_iota(jnp.int32, (t, t), 0) <= lax.broadcasted_iota(jnp.int32, (t, t), 1)
    for h in range(2):
        s0[h] = jnp.where(causal, jnp.dot(k_own, qm_ref[h], preferred_element_type=F32), NEG_INF)
    stage_a(1, s1)
    stage_b(0, s0, p0, a0)

    def two_steps(u, carry):
        x = 2 * u + 2
        stage_c(x - 2, p0, a0)
        stage_b(x - 1, s1, p1, a1)
        stage_a(x, s0)
        stage_c(x - 1, p1, a1)
        stage_b(x, s0, p0, a0)
        stage_a(x + 1, s1)
        return carry

    lax.fori_loop(0, (i + 2) // 2, two_steps, 0)


def _flash_scratch(v_rows):
    t = ATTN_TILE
    return [pltpu.VMEM((2, LANES, t), BF16),
            pltpu.VMEM((2, t, t), F32), pltpu.VMEM((2, t, t), F32),
            pltpu.VMEM((2, t, t), BF16), pltpu.VMEM((2, t, t), BF16),
            pltpu.VMEM((2, 1, t), F32), pltpu.VMEM((2, 1, t), F32),
            pltpu.VMEM((2, 1, t), F32), pltpu.VMEM((2, v_rows + ONES_ROWS, t), F32)]


def _blocked_t(v, group):
    b, s, w = v.shape
    t = ATTN_TILE
    v_t = v.reshape(b, s // t, t, w // LANES, LANES // group, group).transpose(0, 3, 1, 4, 5, 2)
    ones = jnp.ones(v_t.shape[:4] + (ONES_ROWS, t), v.dtype)
    v_t = jnp.concatenate([v_t, ones], axis=4)
    return v_t.reshape(b, w // LANES, s // t, (LANES // group) * (group + ONES_ROWS), t)


def _moba_body(qt_ref, k_ref, vt_ref, km_ref, g_ref, o_ref, bias_ref, qm_ref, s0, s1, p0, p1, a0, a1,
               m_ref, acc_ref, *, nb):
    i = pl.program_id(2)
    t = ATTN_TILE
    bufs = ((s0, s1), (p0, p1), (a0, a1))
    rows = HEAD_DIM + ONES_ROWS
    q_t = qt_ref[0]
    dim_lo = lax.broadcasted_iota(jnp.int32, (LANES, 1), 0) < HEAD_DIM
    lo = _lane_lo()
    km = km_ref[0]
    blk = lax.broadcasted_iota(jnp.int32, (nb, t), 0).astype(F32)
    own = i.astype(F32)
    for h in range(2):
        qh = jnp.where(dim_lo if h == 0 else jnp.logical_not(dim_lo), q_t, 0.0)
        qm_ref[h] = (qh * (ATTN_SCALE * LOG2E)).astype(BF16)
        kmh = jnp.where(lo if h == 0 else jnp.logical_not(lo), km, 0.0)
        gate = jnp.dot(kmh, q_t, precision=HIGHEST, preferred_element_type=F32)
        gate = jnp.where(blk < own, gate, NEG_INF)
        sel = jnp.zeros((nb, t), F32)
        for _ in range(MOBA_TOPK):
            mx = jnp.max(gate, axis=0, keepdims=True)
            first = jnp.min(jnp.where(gate == mx, blk, float(nb)), axis=0, keepdims=True)
            hit = blk == first
            sel = jnp.where(jnp.logical_and(hit, first < own), 1.0, sel)
            gate = jnp.where(hit, -jnp.inf, gate)
        bias_ref[h] = jnp.where(sel > 0.0, 0.0, NEG_INF)

    _flash_t(i, k_ref, vt_ref, qm_ref, bufs, m_ref, acc_ref,
             v_rows=lambda h: slice(h * rows, (h + 1) * rows),
             bias_row=lambda h, kb: bias_ref[h, pl.ds(kb, 1), :])
    o_t = jnp.concatenate([acc_ref[h, :HEAD_DIM, :] / acc_ref[h, HEAD_DIM:HEAD_DIM + 1, :] for h in range(2)],
                          axis=0)
    o_ref[0] = (o_t.T * _silu(g_ref[0])).astype(o_ref.dtype)


def moba_attention(q, k, v, kmean, gate):
    b, s, w = q.shape
    t = ATTN_TILE
    nb = s // t
    tile = pl.BlockSpec((1, t, LANES), lambda bi, p, i: (bi, i, p))
    return pl.pallas_call(
        functools.partial(_moba_body, nb=nb), grid=(b, w // LANES, nb),
        in_specs=[pl.BlockSpec((1, LANES, t), lambda bi, p, i: (bi, p, i)),
                  pl.BlockSpec((1, s, LANES), lambda bi, p, i: (bi, 0, p)),
                  pl.BlockSpec((1, 1, nb, 2 * (HEAD_DIM + ONES_ROWS), t), lambda bi, p, i: (bi, p, 0, 0, 0)),
                  pl.BlockSpec((1, nb, LANES), lambda bi, p, i: (bi, 0, p)), tile],
        out_specs=tile, out_shape=jax.ShapeDtypeStruct((b, s, w), BF16),
        scratch_shapes=[pltpu.VMEM((2, nb, t), F32)] + _flash_scratch(HEAD_DIM),
        compiler_params=_cparams(("parallel", "parallel", "arbitrary")), name="moba_attention",
    )(jnp.swapaxes(q, 1, 2), k, _blocked_t(v, HEAD_DIM), kmean, gate)


def _diff_body(qt_ref, k_ref, vt_ref, g_ref, lqk_ref, sg_ref, o_ref, qm_ref, s0, s1, p0, p1, a0, a1,
               m_ref, acc_ref, *, lam_init):
    i = pl.program_id(2)
    q_t = qt_ref[0]
    dim_lo = lax.broadcasted_iota(jnp.int32, (LANES, 1), 0) < HEAD_DIM
    for h in range(2):
        qm_ref[h] = jnp.where(dim_lo if h == 0 else jnp.logical_not(dim_lo), q_t, 0)
    _flash_t(i, k_ref, vt_ref, qm_ref, ((s0, s1), (p0, p1), (a0, a1)), m_ref, acc_ref,
             v_rows=lambda h: slice(0, LANES + ONES_ROWS), bias_row=None)
    lqk = lqk_ref[...]
    lam = (jnp.exp(jnp.sum(lqk[0:1] * lqk[1:2], axis=1, keepdims=True))
           - jnp.exp(jnp.sum(lqk[2:3] * lqk[3:4], axis=1, keepdims=True)) + lam_init)
    att = [acc_ref[h, :LANES, :] / acc_ref[h, LANES:LANES + 1, :] for h in range(2)]
    o = (att[0] - lam * att[1]).T
    ms = jnp.mean(o * o, axis=1, keepdims=True)
    o = o * lax.rsqrt(ms + NORM_EPS) * sg_ref[...] * (1.0 - lam_init)
    o_ref[0] = (o * _silu(g_ref[0])).astype(o_ref.dtype)


def diff_attention(q, k, v, gate, lqk, subln_g, lam_init):
    b, s, w = q.shape
    t = ATTN_TILE
    nb = s // t
    tile = pl.BlockSpec((1, t, LANES), lambda bi, h, i: (bi, i, h))
    return pl.pallas_call(
        functools.partial(_diff_body, lam_init=lam_init), grid=(b, w // LANES, nb),
        in_specs=[pl.BlockSpec((1, LANES, t), lambda bi, h, i: (bi, h, i)),
                  pl.BlockSpec((1, s, LANES), lambda bi, h, i: (bi, 0, h)),
                  pl.BlockSpec((1, 1, nb, LANES + ONES_ROWS, t), lambda bi, h, i: (bi, h, 0, 0, 0)),
                  tile, pl.BlockSpec((4, HEAD_DIM), lambda bi, h, i: (0, 0)),
                  pl.BlockSpec((1, LANES), lambda bi, h, i: (0, 0))],
        out_specs=tile, out_shape=jax.ShapeDtypeStruct((b, s, w), BF16),
        scratch_shapes=_flash_scratch(LANES),
        compiler_params=_cparams(("parallel", "parallel", "arbitrary")), name="diff_attention",
    )(jnp.swapaxes(q, 1, 2), k, _blocked_t(v, LANES), gate, lqk, subln_g.reshape(1, LANES).astype(F32))


def _mem_body(q_ref, km_ref, vm_ref, g_ref, qg_ref, kg_ref, o_ref):
    gm = _head_mean_matrix(LANES)
    lo = _lane_lo()
    masks = (lo, jnp.logical_not(lo))
    q = q_ref[0]
    q = q * lax.rsqrt(_mm_sel_r(q * q, gm) + NORM_EPS) * qg_ref[...]
    k = km_ref[0]
    k = (k * lax.rsqrt(_mm_sel_r(k * k, gm) + NORM_EPS) * kg_ref[...]).astype(BF16)
    v = vm_ref[0]
    outs = []
    for h in range(2):
        qh = (jnp.where(masks[h], q, 0.0) * ATTN_SCALE).astype(BF16)
        m, l, acc = _softmax_first(_mm_nt(qh, k), v)
        outs.append(acc / l)
    out = jnp.where(lo, outs[0], outs[1])
    o_ref[0] = (out * _silu(g_ref[0])).astype(o_ref.dtype)


def mem_attention(q, km, vm, gate, q_gain, k_gain):
    b, s, w = q.shape
    t = PREP_ROWS
    tile = pl.BlockSpec((1, t, LANES), lambda bi, p, i: (bi, i, p))
    mem = pl.BlockSpec((1, N_MEM, LANES), lambda bi, p, i: (bi, 0, p))
    gain = pl.BlockSpec((1, LANES), lambda bi, p, i: (0, 0))
    rep = LANES // HEAD_DIM
    return pl.pallas_call(
        _mem_body, grid=(b, w // LANES, s // t),
        in_specs=[tile, mem, mem, tile, gain, gain],
        out_specs=tile, out_shape=jax.ShapeDtypeStruct((b, s, w), BF16),
        compiler_params=_cparams(("parallel", "parallel", "parallel")), name="mem_attention",
    )(q, km, vm, gate, jnp.tile(q_gain.astype(F32), rep).reshape(1, LANES),
      jnp.tile(k_gain.astype(F32), rep).reshape(1, LANES))


def _shift(cur, prev8, mu, first):
    rows = lax.broadcasted_iota(jnp.int32, cur.shape, 0)
    before = jnp.where(first, 0.0, prev8[7:8, :])
    prev = jnp.where(rows == 0, before, pltpu.roll(cur, 1, 0))
    return cur + (prev - cur) * mu


def _rwkv_prep_body(r_ref, k_ref, v_ref, lo_ref, rp_ref, kp_ref, vp_ref, lp_ref, mu_ref, mul_ref, w0_ref, w2_ref,
                    a0_ref, a2_ref, kk_ref, ka_ref, rk_ref,
                    ro_ref, lw_ref, ko_ref, vo_ref, kn_ref, ao_ref, bo_ref):
    first = pl.program_id(1) == 0
    mu = mu_ref[...]
    r = _shift(r_ref[0], rp_ref[0], mu[0:1], first)
    k = _shift(k_ref[0], kp_ref[0], mu[1:2], first)
    v = _shift(v_ref[0], vp_ref[0], mu[2:3], first)
    lora = _shift(lo_ref[0], lp_ref[0], mul_ref[...], first)
    z = w0_ref[...] + _mm_hi(jnp.tanh(lora), w2_ref[...])
    w_log = -(jnp.maximum(-z, 0.0) + jnp.log1p(jnp.exp(-jnp.abs(z)))) - 0.5
    a = jax.nn.sigmoid(a0_ref[...] + _mm_hi(lora, a2_ref[...]))
    kk = k * kk_ref[...]
    k2 = k * (1.0 + (a - 1.0) * ka_ref[...])
    ones = _head_mean_matrix(B_W) * float(HEAD_DIM)
    kk = kk * lax.rsqrt(jnp.maximum(_mm_sel_r(kk * kk, ones), 1e-24))
    ro_ref[0] = r
    lw_ref[0] = -jnp.exp(w_log)
    ko_ref[0] = k2
    vo_ref[0] = v
    kn_ref[0] = kk
    ao_ref[0] = a
    bo_ref[0] = _mm_sel_r(r * k2 * rk_ref[...], ones) * v


def rwkv_prep(r, k, v, lora, mu, w0, w2, a0, a2, k_k, k_a, r_k):
    b, s, w = r.shape
    t = ROW_TILE
    tile = pl.BlockSpec((1, t, w), lambda bi, i: (bi, i, 0))
    ltile = pl.BlockSpec((1, t, LANES), lambda bi, i: (bi, i, 0))
    prev_idx = lambda bi, i: (bi, jnp.maximum(i * (t // 8) - 1, 0), 0)
    ptile = pl.BlockSpec((1, 8, w), prev_idx)
    pltile = pl.BlockSpec((1, 8, LANES), prev_idx)
    const = lambda shape: pl.BlockSpec(shape, lambda bi, i: (0, 0))
    mu3 = jnp.stack([mu[:w], mu[w:2 * w], mu[2 * w:3 * w]]).astype(F32)
    mul = jnp.zeros((1, LANES), F32).at[0, :2 * B_LORA].set(mu[3 * w:])
    w2p = jnp.zeros((LANES, w), F32).at[:B_LORA].set(w2)
    a2p = jnp.zeros((LANES, w), F32).at[B_LORA:2 * B_LORA].set(a2)
    row = lambda p: p.reshape(1, w).astype(F32)
    return pl.pallas_call(
        _rwkv_prep_body, grid=(b, s // t),
        in_specs=[tile, tile, tile, ltile, ptile, ptile, ptile, pltile, const((3, w)), const((1, LANES)),
                  const((1, w)), const((LANES, w)), const((1, w)), const((LANES, w)), const((1, w)), const((1, w)),
                  const((1, w))],
        out_specs=[tile] * 7, out_shape=[jax.ShapeDtypeStruct((b, s, w), F32)] * 7,
        compiler_params=_cparams(("parallel", "parallel")), name="rwkv_prep",
    )(r, k, v, lora, r, k, v, lora, mu3, mul, row(w0), w2p, row(a0), a2p, row(k_k), row(k_a), row(r_k))


def _tri(n, strict):
    r = lax.broadcasted_iota(jnp.int32, (n, n), 0)
    c = lax.broadcasted_iota(jnp.int32, (n, n), 1)
    return (c < r) if strict else (c <= r)


def _block_diag_mask():
    r = lax.broadcasted_iota(jnp.int32, (LANES, LANES), 0) // HEAD_DIM
    c = lax.broadcasted_iota(jnp.int32, (LANES, LANES), 1) // HEAD_DIM
    return r == c


def _rwkv_scan_body(r_ref, lw_ref, k_ref, v_ref, kn_ref, a_ref, bo_ref, g_ref, lg_ref, lb_ref, o_ref, s_ref, *,
                    nbatch, npair):
    @pl.when(pl.program_id(0) == 0)
    def _():
        s_ref[...] = jnp.zeros_like(s_ref)

    c = CHUNK
    lo = _lane_lo()
    masks = (lo, jnp.logical_not(lo))
    strict = _tri(c, True)
    incl = _tri(c, False)
    ltri = incl.astype(F32)
    eye = (lax.broadcasted_iota(jnp.int32, (c, c), 0) == lax.broadcasted_iota(jnp.int32, (c, c), 1)).astype(F32)
    bd = _block_diag_mask()
    gmean = _head_mean_matrix(LANES)

    def chain(sl, bi, pi):
        cols = slice(pi * LANES, (pi + 1) * LANES)
        idx = bi * npair + pi
        r, lw, k, v, kn, a = (ref[bi, sl, cols] for ref in (r_ref, lw_ref, k_ref, v_ref, kn_ref, a_ref))
        cw = _mm_sel_l(ltri, lw)
        yield
        cl = cw[c - 1:c, :]
        at = -kn * jnp.exp(cw - lw)
        e_neg = jnp.exp(-cw)
        bt = kn * a * e_neg
        kt = k * e_neg
        rt = r * jnp.exp(cw)
        e_end = jnp.exp(cl - cw)
        s0 = s_ref[idx]
        rhs = _mm_nt(at, s0)
        ys0 = _mm_nt(rt, s0)
        yield
        ah = [jnp.where(mk, at, 0.0) for mk in masks]
        rh = [jnp.where(mk, rt, 0.0) for mk in masks]
        n = [jnp.where(strict, _mm_nt(x, bt), 0.0) for x in ah]
        yield
        aak = [jnp.where(strict, _mm_nt(x, kt), 0.0) for x in ah]
        yield
        arb = [jnp.where(incl, _mm_nt(x, bt), 0.0) for x in rh]
        yield
        ark = [jnp.where(incl, _mm_nt(x, kt), 0.0) for x in rh]
        yield
        xs = [rhs + _mm(x, v) for x in aak]
        yv = [_mm(x, v) for x in ark]
        yield
        tinv = [eye + x for x in n]
        p = n
        for _ in range(5):
            p = [_mm(x, x) for x in p]
            yield
            tinv = [x + _mm(x, y) for x, y in zip(tinv, p)]
            yield
        u = jnp.where(lo, _mm(tinv[0], xs[0]), _mm(tinv[1], xs[1]))
        yield
        y = ys0 + jnp.where(lo, _mm(arb[0], u) + yv[0], _mm(arb[1], u) + yv[1])
        upd = _mm_tn(u, kn * a * e_end) + _mm_tn(v, k * e_end)
        yield
        s_ref[idx] = s0 * jnp.exp(cl) + jnp.where(bd, upd, 0.0)
        mean = _mm_sel_r(y, gmean)
        yield
        d = y - mean
        var = _mm_sel_r(d * d, gmean)
        yield
        yn = d * lax.rsqrt(var + RWKV_GN_EPS) * lg_ref[:, cols] + lb_ref[:, cols] + bo_ref[bi, sl, cols]
        o_ref[bi, sl, cols] = (yn * _silu(g_ref[bi, sl, cols])).astype(o_ref.dtype)

    def chunk(ci, carry):
        sl = pl.ds(pl.multiple_of(ci * c, c), c)
        gens = [chain(sl, bi, pi) for bi in range(nbatch) for pi in range(npair)]
        for _ in itertools.zip_longest(*gens):
            pass
        return carry

    lax.fori_loop(0, SCAN_ROWS // c, chunk, 0)


def rwkv_scan(r, lw, k, v, kn, a, bonus, gate, lnx_g, lnx_b):
    b, s, w = r.shape
    t = SCAN_ROWS
    tile = pl.BlockSpec((b, t, w), lambda i: (0, i, 0))
    vec = pl.BlockSpec((1, w), lambda i: (0, 0))
    return pl.pallas_call(
        functools.partial(_rwkv_scan_body, nbatch=b, npair=w // LANES), grid=(s // t,),
        in_specs=[tile] * 8 + [vec, vec],
        out_specs=tile, out_shape=jax.ShapeDtypeStruct((b, s, w), BF16),
        scratch_shapes=[pltpu.VMEM((b * (w // LANES), LANES, LANES), F32)],
        compiler_params=_cparams(("arbitrary",)), name="rwkv_scan",
    )(r, lw, k, v, kn, a, bonus, gate, lnx_g.reshape(1, w).astype(F32), lnx_b.reshape(1, w).astype(F32))


def _hgrn_body(q_ref, f_ref, i_ref, g_ref, lb_ref, gn_ref, o_ref, s_ref, *, nbatch, npair):
    @pl.when(pl.program_id(0) == 0)
    def _():
        s_ref[...] = jnp.zeros_like(s_ref)

    c = CHUNK
    lo = _lane_lo()
    masks = (lo, jnp.logical_not(lo))
    ltri = _tri(c, False).astype(F32)
    bd = _block_diag_mask()
    gmean = _head_mean_matrix(LANES)
    head_ones = gmean * float(HEAD_DIM)
    rows = lax.broadcasted_iota(jnp.int32, (SUB, 1), 0)

    def chain(sl, bi, pi):
        cols = slice(pi * LANES, (pi + 1) * LANES)
        idx = bi * npair + pi
        lb = lb_ref[:, cols]
        log_lb = jnp.log(lb)
        log_1m = jnp.log1p(-lb)
        q, fr, v = q_ref[bi, sl, cols], f_ref[bi, sl, cols], i_ref[bi, sl, cols]
        log_sig = jnp.minimum(fr, 0.0) - jnp.log1p(jnp.exp(-jnp.abs(fr)))
        z = log_1m + log_sig
        hi = jnp.maximum(log_lb, z)
        log_f = hi + jnp.log1p(jnp.exp(-jnp.abs(log_lb - z)))
        k = (1.0 - lb) * jax.nn.sigmoid(-fr)
        bc = _mm_sel_l(ltri, log_f)
        yield
        bl = bc[c - 1:c, :]
        s0 = s_ref[idx]
        outs = []
        for sb in range(c // SUB):
            r0 = sb * SUB
            qs, bs, ks, vs = (x[r0:r0 + SUB] for x in (q, bc, k, v))
            o = _mm_nt(qs * jnp.exp(bs), s0)
            if sb > 0:
                ref = bc[r0 - 1:r0, :]
                qh = qs * jnp.exp(bs - ref)
                kh = k[:r0] * jnp.exp(ref - bc[:r0])
                vh = v[:r0]
                sc = [_mm_nt(jnp.where(mk, qh, 0.0), kh) for mk in masks]
                yield
                o = o + jnp.where(lo, _mm(sc[0], vh), _mm(sc[1], vh))
            pair = jnp.concatenate(
                [qs * ks[si:si + 1, :] * jnp.exp(jnp.where(rows >= si, bs - bs[si:si + 1, :], NEG_INF))
                 for si in range(SUB)], axis=0)
            score = _mm_sel_r(pair, head_ones)
            yield
            for si in range(SUB):
                o = o + score[si * SUB:(si + 1) * SUB] * vs[si:si + 1, :]
            outs.append(o)
        od = jnp.concatenate(outs, axis=0)
        s_ref[idx] = s0 * jnp.exp(bl) + jnp.where(bd, _mm_tn(v, k * jnp.exp(bl - bc)), 0.0)
        ms = _mm_sel_r(od * od, gmean)
        yield
        on = od * lax.rsqrt(ms + NORM_EPS) * gn_ref[...]
        o_ref[bi, sl, cols] = (on * _silu(g_ref[bi, sl, cols])).astype(o_ref.dtype)

    def chunk(ci, carry):
        sl = pl.ds(pl.multiple_of(ci * c, c), c)
        gens = [chain(sl, bi, pi) for bi in range(nbatch) for pi in range(npair)]
        for _ in itertools.zip_longest(*gens):
            pass
        return carry

    lax.fori_loop(0, SCAN_ROWS // c, chunk, 0)


def hgrn2(q, f, iv, gate, lb, gn_g):
    b, s, w = q.shape
    t = SCAN_ROWS
    tile = pl.BlockSpec((b, t, w), lambda i: (0, i, 0))
    rep = LANES // HEAD_DIM
    return pl.pallas_call(
        functools.partial(_hgrn_body, nbatch=b, npair=w // LANES), grid=(s // t,),
        in_specs=[tile] * 4 + [pl.BlockSpec((1, w), lambda i: (0, 0)), pl.BlockSpec((1, LANES), lambda i: (0, 0))],
        out_specs=tile, out_shape=jax.ShapeDtypeStruct((b, s, w), BF16),
        scratch_shapes=[pltpu.VMEM((b * (w // LANES), LANES, LANES), F32)],
        compiler_params=_cparams(("arbitrary",)), name="hgrn2",
    )(q, f, iv, gate, lb.reshape(1, w).astype(F32), jnp.tile(gn_g.astype(F32), rep).reshape(1, LANES))


def _memory_kv(memf, b, g, w_kv):
    wb = w_kv.astype(BF16)
    km, vm = rms_proj(memf, g, [wb[:, :M_W], wb[:, M_W:]], [F32, BF16])
    return km.reshape(b, N_MEM, M_W), vm.reshape(b, N_MEM, M_W)


def _even_layer(xf, b, s, km, vm, tables, ln_g, w_in, w_out, a_qn_g, a_kn_g, m_qn_g, m_kn_g,
                mu, w0, w2, a0, a2, k_k, k_a, r_k, lnx_g, lnx_b):
    wb = w_in.astype(BF16)
    edges = [0]
    for width in (A_W, A_W, A_W, A_W, B_W, B_W, B_W, 2 * B_LORA, B_W, M_W, M_W):
        edges.append(edges[-1] + width)
    ws = [wb[:, edges[n]:edges[n + 1]] for n in range(11)]
    ws[7] = jnp.pad(ws[7], ((0, 0), (0, LANES - 2 * B_LORA)))
    dts = [F32, F32, BF16, F32, F32, F32, F32, F32, F32, F32, F32]
    qa, ka, va, ga, rr, rk, rv, lora, gb, qm, gm = rms_proj(xf, ln_g, ws, dts)
    sh = lambda t: t.reshape(b, s, t.shape[-1])
    q = qk_prep(sh(qa), a_qn_g, tables, F32)
    k, kmean = qk_prep(sh(ka), a_kn_g, tables, BF16, want_block_mean=True)
    oa = moba_attention(q, k, sh(va), kmean, sh(ga))
    pre = rwkv_prep(sh(rr), sh(rk), sh(rv), sh(lora), mu, w0, w2, a0, a2, k_k, k_a, r_k.reshape(-1))
    ob = rwkv_scan(*pre, sh(gb), lnx_g, lnx_b)
    om = mem_attention(sh(qm), km, vm, sh(gm), m_qn_g, m_kn_g)
    wo = w_out.astype(BF16)
    fl = lambda t: t.reshape(b * s, t.shape[-1])
    return out_proj(xf, [fl(oa), fl(ob), fl(om)], [wo[:A_W], wo[A_W:A_W + B_W], wo[A_W + B_W:]])


def _odd_layer(xf, b, s, km, vm, tables, li, lb, ln_g, w_in, w_out, c_qn_g, c_kn_g, lqk, subln_g, d_gn_g,
               m_qn_g, m_kn_g):
    wb = w_in.astype(BF16)
    edges = [0]
    for width in (C_W, C_W, C_W, C_W, D_W, D_W, D_W, D_W, M_W, M_W):
        edges.append(edges[-1] + width)
    ws = [wb[:, edges[n]:edges[n + 1]] for n in range(10)]
    dts = [F32, F32, BF16, F32, F32, F32, F32, F32, F32, F32]
    qc, kc, vc, gc, qd, fd, idd, gd, qm, gm = rms_proj(xf, ln_g, ws, dts)
    sh = lambda t: t.reshape(b, s, t.shape[-1])
    q = qk_prep(sh(qc), c_qn_g, tables, BF16, scale=ATTN_SCALE * LOG2E)
    k = qk_prep(sh(kc), c_kn_g, tables, BF16)
    lam_init = 0.8 - 0.6 * math.exp(-0.3 * li)
    oc = diff_attention(q, k, sh(vc), sh(gc), lqk, subln_g, lam_init)
    od = hgrn2(sh(qd), sh(fd), sh(idd), sh(gd), lb, d_gn_g)
    om = mem_attention(sh(qm), km, vm, sh(gm), m_qn_g, m_kn_g)
    wo = w_out.astype(BF16)
    fl = lambda t: t.reshape(b * s, t.shape[-1])
    return out_proj(xf, [fl(oc), fl(od), fl(om)], [wo[:C_W], wo[C_W:C_W + D_W], wo[C_W + D_W:]])


def kernel(x, mem, ln_g, mem_ln_g, w_mem_kv, m_qn_g, m_kn_g, e_w_in, e_w_out, a_qn_g, a_kn_g, b_mu, b_w0, b_w2, b_a0, b_a2, b_k_k, b_k_a, b_r_k, b_lnx_g, b_lnx_b, o_w_in, o_w_out, c_qn_g, c_kn_g, c_lq1, c_lk1, c_lq2, c_lk2, c_subln_g, d_lb, d_gn_g):
    b, s, d = x.shape
    depth = ln_g.shape[0]
    tables = rope_tables_lanes(s)
    lbs = jax.nn.softmax(d_lb.astype(F32), axis=0)
    lbs = jnp.cumsum(lbs, axis=0) - lbs[0:1]
    xf = x.reshape(b * s, d)
    memf = mem.reshape(b * N_MEM, d)
    for li in range(depth):
        j = li // 2
        km, vm = _memory_kv(memf, b, mem_ln_g[li], w_mem_kv[li])
        if li % 2 == 0:
            xf = _even_layer(xf, b, s, km, vm, tables, ln_g[li], e_w_in[j], e_w_out[j], a_qn_g[j], a_kn_g[j],
                             m_qn_g[li], m_kn_g[li], b_mu[j], b_w0[j], b_w2[j], b_a0[j], b_a2[j], b_k_k[j],
                             b_k_a[j], b_r_k[j], b_lnx_g[j], b_lnx_b[j])
        else:
            lqk = jnp.stack([c_lq1[j], c_lk1[j], c_lq2[j], c_lk2[j]]).astype(F32)
            xf = _odd_layer(xf, b, s, km, vm, tables, li, jnp.maximum(lbs[j], 0.0), ln_g[li], o_w_in[j],
                            o_w_out[j], c_qn_g[j], c_kn_g[j], lqk, c_subln_g[j], d_gn_g[j], m_qn_g[li], m_kn_g[li])
    return xf.reshape(b, s, d)
```

```python
import functools
import itertools
import math

import jax
import jax.numpy as jnp
from jax import lax
from jax.experimental import pallas as pl
from jax.experimental.pallas import tpu as pltpu

F32 = jnp.float32
BF16 = jnp.bfloat16
HIGHEST = lax.Precision.HIGHEST

N_MEM = 256
HEAD_DIM = 64
ROPE_THETA = 500000.0
ROPE_DIM = HEAD_DIM // 4
NORM_EPS = 1e-6
NEG_INF = -1e30
A_HEADS = 6
MOBA_BLOCK = 256
MOBA_TOPK = 3
B_HEADS = 6
B_LORA = 32
RWKV_GN_EPS = 64e-5
C_HEADS = 4
D_HEADS = 4
M_HEADS = 4
A_W = A_HEADS * HEAD_DIM
B_W = B_HEADS * HEAD_DIM
C_W = C_HEADS * 2 * HEAD_DIM
D_W = D_HEADS * HEAD_DIM
M_W = M_HEADS * HEAD_DIM
ATTN_SCALE = HEAD_DIM ** -0.5

LANES = 128
VMEM_LIMIT = 48 * 1024 * 1024

ROW_TILE = 512
PREP_ROWS = 512
LOG2E = math.log2(math.e)
ONES_ROWS = 16
ATTN_TILE = 256
Q_TILE = 2 * ATTN_TILE
FLASH_UNROLL = 4
CHUNK = 64
SUB = 16
SCAN_ROWS = 256

_NT = (((1,), (1,)), ((), ()))
_TN = (((0,), (0,)), ((), ()))


def _cparams(sem):
    return pltpu.CompilerParams(dimension_semantics=sem, vmem_limit_bytes=VMEM_LIMIT)


def _mm(a, b):
    return jnp.dot(a.astype(BF16), b.astype(BF16), preferred_element_type=F32)


def _mm_nt(a, b):
    return lax.dot_general(a.astype(BF16), b.astype(BF16), _NT, preferred_element_type=F32)


def _mm_tn(a, b):
    return lax.dot_general(a.astype(BF16), b.astype(BF16), _TN, preferred_element_type=F32)


def _mm_hi(a, b):
    return jnp.dot(a, b, precision=HIGHEST, preferred_element_type=F32)


def _split3(x):
    x1 = x.astype(BF16)
    r1 = x - x1.astype(F32)
    x2 = r1.astype(BF16)
    x3 = (r1 - x2.astype(F32)).astype(BF16)
    return x1, x2, x3


def _mm_sel_r(x, sel):
    sb = sel.astype(BF16)
    x1, x2, x3 = _split3(x)
    return (jnp.dot(x3, sb, preferred_element_type=F32) + jnp.dot(x2, sb, preferred_element_type=F32)
            + jnp.dot(x1, sb, preferred_element_type=F32))


def _mm_sel_l(sel, x):
    sb = sel.astype(BF16)
    x1, x2, x3 = _split3(x)
    return (jnp.dot(sb, x3, preferred_element_type=F32) + jnp.dot(sb, x2, preferred_element_type=F32)
            + jnp.dot(sb, x1, preferred_element_type=F32))


def _silu(x):
    return x * jax.nn.sigmoid(x)


def _lane_lo(width=LANES):
    lane = lax.broadcasted_iota(jnp.int32, (1, width), 1)
    return (lane % LANES) < HEAD_DIM


def _head_mean_matrix(width):
    r = lax.broadcasted_iota(jnp.int32, (width, width), 0) // HEAD_DIM
    c = lax.broadcasted_iota(jnp.int32, (width, width), 1) // HEAD_DIM
    return jnp.where(r == c, 1.0 / HEAD_DIM, 0.0).astype(F32)


def _proj_body(x_ref, g_ref, *refs, n_out):
    x = x_ref[...]
    ms = jnp.mean(x * x, axis=-1, keepdims=True)
    h = (x * lax.rsqrt(ms + NORM_EPS) * g_ref[...]).astype(BF16)
    for w_ref, o_ref in zip(refs[:n_out], refs[n_out:]):
        o_ref[...] = jnp.dot(h, w_ref[...], preferred_element_type=F32).astype(o_ref.dtype)


def rms_proj(x2d, g, ws, out_dtypes):
    n, d = x2d.shape
    tm = min(ROW_TILE, n)
    in_specs = [pl.BlockSpec((tm, d), lambda i: (i, 0)), pl.BlockSpec((1, d), lambda i: (0, 0))]
    in_specs += [pl.BlockSpec(w.shape, lambda i: (0, 0)) for w in ws]
    out_specs = [pl.BlockSpec((tm, w.shape[1]), lambda i: (i, 0)) for w in ws]
    out_shape = [jax.ShapeDtypeStruct((n, w.shape[1]), dt) for w, dt in zip(ws, out_dtypes)]
    return pl.pallas_call(
        functools.partial(_proj_body, n_out=len(ws)),
        grid=(n // tm,), in_specs=in_specs, out_specs=out_specs, out_shape=out_shape,
        compiler_params=_cparams(("parallel",)), name="rms_proj",
    )(x2d, g.reshape(1, d).astype(F32), *ws)


def _out_proj_body(x_ref, *refs, n_in):
    o_ref = refs[-1]
    acc = x_ref[...]
    for m_ref, w_ref in zip(refs[:n_in], refs[n_in:2 * n_in]):
        acc = acc + jnp.dot(m_ref[...], w_ref[...], preferred_element_type=F32)
    o_ref[...] = acc


def out_proj(x2d, parts, ws):
    n, d = x2d.shape
    tm = min(ROW_TILE, n)
    in_specs = [pl.BlockSpec((tm, d), lambda i: (i, 0))]
    in_specs += [pl.BlockSpec((tm, p.shape[1]), lambda i: (i, 0)) for p in parts]
    in_specs += [pl.BlockSpec(w.shape, lambda i: (0, 0)) for w in ws]
    return pl.pallas_call(
        functools.partial(_out_proj_body, n_in=len(parts)),
        grid=(n // tm,), in_specs=in_specs, out_specs=pl.BlockSpec((tm, d), lambda i: (i, 0)),
        out_shape=jax.ShapeDtypeStruct((n, d), F32),
        compiler_params=_cparams(("parallel",)), name="out_proj",
    )(x2d, *parts, *ws)


def _qk_prep_body(x_ref, g_ref, c_ref, s1_ref, s2_ref, o_ref, *maybe_mean_ref, scale):
    gm = _head_mean_matrix(LANES)
    cos, sin1, sin2 = c_ref[...], s1_ref[...], s2_ref[...]
    for c in range(x_ref.shape[2] // LANES):
        cols = slice(c * LANES, (c + 1) * LANES)
        x = x_ref[0, :, cols]
        y = x * lax.rsqrt(_mm_sel_r(x * x, gm) + NORM_EPS) * g_ref[...]
        yr = y * cos + pltpu.roll(y, LANES - ROPE_DIM // 2, 1) * sin1 + pltpu.roll(y, ROPE_DIM // 2, 1) * sin2
        o_ref[0, :, cols] = (yr * scale).astype(o_ref.dtype)
        if maybe_mean_ref:
            for rb in range(PREP_ROWS // MOBA_BLOCK):
                blk = yr[rb * MOBA_BLOCK:(rb + 1) * MOBA_BLOCK]
                maybe_mean_ref[0][rb, :, cols] = jnp.mean(blk, axis=0, keepdims=True)


def qk_prep(x, gain, tables, out_dtype, scale=1.0, want_block_mean=False):
    b, s, w = x.shape
    tm = PREP_ROWS
    per = tm // MOBA_BLOCK
    cos_t, sin1_t, sin2_t = tables
    g = jnp.tile(gain.astype(F32), LANES // HEAD_DIM).reshape(1, LANES)
    tile_spec = pl.BlockSpec((1, tm, w), lambda bi, i: (bi, i, 0))
    tab_spec = pl.BlockSpec((tm, LANES), lambda bi, i: (i, 0))
    out_specs = [tile_spec]
    out_shape = [jax.ShapeDtypeStruct((b, s, w), out_dtype)]
    if want_block_mean:
        nt = s // tm
        out_specs.append(pl.BlockSpec((per, 1, w), lambda bi, i: (bi * nt + i, 0, 0)))
        out_shape.append(jax.ShapeDtypeStruct((b * nt * per, 1, w), F32))
    outs = pl.pallas_call(
        functools.partial(_qk_prep_body, scale=scale), grid=(b, s // tm),
        in_specs=[tile_spec, pl.BlockSpec((1, LANES), lambda bi, i: (0, 0)), tab_spec, tab_spec, tab_spec],
        out_specs=out_specs, out_shape=out_shape,
        compiler_params=_cparams(("parallel", "parallel")), name="qk_prep",
    )(x, g, cos_t, sin1_t, sin2_t)
    if want_block_mean:
        return outs[0], outs[1].reshape(b, s // MOBA_BLOCK, w)
    return outs[0]


def rope_tables_lanes(seq):
    pos = jnp.arange(seq, dtype=F32)
    inv = 1.0 / (ROPE_THETA ** (jnp.arange(0, ROPE_DIM, 2, dtype=F32) / ROPE_DIM))
    ang = pos[:, None] * inv[None, :]
    cos, sin = jnp.cos(ang), jnp.sin(ang)
    half = ROPE_DIM // 2
    ones = jnp.ones((seq, HEAD_DIM - ROPE_DIM), F32)
    zeros_h = jnp.zeros((seq, half), F32)
    zeros_r = jnp.zeros((seq, HEAD_DIM - ROPE_DIM), F32)
    c = jnp.concatenate([cos, cos, ones], axis=1)
    s1 = jnp.concatenate([-sin, zeros_h, zeros_r], axis=1)
    s2 = jnp.concatenate([zeros_h, sin, zeros_r], axis=1)
    rep = LANES // HEAD_DIM
    return jnp.tile(c, (1, rep)), jnp.tile(s1, (1, rep)), jnp.tile(s2, (1, rep))


def _softmax_first(s, v):
    m = jnp.max(s, axis=1, keepdims=True)
    p = jnp.exp(s - m)
    return m, jnp.sum(p, axis=1, keepdims=True), _mm(p, v)


def _flash_t(i, k_ref, vt_ref, qm_ref, bufs, m_ref, acc_ref, v_rows, bias_row):
    t = ATTN_TILE
    (s0, s1), (p0, p1), (a0, a1) = bufs
    n_real = 2 * i + 1
    last = jnp.maximum(2 * i - 1, 0)
    ncol = Q_TILE // LANES

    def key_tile(x):
        own = jnp.where(x == 0, 2 * i + 1, 2 * i)
        return jnp.where(x < 2, own, jnp.minimum(x - 2, last))

    def a_part(x, h, s_w):
        kb = jnp.minimum(x - 2, last)
        k_blk = k_ref[0, pl.ds(pl.multiple_of(kb * t, t), t), :]
        s = jnp.dot(k_blk, qm_ref[h], preferred_element_type=F32)
        if bias_row is not None:
            s = s + bias_row(h, kb)
        s_w[h] = s

    def b_part(x, h, c, s_r, p_w, a_w):
        real = x <= n_real
        cs = slice(c * LANES, (c + 1) * LANES)
        s = s_r[h, :, cs]
        m_old = m_ref[h, :, cs]
        m_top = jnp.maximum(m_old, jnp.max(s, axis=0, keepdims=True))
        m_new = jnp.where(real, m_top, m_old)
        m_ref[h, :, cs] = m_new
        a_w[h, :, cs] = jnp.exp2(m_old - m_new)
        p_w[h, :, cs] = jnp.exp2(s - m_top).astype(BF16)

    def c_part(x, h, p_r, a_r):
        v_blk = vt_ref[0, 0, key_tile(x), v_rows(h), :]
        v_blk = jnp.where(x <= n_real, v_blk, jnp.zeros_like(v_blk))
        acc_ref[h] = a_r[h] * acc_ref[h] + jnp.dot(v_blk, p_r[h], preferred_element_type=F32)

    def stage_a(x, s_w):
        for h in range(2):
            a_part(x, h, s_w)

    def stage_b(x, s_r, p_w, a_w):
        for h in range(2):
            for c in range(ncol):
                b_part(x, h, c, s_r, p_w, a_w)

    def stage_c(x, p_r, a_r):
        for h in range(2):
            c_part(x, h, p_r, a_r)

    m_ref[...] = jnp.full(m_ref.shape, NEG_INF, F32)
    acc_ref[...] = jnp.zeros(acc_ref.shape, F32)
    kpos = lax.broadcasted_iota(jnp.int32, (t, Q_TILE), 0)
    qpos = lax.broadcasted_iota(jnp.int32, (t, Q_TILE), 1)
    late = qpos >= t
    late_row = lax.broadcasted_iota(jnp.int32, (1, Q_TILE), 1) >= t
    allow_hi = jnp.logical_and(late, kpos <= qpos - t)
    allow_lo = jnp.logical_or(late, kpos <= qpos)
    k_hi = k_ref[0, pl.ds(pl.multiple_of((2 * i + 1) * t, t), t), :]
    k_lo = k_ref[0, pl.ds(pl.multiple_of(2 * i * t, t), t), :]
    for h in range(2):
        s0[h] = jnp.where(allow_hi, jnp.dot(k_hi, qm_ref[h], preferred_element_type=F32), NEG_INF)
        s_lo = jnp.dot(k_lo, qm_ref[h], preferred_element_type=F32)
        if bias_row is not None:
            s_lo = s_lo + jnp.where(late_row, bias_row(h, 2 * i), 0.0)
        s1[h] = jnp.where(allow_lo, s_lo, NEG_INF)
    bufs_of = ((s0, p0, a0), (s1, p1, a1))

    def step(x, k):
        s_w, p_r, a_r = bufs_of[k % 2]
        s_r, p_w, a_w = bufs_of[(k + 1) % 2]
        stage_c(x - 2, p_r, a_r)
        stage_b(x - 1, s_r, p_w, a_w)
        stage_a(x, s_w)

    stage_b(0, s0, p0, a0)
    u_steps = FLASH_UNROLL

    def unrolled(u, carry):
        for k in range(u_steps):
            step(u_steps * u + 2 + k, k)
        return carry

    lax.fori_loop(0, (2 * i + 2 + u_steps - 1) // u_steps, unrolled, 0)


def _flash_scratch(v_rows):
    t, tq = ATTN_TILE, Q_TILE
    return [pltpu.VMEM((2, LANES, tq), BF16),
            pltpu.VMEM((2, t, tq), F32), pltpu.VMEM((2, t, tq), F32),
            pltpu.VMEM((2, t, tq), BF16), pltpu.VMEM((2, t, tq), BF16),
            pltpu.VMEM((2, 1, tq), F32), pltpu.VMEM((2, 1, tq), F32),
            pltpu.VMEM((2, 1, tq), F32), pltpu.VMEM((2, v_rows + ONES_ROWS, tq), F32)]


def _blocked_t(v, group):
    b, s, w = v.shape
    t = ATTN_TILE
    v_t = v.reshape(b, s // t, t, w // LANES, LANES // group, group).transpose(0, 3, 1, 4, 5, 2)
    ones = jnp.ones(v_t.shape[:4] + (ONES_ROWS, t), v.dtype)
    v_t = jnp.concatenate([v_t, ones], axis=4)
    return v_t.reshape(b, w // LANES, s // t, (LANES // group) * (group + ONES_ROWS), t)


def _moba_body(qt_ref, k_ref, vt_ref, km_ref, g_ref, o_ref, bias_ref, qm_ref, s0, s1, p0, p1, a0, a1,
               m_ref, acc_ref, *, nb):
    i = pl.program_id(2)
    t = Q_TILE
    bufs = ((s0, s1), (p0, p1), (a0, a1))
    rows = HEAD_DIM + ONES_ROWS
    q_t = qt_ref[0]
    dim_lo = lax.broadcasted_iota(jnp.int32, (LANES, 1), 0) < HEAD_DIM
    lo = _lane_lo()
    km = km_ref[0]
    blk = lax.broadcasted_iota(jnp.int32, (nb, t), 0).astype(F32)
    own = (2 * i + (lax.broadcasted_iota(jnp.int32, (1, t), 1) >= MOBA_BLOCK).astype(jnp.int32)).astype(F32)
    for h in range(2):
        qh = jnp.where(dim_lo if h == 0 else jnp.logical_not(dim_lo), q_t, 0.0)
        qm_ref[h] = (qh * (ATTN_SCALE * LOG2E)).astype(BF16)
        kmh = jnp.where(lo if h == 0 else jnp.logical_not(lo), km, 0.0)
        gate = jnp.dot(kmh, q_t, precision=HIGHEST, preferred_element_type=F32)
        gate = jnp.where(blk < own, gate, NEG_INF)
        sel = jnp.zeros((nb, t), F32)
        for _ in range(MOBA_TOPK):
            mx = jnp.max(gate, axis=0, keepdims=True)
            first = jnp.min(jnp.where(gate == mx, blk, float(nb)), axis=0, keepdims=True)
            hit = blk == first
            sel = jnp.where(jnp.logical_and(hit, first < own), 1.0, sel)
            gate = jnp.where(hit, -jnp.inf, gate)
        bias_ref[h] = jnp.where(sel > 0.0, 0.0, NEG_INF)

    _flash_t(i, k_ref, vt_ref, qm_ref, bufs, m_ref, acc_ref,
             v_rows=lambda h: slice(h * rows, (h + 1) * rows),
             bias_row=lambda h, kb: bias_ref[h, pl.ds(kb, 1), :])
    o_t = jnp.concatenate([acc_ref[h, :HEAD_DIM, :] / acc_ref[h, HEAD_DIM:HEAD_DIM + 1, :] for h in range(2)],
                          axis=0)
    o_ref[0] = (o_t.T * _silu(g_ref[0])).astype(o_ref.dtype)


def moba_attention(q, k, v, kmean, gate):
    b, s, w = q.shape
    t, tq = ATTN_TILE, Q_TILE
    nb = s // t
    tile = pl.BlockSpec((1, tq, LANES), lambda bi, p, i: (bi, i, p))
    return pl.pallas_call(
        functools.partial(_moba_body, nb=nb), grid=(b, w // LANES, s // tq),
        in_specs=[pl.BlockSpec((1, LANES, tq), lambda bi, p, i: (bi, p, i)),
                  pl.BlockSpec((1, s, LANES), lambda bi, p, i: (bi, 0, p)),
                  pl.BlockSpec((1, 1, nb, 2 * (HEAD_DIM + ONES_ROWS), t), lambda bi, p, i: (bi, p, 0, 0, 0)),
                  pl.BlockSpec((1, nb, LANES), lambda bi, p, i: (bi, 0, p)), tile],
        out_specs=tile, out_shape=jax.ShapeDtypeStruct((b, s, w), BF16),
        scratch_shapes=[pltpu.VMEM((2, nb, tq), F32)] + _flash_scratch(HEAD_DIM),
        compiler_params=_cparams(("parallel", "parallel", "arbitrary")), name="moba_attention",
    )(jnp.swapaxes(q, 1, 2), k, _blocked_t(v, HEAD_DIM), kmean, gate)


def _diff_body(qt_ref, k_ref, vt_ref, g_ref, lqk_ref, sg_ref, o_ref, qm_ref, s0, s1, p0, p1, a0, a1,
               m_ref, acc_ref, *, lam_init):
    i = pl.program_id(2)
    q_t = qt_ref[0]
    dim_lo = lax.broadcasted_iota(jnp.int32, (LANES, 1), 0) < HEAD_DIM
    for h in range(2):
        qm_ref[h] = jnp.where(dim_lo if h == 0 else jnp.logical_not(dim_lo), q_t, 0)
    _flash_t(i, k_ref, vt_ref, qm_ref, ((s0, s1), (p0, p1), (a0, a1)), m_ref, acc_ref,
             v_rows=lambda h: slice(0, LANES + ONES_ROWS), bias_row=None)
    lqk = lqk_ref[...]
    lam = (jnp.exp(jnp.sum(lqk[0:1] * lqk[1:2], axis=1, keepdims=True))
           - jnp.exp(jnp.sum(lqk[2:3] * lqk[3:4], axis=1, keepdims=True)) + lam_init)
    att = [acc_ref[h, :LANES, :] / acc_ref[h, LANES:LANES + 1, :] for h in range(2)]
    o = (att[0] - lam * att[1]).T
    ms = jnp.mean(o * o, axis=1, keepdims=True)
    o = o * lax.rsqrt(ms + NORM_EPS) * sg_ref[...] * (1.0 - lam_init)
    o_ref[0] = (o * _silu(g_ref[0])).astype(o_ref.dtype)


def diff_attention(q, k, v, gate, lqk, subln_g, lam_init):
    b, s, w = q.shape
    t, tq = ATTN_TILE, Q_TILE
    nb = s // t
    tile = pl.BlockSpec((1, tq, LANES), lambda bi, h, i: (bi, i, h))
    return pl.pallas_call(
        functools.partial(_diff_body, lam_init=lam_init), grid=(b, w // LANES, s // tq),
        in_specs=[pl.BlockSpec((1, LANES, tq), lambda bi, h, i: (bi, h, i)),
                  pl.BlockSpec((1, s, LANES), lambda bi, h, i: (bi, 0, h)),
                  pl.BlockSpec((1, 1, nb, LANES + ONES_ROWS, t), lambda bi, h, i: (bi, h, 0, 0, 0)),
                  tile, pl.BlockSpec((4, HEAD_DIM), lambda bi, h, i: (0, 0)),
                  pl.BlockSpec((1, LANES), lambda bi, h, i: (0, 0))],
        out_specs=tile, out_shape=jax.ShapeDtypeStruct((b, s, w), BF16),
        scratch_shapes=_flash_scratch(LANES),
        compiler_params=_cparams(("parallel", "parallel", "arbitrary")), name="diff_attention",
    )(jnp.swapaxes(q, 1, 2), k, _blocked_t(v, LANES), gate, lqk, subln_g.reshape(1, LANES).astype(F32))


def _mem_body(q_ref, km_ref, vm_ref, g_ref, qg_ref, kg_ref, o_ref):
    gm = _head_mean_matrix(LANES)
    lo = _lane_lo()
    masks = (lo, jnp.logical_not(lo))
    q = q_ref[0]
    q = q * lax.rsqrt(_mm_sel_r(q * q, gm) + NORM_EPS) * qg_ref[...]
    k = km_ref[0]
    k = (k * lax.rsqrt(_mm_sel_r(k * k, gm) + NORM_EPS) * kg_ref[...]).astype(BF16)
    v = vm_ref[0]
    outs = []
    for h in range(2):
        qh = (jnp.where(masks[h], q, 0.0) * ATTN_SCALE).astype(BF16)
        m, l, acc = _softmax_first(_mm_nt(qh, k), v)
        outs.append(acc / l)
    out = jnp.where(lo, outs[0], outs[1])
    o_ref[0] = (out * _silu(g_ref[0])).astype(o_ref.dtype)


def mem_attention(q, km, vm, gate, q_gain, k_gain):
    b, s, w = q.shape
    t = PREP_ROWS
    tile = pl.BlockSpec((1, t, LANES), lambda bi, p, i: (bi, i, p))
    mem = pl.BlockSpec((1, N_MEM, LANES), lambda bi, p, i: (bi, 0, p))
    gain = pl.BlockSpec((1, LANES), lambda bi, p, i: (0, 0))
    rep = LANES // HEAD_DIM
    return pl.pallas_call(
        _mem_body, grid=(b, w // LANES, s // t),
        in_specs=[tile, mem, mem, tile, gain, gain],
        out_specs=tile, out_shape=jax.ShapeDtypeStruct((b, s, w), BF16),
        compiler_params=_cparams(("parallel", "parallel", "parallel")), name="mem_attention",
    )(q, km, vm, gate, jnp.tile(q_gain.astype(F32), rep).reshape(1, LANES),
      jnp.tile(k_gain.astype(F32), rep).reshape(1, LANES))


def _shift(cur, prev8, mu, first):
    rows = lax.broadcasted_iota(jnp.int32, cur.shape, 0)
    before = jnp.where(first, 0.0, prev8[7:8, :])
    prev = jnp.where(rows == 0, before, pltpu.roll(cur, 1, 0))
    return cur + (prev - cur) * mu


def _rwkv_prep_body(r_ref, k_ref, v_ref, lo_ref, rp_ref, kp_ref, vp_ref, lp_ref, mu_ref, mul_ref, w0_ref, w2_ref,
                    a0_ref, a2_ref, kk_ref, ka_ref, rk_ref,
                    ro_ref, lw_ref, ko_ref, vo_ref, kn_ref, ao_ref, bo_ref):
    first = pl.program_id(1) == 0
    mu = mu_ref[...]
    r = _shift(r_ref[0], rp_ref[0], mu[0:1], first)
    k = _shift(k_ref[0], kp_ref[0], mu[1:2], first)
    v = _shift(v_ref[0], vp_ref[0], mu[2:3], first)
    lora = _shift(lo_ref[0], lp_ref[0], mul_ref[...], first)
    z = w0_ref[...] + _mm_hi(jnp.tanh(lora), w2_ref[...])
    w_log = -(jnp.maximum(-z, 0.0) + jnp.log1p(jnp.exp(-jnp.abs(z)))) - 0.5
    a = jax.nn.sigmoid(a0_ref[...] + _mm_hi(lora, a2_ref[...]))
    kk = k * kk_ref[...]
    k2 = k * (1.0 + (a - 1.0) * ka_ref[...])
    ones = _head_mean_matrix(B_W) * float(HEAD_DIM)
    kk = kk * lax.rsqrt(jnp.maximum(_mm_sel_r(kk * kk, ones), 1e-24))
    ro_ref[0] = r
    lw_ref[0] = -jnp.exp(w_log)
    ko_ref[0] = k2
    vo_ref[0] = v
    kn_ref[0] = kk
    ao_ref[0] = a
    bo_ref[0] = _mm_sel_r(r * k2 * rk_ref[...], ones) * v


def rwkv_prep(r, k, v, lora, mu, w0, w2, a0, a2, k_k, k_a, r_k):
    b, s, w = r.shape
    t = ROW_TILE
    tile = pl.BlockSpec((1, t, w), lambda bi, i: (bi, i, 0))
    ltile = pl.BlockSpec((1, t, LANES), lambda bi, i: (bi, i, 0))
    prev_idx = lambda bi, i: (bi, jnp.maximum(i * (t // 8) - 1, 0), 0)
    ptile = pl.BlockSpec((1, 8, w), prev_idx)
    pltile = pl.BlockSpec((1, 8, LANES), prev_idx)
    const = lambda shape: pl.BlockSpec(shape, lambda bi, i: (0, 0))
    mu3 = jnp.stack([mu[:w], mu[w:2 * w], mu[2 * w:3 * w]]).astype(F32)
    mul = jnp.zeros((1, LANES), F32).at[0, :2 * B_LORA].set(mu[3 * w:])
    w2p = jnp.zeros((LANES, w), F32).at[:B_LORA].set(w2)
    a2p = jnp.zeros((LANES, w), F32).at[B_LORA:2 * B_LORA].set(a2)
    row = lambda p: p.reshape(1, w).astype(F32)
    return pl.pallas_call(
        _rwkv_prep_body, grid=(b, s // t),
        in_specs=[tile, tile, tile, ltile, ptile, ptile, ptile, pltile, const((3, w)), const((1, LANES)),
                  const((1, w)), const((LANES, w)), const((1, w)), const((LANES, w)), const((1, w)), const((1, w)),
                  const((1, w))],
        out_specs=[tile] * 7, out_shape=[jax.ShapeDtypeStruct((b, s, w), F32)] * 7,
        compiler_params=_cparams(("parallel", "parallel")), name="rwkv_prep",
    )(r, k, v, lora, r, k, v, lora, mu3, mul, row(w0), w2p, row(a0), a2p, row(k_k), row(k_a), row(r_k))


def _tri(n, strict):
    r = lax.broadcasted_iota(jnp.int32, (n, n), 0)
    c = lax.broadcasted_iota(jnp.int32, (n, n), 1)
    return (c < r) if strict else (c <= r)


def _block_diag_mask():
    r = lax.broadcasted_iota(jnp.int32, (LANES, LANES), 0) // HEAD_DIM
    c = lax.broadcasted_iota(jnp.int32, (LANES, LANES), 1) // HEAD_DIM
    return r == c


def _rwkv_scan_body(r_ref, lw_ref, k_ref, v_ref, kn_ref, a_ref, bo_ref, g_ref, lg_ref, lb_ref, o_ref, s_ref, *,
                    nbatch, npair):
    @pl.when(pl.program_id(0) == 0)
    def _():
        s_ref[...] = jnp.zeros_like(s_ref)

    c = CHUNK
    lo = _lane_lo()
    masks = (lo, jnp.logical_not(lo))
    strict = _tri(c, True)
    incl = _tri(c, False)
    ltri = incl.astype(F32)
    eye = (lax.broadcasted_iota(jnp.int32, (c, c), 0) == lax.broadcasted_iota(jnp.int32, (c, c), 1)).astype(F32)
    bd = _block_diag_mask()
    gmean = _head_mean_matrix(LANES)

    def chain(sl, bi, pi):
        cols = slice(pi * LANES, (pi + 1) * LANES)
        idx = bi * npair + pi
        r, lw, k, v, kn, a = (ref[bi, sl, cols] for ref in (r_ref, lw_ref, k_ref, v_ref, kn_ref, a_ref))
        cw = _mm_sel_l(ltri, lw)
        yield
        cl = cw[c - 1:c, :]
        at = -kn * jnp.exp(cw - lw)
        e_neg = jnp.exp(-cw)
        bt = kn * a * e_neg
        kt = k * e_neg
        rt = r * jnp.exp(cw)
        e_end = jnp.exp(cl - cw)
        s0 = s_ref[idx]
        rhs = _mm_nt(at, s0)
        ys0 = _mm_nt(rt, s0)
        yield
        ah = [jnp.where(mk, at, 0.0) for mk in masks]
        rh = [jnp.where(mk, rt, 0.0) for mk in masks]
        n = [jnp.where(strict, _mm_nt(x, bt), 0.0) for x in ah]
        yield
        aak = [jnp.where(strict, _mm_nt(x, kt), 0.0) for x in ah]
        yield
        arb = [jnp.where(incl, _mm_nt(x, bt), 0.0) for x in rh]
        yield
        ark = [jnp.where(incl, _mm_nt(x, kt), 0.0) for x in rh]
        yield
        xs = [rhs + _mm(x, v) for x in aak]
        yv = [_mm(x, v) for x in ark]
        yield
        tinv = [eye + x for x in n]
        p = n
        for _ in range(5):
            p = [_mm(x, x) for x in p]
            yield
            tinv = [x + _mm(x, y) for x, y in zip(tinv, p)]
            yield
        u = jnp.where(lo, _mm(tinv[0], xs[0]), _mm(tinv[1], xs[1]))
        yield
        y = ys0 + jnp.where(lo, _mm(arb[0], u) + yv[0], _mm(arb[1], u) + yv[1])
        upd = _mm_tn(u, kn * a * e_end) + _mm_tn(v, k * e_end)
        yield
        s_ref[idx] = s0 * jnp.exp(cl) + jnp.where(bd, upd, 0.0)
        mean = _mm_sel_r(y, gmean)
        yield
        d = y - mean
        var = _mm_sel_r(d * d, gmean)
        yield
        yn = d * lax.rsqrt(var + RWKV_GN_EPS) * lg_ref[:, cols] + lb_ref[:, cols] + bo_ref[bi, sl, cols]
        o_ref[bi, sl, cols] = (yn * _silu(g_ref[bi, sl, cols])).astype(o_ref.dtype)

    def chunk(ci, carry):
        sl = pl.ds(pl.multiple_of(ci * c, c), c)
        gens = [chain(sl, bi, pi) for bi in range(nbatch) for pi in range(npair)]
        for _ in itertools.zip_longest(*gens):
            pass
        return carry

    lax.fori_loop(0, SCAN_ROWS // c, chunk, 0)


def rwkv_scan(r, lw, k, v, kn, a, bonus, gate, lnx_g, lnx_b):
    b, s, w = r.shape
    t = SCAN_ROWS
    tile = pl.BlockSpec((b, t, w), lambda i: (0, i, 0))
    vec = pl.BlockSpec((1, w), lambda i: (0, 0))
    return pl.pallas_call(
        functools.partial(_rwkv_scan_body, nbatch=b, npair=w // LANES), grid=(s // t,),
        in_specs=[tile] * 8 + [vec, vec],
        out_specs=tile, out_shape=jax.ShapeDtypeStruct((b, s, w), BF16),
        scratch_shapes=[pltpu.VMEM((b * (w // LANES), LANES, LANES), F32)],
        compiler_params=_cparams(("arbitrary",)), name="rwkv_scan",
    )(r, lw, k, v, kn, a, bonus, gate, lnx_g.reshape(1, w).astype(F32), lnx_b.reshape(1, w).astype(F32))


def _hgrn_body(q_ref, f_ref, i_ref, g_ref, lb_ref, gn_ref, o_ref, s_ref, *, nbatch, npair):
    @pl.when(pl.program_id(0) == 0)
    def _():
        s_ref[...] = jnp.zeros_like(s_ref)

    c = CHUNK
    lo = _lane_lo()
    masks = (lo, jnp.logical_not(lo))
    ltri = _tri(c, False).astype(F32)
    bd = _block_diag_mask()
    gmean = _head_mean_matrix(LANES)
    head_ones = gmean * float(HEAD_DIM)
    rows = lax.broadcasted_iota(jnp.int32, (SUB, 1), 0)

    def chain(sl, bi, pi):
        cols = slice(pi * LANES, (pi + 1) * LANES)
        idx = bi * npair + pi
        lb = lb_ref[:, cols]
        log_lb = jnp.log(lb)
        log_1m = jnp.log1p(-lb)
        q, fr, v = q_ref[bi, sl, cols], f_ref[bi, sl, cols], i_ref[bi, sl, cols]
        log_sig = jnp.minimum(fr, 0.0) - jnp.log1p(jnp.exp(-jnp.abs(fr)))
        z = log_1m + log_sig
        hi = jnp.maximum(log_lb, z)
        log_f = hi + jnp.log1p(jnp.exp(-jnp.abs(log_lb - z)))
        k = (1.0 - lb) * jax.nn.sigmoid(-fr)
        bc = _mm_sel_l(ltri, log_f)
        yield
        bl = bc[c - 1:c, :]
        s0 = s_ref[idx]
        outs = []
        for sb in range(c // SUB):
            r0 = sb * SUB
            qs, bs, ks, vs = (x[r0:r0 + SUB] for x in (q, bc, k, v))
            o = _mm_nt(qs * jnp.exp(bs), s0)
            if sb > 0:
                ref = bc[r0 - 1:r0, :]
                qh = qs * jnp.exp(bs - ref)
                kh = k[:r0] * jnp.exp(ref - bc[:r0])
                vh = v[:r0]
                sc = [_mm_nt(jnp.where(mk, qh, 0.0), kh) for mk in masks]
                yield
                o = o + jnp.where(lo, _mm(sc[0], vh), _mm(sc[1], vh))
            pair = jnp.concatenate(
                [qs * ks[si:si + 1, :] * jnp.exp(jnp.where(rows >= si, bs - bs[si:si + 1, :], NEG_INF))
                 for si in range(SUB)], axis=0)
            score = _mm_sel_r(pair, head_ones)
            yield
            for si in range(SUB):
                o = o + score[si * SUB:(si + 1) * SUB] * vs[si:si + 1, :]
            outs.append(o)
        od = jnp.concatenate(outs, axis=0)
        s_ref[idx] = s0 * jnp.exp(bl) + jnp.where(bd, _mm_tn(v, k * jnp.exp(bl - bc)), 0.0)
        ms = _mm_sel_r(od * od, gmean)
        yield
        on = od * lax.rsqrt(ms + NORM_EPS) * gn_ref[...]
        o_ref[bi, sl, cols] = (on * _silu(g_ref[bi, sl, cols])).astype(o_ref.dtype)

    def chunk(ci, carry):
        sl = pl.ds(pl.multiple_of(ci * c, c), c)
        gens = [chain(sl, bi, pi) for bi in range(nbatch) for pi in range(npair)]
        for _ in itertools.zip_longest(*gens):
            pass
        return carry

    lax.fori_loop(0, SCAN_ROWS // c, chunk, 0)


def hgrn2(q, f, iv, gate, lb, gn_g):
    b, s, w = q.shape
    t = SCAN_ROWS
    tile = pl.BlockSpec((b, t, w), lambda i: (0, i, 0))
    rep = LANES // HEAD_DIM
    return pl.pallas_call(
        functools.partial(_hgrn_body, nbatch=b, npair=w // LANES), grid=(s // t,),
        in_specs=[tile] * 4 + [pl.BlockSpec((1, w), lambda i: (0, 0)), pl.BlockSpec((1, LANES), lambda i: (0, 0))],
        out_specs=tile, out_shape=jax.ShapeDtypeStruct((b, s, w), BF16),
        scratch_shapes=[pltpu.VMEM((b * (w // LANES), LANES, LANES), F32)],
        compiler_params=_cparams(("arbitrary",)), name="hgrn2",
    )(q, f, iv, gate, lb.reshape(1, w).astype(F32), jnp.tile(gn_g.astype(F32), rep).reshape(1, LANES))


def _memory_kv(memf, b, g, w_kv):
    wb = w_kv.astype(BF16)
    km, vm = rms_proj(memf, g, [wb[:, :M_W], wb[:, M_W:]], [F32, BF16])
    return km.reshape(b, N_MEM, M_W), vm.reshape(b, N_MEM, M_W)


def _even_layer(xf, b, s, km, vm, tables, ln_g, w_in, w_out, a_qn_g, a_kn_g, m_qn_g, m_kn_g,
                mu, w0, w2, a0, a2, k_k, k_a, r_k, lnx_g, lnx_b):
    wb = w_in.astype(BF16)
    edges = [0]
    for width in (A_W, A_W, A_W, A_W, B_W, B_W, B_W, 2 * B_LORA, B_W, M_W, M_W):
        edges.append(edges[-1] + width)
    ws = [wb[:, edges[n]:edges[n + 1]] for n in range(11)]
    ws[7] = jnp.pad(ws[7], ((0, 0), (0, LANES - 2 * B_LORA)))
    dts = [F32, F32, BF16, F32, F32, F32, F32, F32, F32, F32, F32]
    qa, ka, va, ga, rr, rk, rv, lora, gb, qm, gm = rms_proj(xf, ln_g, ws, dts)
    sh = lambda t: t.reshape(b, s, t.shape[-1])
    q = qk_prep(sh(qa), a_qn_g, tables, F32)
    k, kmean = qk_prep(sh(ka), a_kn_g, tables, BF16, want_block_mean=True)
    oa = moba_attention(q, k, sh(va), kmean, sh(ga))
    pre = rwkv_prep(sh(rr), sh(rk), sh(rv), sh(lora), mu, w0, w2, a0, a2, k_k, k_a, r_k.reshape(-1))
    ob = rwkv_scan(*pre, sh(gb), lnx_g, lnx_b)
    om = mem_attention(sh(qm), km, vm, sh(gm), m_qn_g, m_kn_g)
    wo = w_out.astype(BF16)
    fl = lambda t: t.reshape(b * s, t.shape[-1])
    return out_proj(xf, [fl(oa), fl(ob), fl(om)], [wo[:A_W], wo[A_W:A_W + B_W], wo[A_W + B_W:]])


def _odd_layer(xf, b, s, km, vm, tables, li, lb, ln_g, w_in, w_out, c_qn_g, c_kn_g, lqk, subln_g, d_gn_g,
               m_qn_g, m_kn_g):
    wb = w_in.astype(BF16)
    edges = [0]
    for width in (C_W, C_W, C_W, C_W, D_W, D_W, D_W, D_W, M_W, M_W):
        edges.append(edges[-1] + width)
    ws = [wb[:, edges[n]:edges[n + 1]] for n in range(10)]
    dts = [F32, F32, BF16, F32, F32, F32, F32, F32, F32, F32]
    qc, kc, vc, gc, qd, fd, idd, gd, qm, gm = rms_proj(xf, ln_g, ws, dts)
    sh = lambda t: t.reshape(b, s, t.shape[-1])
    q = qk_prep(sh(qc), c_qn_g, tables, BF16, scale=ATTN_SCALE * LOG2E)
    k = qk_prep(sh(kc), c_kn_g, tables, BF16)
    lam_init = 0.8 - 0.6 * math.exp(-0.3 * li)
    oc = diff_attention(q, k, sh(vc), sh(gc), lqk, subln_g, lam_init)
    od = hgrn2(sh(qd), sh(fd), sh(idd), sh(gd), lb, d_gn_g)
    om = mem_attention(sh(qm), km, vm, sh(gm), m_qn_g, m_kn_g)
    wo = w_out.astype(BF16)
    fl = lambda t: t.reshape(b * s, t.shape[-1])
    return out_proj(xf, [fl(oc), fl(od), fl(om)], [wo[:C_W], wo[C_W:C_W + D_W], wo[C_W + D_W:]])


def kernel(x, mem, ln_g, mem_ln_g, w_mem_kv, m_qn_g, m_kn_g, e_w_in, e_w_out, a_qn_g, a_kn_g, b_mu, b_w0, b_w2, b_a0, b_a2, b_k_k, b_k_a, b_r_k, b_lnx_g, b_lnx_b, o_w_in, o_w_out, c_qn_g, c_kn_g, c_lq1, c_lk1, c_lq2, c_lk2, c_subln_g, d_lb, d_gn_g):
    b, s, d = x.shape
    depth = ln_g.shape[0]
    tables = rope_tables_lanes(s)
    lbs = jax.nn.softmax(d_lb.astype(F32), axis=0)
    lbs = jnp.cumsum(lbs, axis=0) - lbs[0:1]
    xf = x.reshape(b * s, d)
    memf = mem.reshape(b * N_MEM, d)
    for li in range(depth):
        j = li // 2
        km, vm = _memory_kv(memf, b, mem_ln_g[li], w_mem_kv[li])
        if li % 2 == 0:
            xf = _even_layer(xf, b, s, km, vm, tables, ln_g[li], e_w_in[j], e_w_out[j], a_qn_g[j], a_kn_g[j],
                             m_qn_g[li], m_kn_g[li], b_mu[j], b_w0[j], b_w2[j], b_a0[j], b_a2[j], b_k_k[j],
                             b_k_a[j], b_r_k[j], b_lnx_g[j], b_lnx_b[j])
        else:
            lqk = jnp.stack([c_lq1[j], c_lk1[j], c_lq2[j], c_lk2[j]]).astype(F32)
            xf = _odd_layer(xf, b, s, km, vm, tables, li, jnp.maximum(lbs[j], 0.0), ln_g[li], o_w_in[j],
                            o_w_out[j], c_qn_g[j], c_kn_g[j], lqk, c_subln_g[j], d_gn_g[j], m_qn_g[li], m_kn_g[li])
    return xf.reshape(b, s, d)
```

```python
import functools
import itertools
import math

import jax
import jax.numpy as jnp
from jax import lax
from jax.experimental import pallas as pl
from jax.experimental.pallas import tpu as pltpu

F32 = jnp.float32
BF16 = jnp.bfloat16
HIGHEST = lax.Precision.HIGHEST

N_MEM = 256
HEAD_DIM = 64
ROPE_THETA = 500000.0
ROPE_DIM = HEAD_DIM // 4
NORM_EPS = 1e-6
NEG_INF = -1e30
A_HEADS = 6
MOBA_BLOCK = 256
MOBA_TOPK = 3
B_HEADS = 6
B_LORA = 32
RWKV_GN_EPS = 64e-5
C_HEADS = 4
D_HEADS = 4
M_HEADS = 4
A_W = A_HEADS * HEAD_DIM
B_W = B_HEADS * HEAD_DIM
C_W = C_HEADS * 2 * HEAD_DIM
D_W = D_HEADS * HEAD_DIM
M_W = M_HEADS * HEAD_DIM
ATTN_SCALE = HEAD_DIM ** -0.5

LANES = 128
VMEM_LIMIT = 48 * 1024 * 1024

ROW_TILE = 512
PREP_ROWS = 512
LOG2E = math.log2(math.e)
ONES_ROWS = 16
ATTN_TILE = 256
Q_TILE = 2 * ATTN_TILE
FLASH_UNROLL = 4
CHUNK = 64
SUB = 16
SCAN_ROWS = 256

_NT = (((1,), (1,)), ((), ()))
_TN = (((0,), (0,)), ((), ()))


def _cparams(sem):
    return pltpu.CompilerParams(dimension_semantics=sem, vmem_limit_bytes=VMEM_LIMIT)


def _mm(a, b):
    return jnp.dot(a.astype(BF16), b.astype(BF16), preferred_element_type=F32)


def _mm_nt(a, b):
    return lax.dot_general(a.astype(BF16), b.astype(BF16), _NT, preferred_element_type=F32)


def _mm_tn(a, b):
    return lax.dot_general(a.astype(BF16), b.astype(BF16), _TN, preferred_element_type=F32)


def _mm_hi(a, b):
    return jnp.dot(a, b, precision=HIGHEST, preferred_element_type=F32)


def _split3(x):
    x1 = x.astype(BF16)
    r1 = x - x1.astype(F32)
    x2 = r1.astype(BF16)
    x3 = (r1 - x2.astype(F32)).astype(BF16)
    return x1, x2, x3


def _mm_sel_r(x, sel):
    sb = sel.astype(BF16)
    x1, x2, x3 = _split3(x)
    return (jnp.dot(x3, sb, preferred_element_type=F32) + jnp.dot(x2, sb, preferred_element_type=F32)
            + jnp.dot(x1, sb, preferred_element_type=F32))


def _mm_sel_l(sel, x):
    sb = sel.astype(BF16)
    x1, x2, x3 = _split3(x)
    return (jnp.dot(sb, x3, preferred_element_type=F32) + jnp.dot(sb, x2, preferred_element_type=F32)
            + jnp.dot(sb, x1, preferred_element_type=F32))


def _silu(x):
    return x * jax.nn.sigmoid(x)


def _lane_lo(width=LANES):
    lane = lax.broadcasted_iota(jnp.int32, (1, width), 1)
    return (lane % LANES) < HEAD_DIM


def _head_mean_matrix(width):
    r = lax.broadcasted_iota(jnp.int32, (width, width), 0) // HEAD_DIM
    c = lax.broadcasted_iota(jnp.int32, (width, width), 1) // HEAD_DIM
    return jnp.where(r == c, 1.0 / HEAD_DIM, 0.0).astype(F32)


def _proj_body(*refs, kinds, has_rope):
    it = iter(refs)
    x_ref, g_ref = next(it), next(it)
    if has_rope:
        cos, sin1, sin2 = next(it)[...], next(it)[...], next(it)[...]
        gm = _head_mean_matrix(LANES)
    w_refs = [next(it) for _ in kinds]
    gain_refs = [next(it) if kind[0] == "qk" else None for kind in kinds]
    x = x_ref[...]
    tm = x.shape[0]
    ms = jnp.mean(x * x, axis=-1, keepdims=True)
    h = (x * lax.rsqrt(ms + NORM_EPS) * g_ref[...]).astype(BF16)
    for kind, w_ref, gain_ref in zip(kinds, w_refs, gain_refs):
        acc = jnp.dot(h, w_ref[...], preferred_element_type=F32)
        n_blocks = acc.shape[1] // LANES
        o_ref = next(it)
        if kind[0] == "plain":
            o_ref[...] = acc.astype(o_ref.dtype)
        elif kind[0] == "vt":
            for kt in range(tm // ATTN_TILE):
                for c in range(n_blocks):
                    tile = acc[kt * ATTN_TILE:(kt + 1) * ATTN_TILE, c * LANES:(c + 1) * LANES]
                    o_ref[kt, c] = tile.T.astype(o_ref.dtype)
        else:
            _, _, _, scale, transposed, block_mean = kind
            mean_ref = next(it) if block_mean else None
            for c in range(n_blocks):
                cols = slice(c * LANES, (c + 1) * LANES)
                y = acc[:, cols]
                y = y * lax.rsqrt(_mm_sel_r(y * y, gm) + NORM_EPS) * gain_ref[...]
                yr = y * cos + pltpu.roll(y, LANES - ROPE_DIM // 2, 1) * sin1 + pltpu.roll(y, ROPE_DIM // 2, 1) * sin2
                if transposed:
                    o_ref[0, cols, :] = (yr * scale).T.astype(o_ref.dtype)
                else:
                    o_ref[:, cols] = (yr * scale).astype(o_ref.dtype)
                if block_mean:
                    for rb in range(tm // MOBA_BLOCK):
                        blk = yr[rb * MOBA_BLOCK:(rb + 1) * MOBA_BLOCK]
                        mean_ref[rb, :, cols] = jnp.mean(blk, axis=0, keepdims=True)


def rms_proj(x2d, g, ws, kinds, tables=None, seq=None):
    n, d = x2d.shape
    tm = min(ROW_TILE, n)
    has_rope = tables is not None
    nt = seq // tm if has_rope else 1
    row = lambda i: (i, 0)
    const = lambda i: (0, 0)
    in_specs = [pl.BlockSpec((tm, d), row), pl.BlockSpec((1, d), const)]
    args = [x2d, g.reshape(1, d).astype(F32)]
    if has_rope:
        in_specs += [pl.BlockSpec((tm, LANES), lambda i: (i % nt, 0))] * 3
        args += list(tables)
    in_specs += [pl.BlockSpec(w.shape, const) for w in ws]
    args += list(ws)
    out_specs, out_shape = [], []
    for w, kind in zip(ws, kinds):
        nw = w.shape[1]
        if kind[0] == "plain":
            out_specs.append(pl.BlockSpec((tm, nw), row))
            out_shape.append(jax.ShapeDtypeStruct((n, nw), kind[1]))
        elif kind[0] == "vt":
            blk = (tm // ATTN_TILE, nw // LANES, LANES, ATTN_TILE)
            out_specs.append(pl.BlockSpec(blk, lambda i: (i, 0, 0, 0)))
            out_shape.append(jax.ShapeDtypeStruct((n // ATTN_TILE,) + blk[1:], BF16))
        else:
            _, gain, dtype, _, transposed, block_mean = kind
            in_specs.append(pl.BlockSpec((1, LANES), const))
            args.append(jnp.tile(gain.astype(F32), LANES // HEAD_DIM).reshape(1, LANES))
            if transposed:
                out_specs.append(pl.BlockSpec((1, nw, tm), lambda i: (i // nt, 0, i % nt)))
                out_shape.append(jax.ShapeDtypeStruct((n // seq, nw, seq), dtype))
            else:
                out_specs.append(pl.BlockSpec((tm, nw), row))
                out_shape.append(jax.ShapeDtypeStruct((n, nw), dtype))
            if block_mean:
                out_specs.append(pl.BlockSpec((tm // MOBA_BLOCK, 1, nw), lambda i: (i, 0, 0)))
                out_shape.append(jax.ShapeDtypeStruct((n // MOBA_BLOCK, 1, nw), F32))
    static_kinds = tuple(k if k[0] != "qk" else (k[0], None) + tuple(k[2:]) for k in kinds)
    return pl.pallas_call(
        functools.partial(_proj_body, kinds=static_kinds, has_rope=has_rope),
        grid=(n // tm,), in_specs=in_specs, out_specs=out_specs, out_shape=out_shape,
        compiler_params=_cparams(("parallel",)), name="rms_proj",
    )(*args)


def _out_proj_body(x_ref, *refs, n_in):
    o_ref = refs[-1]
    acc = x_ref[...]
    for m_ref, w_ref in zip(refs[:n_in], refs[n_in:2 * n_in]):
        acc = acc + jnp.dot(m_ref[...], w_ref[...], preferred_element_type=F32)
    o_ref[...] = acc


def out_proj(x2d, parts, ws):
    n, d = x2d.shape
    tm = min(ROW_TILE, n)
    in_specs = [pl.BlockSpec((tm, d), lambda i: (i, 0))]
    in_specs += [pl.BlockSpec((tm, p.shape[1]), lambda i: (i, 0)) for p in parts]
    in_specs += [pl.BlockSpec(w.shape, lambda i: (0, 0)) for w in ws]
    return pl.pallas_call(
        functools.partial(_out_proj_body, n_in=len(parts)),
        grid=(n // tm,), in_specs=in_specs, out_specs=pl.BlockSpec((tm, d), lambda i: (i, 0)),
        out_shape=jax.ShapeDtypeStruct((n, d), F32),
        compiler_params=_cparams(("parallel",)), name="out_proj",
    )(x2d, *parts, *ws)


def rope_tables_lanes(seq):
    pos = jnp.arange(seq, dtype=F32)
    inv = 1.0 / (ROPE_THETA ** (jnp.arange(0, ROPE_DIM, 2, dtype=F32) / ROPE_DIM))
    ang = pos[:, None] * inv[None, :]
    cos, sin = jnp.cos(ang), jnp.sin(ang)
    half = ROPE_DIM // 2
    ones = jnp.ones((seq, HEAD_DIM - ROPE_DIM), F32)
    zeros_h = jnp.zeros((seq, half), F32)
    zeros_r = jnp.zeros((seq, HEAD_DIM - ROPE_DIM), F32)
    c = jnp.concatenate([cos, cos, ones], axis=1)
    s1 = jnp.concatenate([-sin, zeros_h, zeros_r], axis=1)
    s2 = jnp.concatenate([zeros_h, sin, zeros_r], axis=1)
    rep = LANES // HEAD_DIM
    return jnp.tile(c, (1, rep)), jnp.tile(s1, (1, rep)), jnp.tile(s2, (1, rep))


def _softmax_first(s, v):
    m = jnp.max(s, axis=1, keepdims=True)
    p = jnp.exp(s - m)
    return m, jnp.sum(p, axis=1, keepdims=True), _mm(p, v)


def _flash_t(i, k_ref, vt_ref, qm_ref, bufs, m_ref, acc_ref, v_rows, bias_row):
    t = ATTN_TILE
    (s0, s1), (p0, p1), (a0, a1) = bufs
    n_real = 2 * i + 1
    last = jnp.maximum(2 * i - 1, 0)
    ncol = Q_TILE // LANES

    def key_tile(x):
        own = jnp.where(x == 0, 2 * i + 1, 2 * i)
        return jnp.where(x < 2, own, jnp.minimum(x - 2, last))

    def a_part(x, h, s_w):
        kb = jnp.minimum(x - 2, last)
        k_blk = k_ref[0, pl.ds(pl.multiple_of(kb * t, t), t), :]
        s = jnp.dot(k_blk, qm_ref[h], preferred_element_type=F32)
        if bias_row is not None:
            s = s + bias_row(h, kb)
        s_w[h] = s

    def b_part(x, h, c, s_r, p_w, a_w):
        real = x <= n_real
        cs = slice(c * LANES, (c + 1) * LANES)
        s = s_r[h, :, cs]
        m_old = m_ref[h, :, cs]
        m_top = jnp.maximum(m_old, jnp.max(s, axis=0, keepdims=True))
        m_new = jnp.where(real, m_top, m_old)
        m_ref[h, :, cs] = m_new
        a_w[h, :, cs] = jnp.exp2(m_old - m_new)
        p_w[h, :, cs] = jnp.exp2(s - m_top).astype(BF16)

    def c_part(x, h, p_r, a_r):
        v_blk = jnp.concatenate([vt_ref[key_tile(x), 0, v_rows(h), :], jnp.ones((ONES_ROWS, t), BF16)], axis=0)
        v_blk = jnp.where(x <= n_real, v_blk, jnp.zeros_like(v_blk))
        acc_ref[h] = a_r[h] * acc_ref[h] + jnp.dot(v_blk, p_r[h], preferred_element_type=F32)

    def stage_a(x, s_w):
        for h in range(2):
            a_part(x, h, s_w)

    def stage_b(x, s_r, p_w, a_w):
        for h in range(2):
            for c in range(ncol):
                b_part(x, h, c, s_r, p_w, a_w)

    def stage_c(x, p_r, a_r):
        for h in range(2):
            c_part(x, h, p_r, a_r)

    m_ref[...] = jnp.full(m_ref.shape, NEG_INF, F32)
    acc_ref[...] = jnp.zeros(acc_ref.shape, F32)
    kpos = lax.broadcasted_iota(jnp.int32, (t, Q_TILE), 0)
    qpos = lax.broadcasted_iota(jnp.int32, (t, Q_TILE), 1)
    late = qpos >= t
    late_row = lax.broadcasted_iota(jnp.int32, (1, Q_TILE), 1) >= t
    allow_hi = jnp.logical_and(late, kpos <= qpos - t)
    allow_lo = jnp.logical_or(late, kpos <= qpos)
    k_hi = k_ref[0, pl.ds(pl.multiple_of((2 * i + 1) * t, t), t), :]
    k_lo = k_ref[0, pl.ds(pl.multiple_of(2 * i * t, t), t), :]
    for h in range(2):
        s0[h] = jnp.where(allow_hi, jnp.dot(k_hi, qm_ref[h], preferred_element_type=F32), NEG_INF)
        s_lo = jnp.dot(k_lo, qm_ref[h], preferred_element_type=F32)
        if bias_row is not None:
            s_lo = s_lo + jnp.where(late_row, bias_row(h, 2 * i), 0.0)
        s1[h] = jnp.where(allow_lo, s_lo, NEG_INF)
    bufs_of = ((s0, p0, a0), (s1, p1, a1))

    def step(x, k):
        s_w, p_r, a_r = bufs_of[k % 2]
        s_r, p_w, a_w = bufs_of[(k + 1) % 2]
        stage_c(x - 2, p_r, a_r)
        stage_b(x - 1, s_r, p_w, a_w)
        stage_a(x, s_w)

    stage_b(0, s0, p0, a0)
    u_steps = FLASH_UNROLL

    def unrolled(u, carry):
        for k in range(u_steps):
            step(u_steps * u + 2 + k, k)
        return carry

    lax.fori_loop(0, (2 * i + 2 + u_steps - 1) // u_steps, unrolled, 0)


def _flash_scratch(v_rows):
    t, tq = ATTN_TILE, Q_TILE
    return [pltpu.VMEM((2, LANES, tq), BF16),
            pltpu.VMEM((2, t, tq), F32), pltpu.VMEM((2, t, tq), F32),
            pltpu.VMEM((2, t, tq), BF16), pltpu.VMEM((2, t, tq), BF16),
            pltpu.VMEM((2, 1, tq), F32), pltpu.VMEM((2, 1, tq), F32),
            pltpu.VMEM((2, 1, tq), F32), pltpu.VMEM((2, v_rows + ONES_ROWS, tq), F32)]


def _moba_body(qt_ref, k_ref, vt_ref, km_ref, g_ref, o_ref, bias_ref, qm_ref, s0, s1, p0, p1, a0, a1,
               m_ref, acc_ref, *, nb):
    i = pl.program_id(2)
    t = Q_TILE
    bufs = ((s0, s1), (p0, p1), (a0, a1))
    q_t = qt_ref[0]
    dim_lo = lax.broadcasted_iota(jnp.int32, (LANES, 1), 0) < HEAD_DIM
    lo = _lane_lo()
    km = km_ref[0]
    blk = lax.broadcasted_iota(jnp.int32, (nb, t), 0).astype(F32)
    own = (2 * i + (lax.broadcasted_iota(jnp.int32, (1, t), 1) >= MOBA_BLOCK).astype(jnp.int32)).astype(F32)
    for h in range(2):
        qh = jnp.where(dim_lo if h == 0 else jnp.logical_not(dim_lo), q_t, 0.0)
        qm_ref[h] = (qh * (ATTN_SCALE * LOG2E)).astype(BF16)
        kmh = jnp.where(lo if h == 0 else jnp.logical_not(lo), km, 0.0)
        gate = jnp.dot(kmh, q_t, precision=HIGHEST, preferred_element_type=F32)
        gate = jnp.where(blk < own, gate, NEG_INF)
        sel = jnp.zeros((nb, t), F32)
        for _ in range(MOBA_TOPK):
            mx = jnp.max(gate, axis=0, keepdims=True)
            first = jnp.min(jnp.where(gate == mx, blk, float(nb)), axis=0, keepdims=True)
            hit = blk == first
            sel = jnp.where(jnp.logical_and(hit, first < own), 1.0, sel)
            gate = jnp.where(hit, -jnp.inf, gate)
        bias_ref[h] = jnp.where(sel > 0.0, 0.0, NEG_INF)

    _flash_t(i, k_ref, vt_ref, qm_ref, bufs, m_ref, acc_ref,
             v_rows=lambda h: slice(h * HEAD_DIM, (h + 1) * HEAD_DIM),
             bias_row=lambda h, kb: bias_ref[h, pl.ds(kb, 1), :])
    o_t = jnp.concatenate([acc_ref[h, :HEAD_DIM, :] / acc_ref[h, HEAD_DIM:HEAD_DIM + 1, :] for h in range(2)],
                          axis=0)
    o_ref[0] = (o_t.T * _silu(g_ref[0])).astype(o_ref.dtype)


def moba_attention(q_t, k, v_t, kmean, gate):
    b, w, s = q_t.shape
    t, tq = ATTN_TILE, Q_TILE
    nb = s // t
    tile = pl.BlockSpec((1, tq, LANES), lambda bi, p, i: (bi, i, p))
    return pl.pallas_call(
        functools.partial(_moba_body, nb=nb), grid=(b, w // LANES, s // tq),
        in_specs=[pl.BlockSpec((1, LANES, tq), lambda bi, p, i: (bi, p, i)),
                  pl.BlockSpec((1, s, LANES), lambda bi, p, i: (bi, 0, p)),
                  pl.BlockSpec((nb, 1, LANES, t), lambda bi, p, i: (bi, p, 0, 0)),
                  pl.BlockSpec((1, nb, LANES), lambda bi, p, i: (bi, 0, p)), tile],
        out_specs=tile, out_shape=jax.ShapeDtypeStruct((b, s, w), BF16),
        scratch_shapes=[pltpu.VMEM((2, nb, tq), F32)] + _flash_scratch(HEAD_DIM),
        compiler_params=_cparams(("parallel", "parallel", "arbitrary")), name="moba_attention",
    )(q_t, k, v_t, kmean, gate)


def _diff_body(qt_ref, k_ref, vt_ref, g_ref, lqk_ref, sg_ref, o_ref, qm_ref, s0, s1, p0, p1, a0, a1,
               m_ref, acc_ref, *, lam_init):
    i = pl.program_id(2)
    q_t = qt_ref[0]
    dim_lo = lax.broadcasted_iota(jnp.int32, (LANES, 1), 0) < HEAD_DIM
    for h in range(2):
        qm_ref[h] = jnp.where(dim_lo if h == 0 else jnp.logical_not(dim_lo), q_t, 0)
    _flash_t(i, k_ref, vt_ref, qm_ref, ((s0, s1), (p0, p1), (a0, a1)), m_ref, acc_ref,
             v_rows=lambda h: slice(0, LANES), bias_row=None)
    lqk = lqk_ref[...]
    lam = (jnp.exp(jnp.sum(lqk[0:1] * lqk[1:2], axis=1, keepdims=True))
           - jnp.exp(jnp.sum(lqk[2:3] * lqk[3:4], axis=1, keepdims=True)) + lam_init)
    att = [acc_ref[h, :LANES, :] / acc_ref[h, LANES:LANES + 1, :] for h in range(2)]
    o = (att[0] - lam * att[1]).T
    ms = jnp.mean(o * o, axis=1, keepdims=True)
    o = o * lax.rsqrt(ms + NORM_EPS) * sg_ref[...] * (1.0 - lam_init)
    o_ref[0] = (o * _silu(g_ref[0])).astype(o_ref.dtype)


def diff_attention(q_t, k, v_t, gate, lqk, subln_g, lam_init):
    b, w, s = q_t.shape
    t, tq = ATTN_TILE, Q_TILE
    nb = s // t
    tile = pl.BlockSpec((1, tq, LANES), lambda bi, h, i: (bi, i, h))
    return pl.pallas_call(
        functools.partial(_diff_body, lam_init=lam_init), grid=(b, w // LANES, s // tq),
        in_specs=[pl.BlockSpec((1, LANES, tq), lambda bi, h, i: (bi, h, i)),
                  pl.BlockSpec((1, s, LANES), lambda bi, h, i: (bi, 0, h)),
                  pl.BlockSpec((nb, 1, LANES, t), lambda bi, h, i: (bi, h, 0, 0)),
                  tile, pl.BlockSpec((4, HEAD_DIM), lambda bi, h, i: (0, 0)),
                  pl.BlockSpec((1, LANES), lambda bi, h, i: (0, 0))],
        out_specs=tile, out_shape=jax.ShapeDtypeStruct((b, s, w), BF16),
        scratch_shapes=_flash_scratch(LANES),
        compiler_params=_cparams(("parallel", "parallel", "arbitrary")), name="diff_attention",
    )(q_t, k, v_t, gate, lqk, subln_g.reshape(1, LANES).astype(F32))


def _mem_body(q_ref, km_ref, vm_ref, g_ref, qg_ref, kg_ref, o_ref):
    gm = _head_mean_matrix(LANES)
    lo = _lane_lo()
    masks = (lo, jnp.logical_not(lo))
    q = q_ref[0]
    q = q * lax.rsqrt(_mm_sel_r(q * q, gm) + NORM_EPS) * qg_ref[...]
    k = km_ref[0]
    k = (k * lax.rsqrt(_mm_sel_r(k * k, gm) + NORM_EPS) * kg_ref[...]).astype(BF16)
    v = vm_ref[0]
    outs = []
    for h in range(2):
        qh = (jnp.where(masks[h], q, 0.0) * ATTN_SCALE).astype(BF16)
        m, l, acc = _softmax_first(_mm_nt(qh, k), v)
        outs.append(acc / l)
    out = jnp.where(lo, outs[0], outs[1])
    o_ref[0] = (out * _silu(g_ref[0])).astype(o_ref.dtype)


def mem_attention(q, km, vm, gate, q_gain, k_gain):
    b, s, w = q.shape
    t = PREP_ROWS
    tile = pl.BlockSpec((1, t, LANES), lambda bi, p, i: (bi, i, p))
    mem = pl.BlockSpec((1, N_MEM, LANES), lambda bi, p, i: (bi, 0, p))
    gain = pl.BlockSpec((1, LANES), lambda bi, p, i: (0, 0))
    rep = LANES // HEAD_DIM
    return pl.pallas_call(
        _mem_body, grid=(b, w // LANES, s // t),
        in_specs=[tile, mem, mem, tile, gain, gain],
        out_specs=tile, out_shape=jax.ShapeDtypeStruct((b, s, w), BF16),
        compiler_params=_cparams(("parallel", "parallel", "parallel")), name="mem_attention",
    )(q, km, vm, gate, jnp.tile(q_gain.astype(F32), rep).reshape(1, LANES),
      jnp.tile(k_gain.astype(F32), rep).reshape(1, LANES))


def _shift(cur, prev8, mu, first):
    rows = lax.broadcasted_iota(jnp.int32, cur.shape, 0)
    before = jnp.where(first, 0.0, prev8[7:8, :])
    prev = jnp.where(rows == 0, before, pltpu.roll(cur, 1, 0))
    return cur + (prev - cur) * mu


def _rwkv_prep_body(r_ref, k_ref, v_ref, lo_ref, rp_ref, kp_ref, vp_ref, lp_ref, mu_ref, mul_ref, w0_ref, w2_ref,
                    a0_ref, a2_ref, kk_ref, ka_ref, rk_ref,
                    ro_ref, lw_ref, ko_ref, vo_ref, kn_ref, ao_ref, bo_ref):
    first = pl.program_id(1) == 0
    mu = mu_ref[...]
    r = _shift(r_ref[0], rp_ref[0], mu[0:1], first)
    k = _shift(k_ref[0], kp_ref[0], mu[1:2], first)
    v = _shift(v_ref[0], vp_ref[0], mu[2:3], first)
    lora = _shift(lo_ref[0], lp_ref[0], mul_ref[...], first)
    z = w0_ref[...] + _mm_hi(jnp.tanh(lora), w2_ref[...])
    w_log = -(jnp.maximum(-z, 0.0) + jnp.log1p(jnp.exp(-jnp.abs(z)))) - 0.5
    a = jax.nn.sigmoid(a0_ref[...] + _mm_hi(lora, a2_ref[...]))
    kk = k * kk_ref[...]
    k2 = k * (1.0 + (a - 1.0) * ka_ref[...])
    ones = _head_mean_matrix(B_W) * float(HEAD_DIM)
    kk = kk * lax.rsqrt(jnp.maximum(_mm_sel_r(kk * kk, ones), 1e-24))
    ro_ref[0] = r
    lw_ref[0] = -jnp.exp(w_log)
    ko_ref[0] = k2
    vo_ref[0] = v
    kn_ref[0] = kk
    ao_ref[0] = a
    bo_ref[0] = _mm_sel_r(r * k2 * rk_ref[...], ones) * v


def rwkv_prep(r, k, v, lora, mu, w0, w2, a0, a2, k_k, k_a, r_k):
    b, s, w = r.shape
    t = ROW_TILE
    tile = pl.BlockSpec((1, t, w), lambda bi, i: (bi, i, 0))
    ltile = pl.BlockSpec((1, t, LANES), lambda bi, i: (bi, i, 0))
    prev_idx = lambda bi, i: (bi, jnp.maximum(i * (t // 8) - 1, 0), 0)
    ptile = pl.BlockSpec((1, 8, w), prev_idx)
    pltile = pl.BlockSpec((1, 8, LANES), prev_idx)
    const = lambda shape: pl.BlockSpec(shape, lambda bi, i: (0, 0))
    mu3 = jnp.stack([mu[:w], mu[w:2 * w], mu[2 * w:3 * w]]).astype(F32)
    mul = jnp.zeros((1, LANES), F32).at[0, :2 * B_LORA].set(mu[3 * w:])
    w2p = jnp.zeros((LANES, w), F32).at[:B_LORA].set(w2)
    a2p = jnp.zeros((LANES, w), F32).at[B_LORA:2 * B_LORA].set(a2)
    row = lambda p: p.reshape(1, w).astype(F32)
    return pl.pallas_call(
        _rwkv_prep_body, grid=(b, s // t),
        in_specs=[tile, tile, tile, ltile, ptile, ptile, ptile, pltile, const((3, w)), const((1, LANES)),
                  const((1, w)), const((LANES, w)), const((1, w)), const((LANES, w)), const((1, w)), const((1, w)),
                  const((1, w))],
        out_specs=[tile] * 7, out_shape=[jax.ShapeDtypeStruct((b, s, w), F32)] * 7,
        compiler_params=_cparams(("parallel", "parallel")), name="rwkv_prep",
    )(r, k, v, lora, r, k, v, lora, mu3, mul, row(w0), w2p, row(a0), a2p, row(k_k), row(k_a), row(r_k))


def _tri(n, strict):
    r = lax.broadcasted_iota(jnp.int32, (n, n), 0)
    c = lax.broadcasted_iota(jnp.int32, (n, n), 1)
    return (c < r) if strict else (c <= r)


def _block_diag_mask():
    r = lax.broadcasted_iota(jnp.int32, (LANES, LANES), 0) // HEAD_DIM
    c = lax.broadcasted_iota(jnp.int32, (LANES, LANES), 1) // HEAD_DIM
    return r == c


def _rwkv_scan_body(r_ref, lw_ref, k_ref, v_ref, kn_ref, a_ref, bo_ref, g_ref, lg_ref, lb_ref, o_ref, s_ref, *,
                    nbatch, npair):
    @pl.when(pl.program_id(0) == 0)
    def _():
        s_ref[...] = jnp.zeros_like(s_ref)

    c = CHUNK
    lo = _lane_lo()
    masks = (lo, jnp.logical_not(lo))
    strict = _tri(c, True)
    incl = _tri(c, False)
    ltri = incl.astype(F32)
    eye = (lax.broadcasted_iota(jnp.int32, (c, c), 0) == lax.broadcasted_iota(jnp.int32, (c, c), 1)).astype(F32)
    bd = _block_diag_mask()
    gmean = _head_mean_matrix(LANES)

    def chain(sl, bi, pi):
        cols = slice(pi * LANES, (pi + 1) * LANES)
        idx = bi * npair + pi
        r, lw, k, v, kn, a = (ref[bi, sl, cols] for ref in (r_ref, lw_ref, k_ref, v_ref, kn_ref, a_ref))
        cw = _mm_sel_l(ltri, lw)
        yield
        cl = cw[c - 1:c, :]
        at = -kn * jnp.exp(cw - lw)
        e_neg = jnp.exp(-cw)
        bt = kn * a * e_neg
        kt = k * e_neg
        rt = r * jnp.exp(cw)
        e_end = jnp.exp(cl - cw)
        s0 = s_ref[idx]
        rhs = _mm_nt(at, s0)
        ys0 = _mm_nt(rt, s0)
        yield
        ah = [jnp.where(mk, at, 0.0) for mk in masks]
        rh = [jnp.where(mk, rt, 0.0) for mk in masks]
        n = [jnp.where(strict, _mm_nt(x, bt), 0.0) for x in ah]
        yield
        aak = [jnp.where(strict, _mm_nt(x, kt), 0.0) for x in ah]
        yield
        arb = [jnp.where(incl, _mm_nt(x, bt), 0.0) for x in rh]
        yield
        ark = [jnp.where(incl, _mm_nt(x, kt), 0.0) for x in rh]
        yield
        xs = [rhs + _mm(x, v) for x in aak]
        yv = [_mm(x, v) for x in ark]
        yield
        tinv = [eye + x for x in n]
        p = n
        for _ in range(5):
            p = [_mm(x, x) for x in p]
            yield
            tinv = [x + _mm(x, y) for x, y in zip(tinv, p)]
            yield
        u = jnp.where(lo, _mm(tinv[0], xs[0]), _mm(tinv[1], xs[1]))
        yield
        y = ys0 + jnp.where(lo, _mm(arb[0], u) + yv[0], _mm(arb[1], u) + yv[1])
        upd = _mm_tn(u, kn * a * e_end) + _mm_tn(v, k * e_end)
        yield
        s_ref[idx] = s0 * jnp.exp(cl) + jnp.where(bd, upd, 0.0)
        mean = _mm_sel_r(y, gmean)
        yield
        d = y - mean
        var = _mm_sel_r(d * d, gmean)
        yield
        yn = d * lax.rsqrt(var + RWKV_GN_EPS) * lg_ref[:, cols] + lb_ref[:, cols] + bo_ref[bi, sl, cols]
        o_ref[bi, sl, cols] = (yn * _silu(g_ref[bi, sl, cols])).astype(o_ref.dtype)

    def chunk(ci, carry):
        sl = pl.ds(pl.multiple_of(ci * c, c), c)
        gens = [chain(sl, bi, pi) for bi in range(nbatch) for pi in range(npair)]
        for _ in itertools.zip_longest(*gens):
            pass
        return carry

    lax.fori_loop(0, SCAN_ROWS // c, chunk, 0)


def rwkv_scan(r, lw, k, v, kn, a, bonus, gate, lnx_g, lnx_b):
    b, s, w = r.shape
    t = SCAN_ROWS
    tile = pl.BlockSpec((b, t, w), lambda i: (0, i, 0))
    vec = pl.BlockSpec((1, w), lambda i: (0, 0))
    return pl.pallas_call(
        functools.partial(_rwkv_scan_body, nbatch=b, npair=w // LANES), grid=(s // t,),
        in_specs=[tile] * 8 + [vec, vec],
        out_specs=tile, out_shape=jax.ShapeDtypeStruct((b, s, w), BF16),
        scratch_shapes=[pltpu.VMEM((b * (w // LANES), LANES, LANES), F32)],
        compiler_params=_cparams(("arbitrary",)), name="rwkv_scan",
    )(r, lw, k, v, kn, a, bonus, gate, lnx_g.reshape(1, w).astype(F32), lnx_b.reshape(1, w).astype(F32))


def _hgrn_body(q_ref, f_ref, i_ref, g_ref, lb_ref, gn_ref, o_ref, s_ref, *, nbatch, npair):
    @pl.when(pl.program_id(0) == 0)
    def _():
        s_ref[...] = jnp.zeros_like(s_ref)

    c = CHUNK
    lo = _lane_lo()
    masks = (lo, jnp.logical_not(lo))
    ltri = _tri(c, False).astype(F32)
    bd = _block_diag_mask()
    gmean = _head_mean_matrix(LANES)
    head_ones = gmean * float(HEAD_DIM)
    rows = lax.broadcasted_iota(jnp.int32, (SUB, 1), 0)

    def chain(sl, bi, pi):
        cols = slice(pi * LANES, (pi + 1) * LANES)
        idx = bi * npair + pi
        lb = lb_ref[:, cols]
        log_lb = jnp.log(lb)
        log_1m = jnp.log1p(-lb)
        q, fr, v = q_ref[bi, sl, cols], f_ref[bi, sl, cols], i_ref[bi, sl, cols]
        log_sig = jnp.minimum(fr, 0.0) - jnp.log1p(jnp.exp(-jnp.abs(fr)))
        z = log_1m + log_sig
        hi = jnp.maximum(log_lb, z)
        log_f = hi + jnp.log1p(jnp.exp(-jnp.abs(log_lb - z)))
        k = (1.0 - lb) * jax.nn.sigmoid(-fr)
        bc = _mm_sel_l(ltri, log_f)
        yield
        bl = bc[c - 1:c, :]
        s0 = s_ref[idx]
        outs = []
        for sb in range(c // SUB):
            r0 = sb * SUB
            qs, bs, ks, vs = (x[r0:r0 + SUB] for x in (q, bc, k, v))
            o = _mm_nt(qs * jnp.exp(bs), s0)
            if sb > 0:
                ref = bc[r0 - 1:r0, :]
                qh = qs * jnp.exp(bs - ref)
                kh = k[:r0] * jnp.exp(ref - bc[:r0])
                vh = v[:r0]
                sc = [_mm_nt(jnp.where(mk, qh, 0.0), kh) for mk in masks]
                yield
                o = o + jnp.where(lo, _mm(sc[0], vh), _mm(sc[1], vh))
            pair = jnp.concatenate(
                [qs * ks[si:si + 1, :] * jnp.exp(jnp.where(rows >= si, bs - bs[si:si + 1, :], NEG_INF))
                 for si in range(SUB)], axis=0)
            score = _mm_sel_r(pair, head_ones)
            yield
            for si in range(SUB):
                o = o + score[si * SUB:(si + 1) * SUB] * vs[si:si + 1, :]
            outs.append(o)
        od = jnp.concatenate(outs, axis=0)
        s_ref[idx] = s0 * jnp.exp(bl) + jnp.where(bd, _mm_tn(v, k * jnp.exp(bl - bc)), 0.0)
        ms = _mm_sel_r(od * od, gmean)
        yield
        on = od * lax.rsqrt(ms + NORM_EPS) * gn_ref[...]
        o_ref[bi, sl, cols] = (on * _silu(g_ref[bi, sl, cols])).astype(o_ref.dtype)

    def chunk(ci, carry):
        sl = pl.ds(pl.multiple_of(ci * c, c), c)
        gens = [chain(sl, bi, pi) for bi in range(nbatch) for pi in range(npair)]
        for _ in itertools.zip_longest(*gens):
            pass
        return carry

    lax.fori_loop(0, SCAN_ROWS // c, chunk, 0)


def hgrn2(q, f, iv, gate, lb, gn_g):
    b, s, w = q.shape
    t = SCAN_ROWS
    tile = pl.BlockSpec((b, t, w), lambda i: (0, i, 0))
    rep = LANES // HEAD_DIM
    return pl.pallas_call(
        functools.partial(_hgrn_body, nbatch=b, npair=w // LANES), grid=(s // t,),
        in_specs=[tile] * 4 + [pl.BlockSpec((1, w), lambda i: (0, 0)), pl.BlockSpec((1, LANES), lambda i: (0, 0))],
        out_specs=tile, out_shape=jax.ShapeDtypeStruct((b, s, w), BF16),
        scratch_shapes=[pltpu.VMEM((b * (w // LANES), LANES, LANES), F32)],
        compiler_params=_cparams(("arbitrary",)), name="hgrn2",
    )(q, f, iv, gate, lb.reshape(1, w).astype(F32), jnp.tile(gn_g.astype(F32), rep).reshape(1, LANES))


def _memory_kv(memf, b, g, w_kv):
    wb = w_kv.astype(BF16)
    km, vm = rms_proj(memf, g, [wb[:, :M_W], wb[:, M_W:]], [("plain", F32), ("plain", BF16)])
    return km.reshape(b, N_MEM, M_W), vm.reshape(b, N_MEM, M_W)


def _even_layer(xf, b, s, km, vm, tables, ln_g, w_in, w_out, a_qn_g, a_kn_g, m_qn_g, m_kn_g,
                mu, w0, w2, a0, a2, k_k, k_a, r_k, lnx_g, lnx_b):
    wb = w_in.astype(BF16)
    edges = [0]
    for width in (A_W, A_W, A_W, A_W, B_W, B_W, B_W, 2 * B_LORA, B_W, M_W, M_W):
        edges.append(edges[-1] + width)
    ws = [wb[:, edges[n]:edges[n + 1]] for n in range(11)]
    ws[7] = jnp.pad(ws[7], ((0, 0), (0, LANES - 2 * B_LORA)))
    f32 = ("plain", F32)
    kinds = [("qk", a_qn_g, F32, 1.0, True, False), ("qk", a_kn_g, BF16, 1.0, False, True), ("vt",),
             f32, f32, f32, f32, f32, f32, f32, f32]
    q_t, k, kmean, v_t, ga, rr, rk, rv, lora, gb, qm, gm = rms_proj(xf, ln_g, ws, kinds, tables, s)
    sh = lambda t: t.reshape(b, s, t.shape[-1])
    oa = moba_attention(q_t, sh(k), v_t, kmean.reshape(b, s // MOBA_BLOCK, A_W), sh(ga))
    pre = rwkv_prep(sh(rr), sh(rk), sh(rv), sh(lora), mu, w0, w2, a0, a2, k_k, k_a, r_k.reshape(-1))
    ob = rwkv_scan(*pre, sh(gb), lnx_g, lnx_b)
    om = mem_attention(sh(qm), km, vm, sh(gm), m_qn_g, m_kn_g)
    wo = w_out.astype(BF16)
    fl = lambda t: t.reshape(b * s, t.shape[-1])
    return out_proj(xf, [fl(oa), fl(ob), fl(om)], [wo[:A_W], wo[A_W:A_W + B_W], wo[A_W + B_W:]])


def _odd_layer(xf, b, s, km, vm, tables, li, lb, ln_g, w_in, w_out, c_qn_g, c_kn_g, lqk, subln_g, d_gn_g,
               m_qn_g, m_kn_g):
    wb = w_in.astype(BF16)
    edges = [0]
    for width in (C_W, C_W, C_W, C_W, D_W, D_W, D_W, D_W, M_W, M_W):
        edges.append(edges[-1] + width)
    ws = [wb[:, edges[n]:edges[n + 1]] for n in range(10)]
    f32 = ("plain", F32)
    kinds = [("qk", c_qn_g, BF16, ATTN_SCALE * LOG2E, True, False), ("qk", c_kn_g, BF16, 1.0, False, False),
             ("vt",), f32, f32, f32, f32, f32, f32, f32]
    q_t, k, v_t, gc, qd, fd, idd, gd, qm, gm = rms_proj(xf, ln_g, ws, kinds, tables, s)
    sh = lambda t: t.reshape(b, s, t.shape[-1])
    lam_init = 0.8 - 0.6 * math.exp(-0.3 * li)
    oc = diff_attention(q_t, sh(k), v_t, sh(gc), lqk, subln_g, lam_init)
    od = hgrn2(sh(qd), sh(fd), sh(idd), sh(gd), lb, d_gn_g)
    om = mem_attention(sh(qm), km, vm, sh(gm), m_qn_g, m_kn_g)
    wo = w_out.astype(BF16)
    fl = lambda t: t.reshape(b * s, t.shape[-1])
    return out_proj(xf, [fl(oc), fl(od), fl(om)], [wo[:C_W], wo[C_W:C_W + D_W], wo[C_W + D_W:]])


def kernel(x, mem, ln_g, mem_ln_g, w_mem_kv, m_qn_g, m_kn_g, e_w_in, e_w_out, a_qn_g, a_kn_g, b_mu, b_w0, b_w2, b_a0, b_a2, b_k_k, b_k_a, b_r_k, b_lnx_g, b_lnx_b, o_w_in, o_w_out, c_qn_g, c_kn_g, c_lq1, c_lk1, c_lq2, c_lk2, c_subln_g, d_lb, d_gn_g):
    b, s, d = x.shape
    depth = ln_g.shape[0]
    tables = rope_tables_lanes(s)
    lbs = jax.nn.softmax(d_lb.astype(F32), axis=0)
    lbs = jnp.cumsum(lbs, axis=0) - lbs[0:1]
    xf = x.reshape(b * s, d)
    memf = mem.reshape(b * N_MEM, d)
    for li in range(depth):
        j = li // 2
        km, vm = _memory_kv(memf, b, mem_ln_g[li], w_mem_kv[li])
        if li % 2 == 0:
            xf = _even_layer(xf, b, s, km, vm, tables, ln_g[li], e_w_in[j], e_w_out[j], a_qn_g[j], a_kn_g[j],
                             m_qn_g[li], m_kn_g[li], b_mu[j], b_w0[j], b_w2[j], b_a0[j], b_a2[j], b_k_k[j],
                             b_k_a[j], b_r_k[j], b_lnx_g[j], b_lnx_b[j])
        else:
            lqk = jnp.stack([c_lq1[j], c_lk1[j], c_lq2[j], c_lk2[j]]).astype(F32)
            xf = _odd_layer(xf, b, s, km, vm, tables, li, jnp.maximum(lbs[j], 0.0), ln_g[li], o_w_in[j],
                            o_w_out[j], c_qn_g[j], c_kn_g[j], lqk, c_subln_g[j], d_gn_g[j], m_qn_g[li], m_kn_g[li])
    return xf.reshape(b, s, d)
```

```python
import functools
import itertools
import math

import jax
import jax.numpy as jnp
from jax import lax
from jax.experimental import pallas as pl
from jax.experimental.pallas import tpu as pltpu

F32 = jnp.float32
BF16 = jnp.bfloat16
HIGHEST = lax.Precision.HIGHEST

N_MEM = 256
HEAD_DIM = 64
ROPE_THETA = 500000.0
ROPE_DIM = HEAD_DIM // 4
NORM_EPS = 1e-6
NEG_INF = -1e30
A_HEADS = 6
MOBA_BLOCK = 256
MOBA_TOPK = 3
B_HEADS = 6
B_LORA = 32
RWKV_GN_EPS = 64e-5
C_HEADS = 4
D_HEADS = 4
M_HEADS = 4
A_W = A_HEADS * HEAD_DIM
B_W = B_HEADS * HEAD_DIM
C_W = C_HEADS * 2 * HEAD_DIM
D_W = D_HEADS * HEAD_DIM
M_W = M_HEADS * HEAD_DIM
ATTN_SCALE = HEAD_DIM ** -0.5

LANES = 128
VMEM_LIMIT = 48 * 1024 * 1024

ROW_TILE = 512
PREP_ROWS = 512
LOG2E = math.log2(math.e)
ONES_ROWS = 16
ATTN_TILE = 256
Q_TILE = 2 * ATTN_TILE
FLASH_UNROLL = 4
CHUNK = 64
SUB = 16
SCAN_ROWS = 256

_NT = (((1,), (1,)), ((), ()))
_TN = (((0,), (0,)), ((), ()))


def _cparams(sem):
    return pltpu.CompilerParams(dimension_semantics=sem, vmem_limit_bytes=VMEM_LIMIT)


def _mm(a, b):
    return jnp.dot(a.astype(BF16), b.astype(BF16), preferred_element_type=F32)


def _mm_nt(a, b):
    return lax.dot_general(a.astype(BF16), b.astype(BF16), _NT, preferred_element_type=F32)


def _mm_tn(a, b):
    return lax.dot_general(a.astype(BF16), b.astype(BF16), _TN, preferred_element_type=F32)


def _split(x, terms):
    parts = []
    for _ in range(terms - 1):
        hi = x.astype(BF16)
        parts.append(hi)
        x = x - hi.astype(F32)
    parts.append(x.astype(BF16))
    return parts[::-1]


def _mm_sel_r(x, sel, terms=3):
    sb = sel.astype(BF16)
    return sum(jnp.dot(part, sb, preferred_element_type=F32) for part in _split(x, terms))


def _mm_sel_l(sel, x, terms=3):
    sb = sel.astype(BF16)
    return sum(jnp.dot(sb, part, preferred_element_type=F32) for part in _split(x, terms))


def _mm_16bit(a, b):
    a_lo, a_hi = _split(a, 2)
    b_lo, b_hi = _split(b, 2)
    return (jnp.dot(a_lo, b_hi, preferred_element_type=F32) + jnp.dot(a_hi, b_lo, preferred_element_type=F32)
            + jnp.dot(a_hi, b_hi, preferred_element_type=F32))


def _silu(x):
    return x * jax.nn.sigmoid(x)


def _lane_lo(width=LANES):
    lane = lax.broadcasted_iota(jnp.int32, (1, width), 1)
    return (lane % LANES) < HEAD_DIM


def _head_mean_matrix(width):
    r = lax.broadcasted_iota(jnp.int32, (width, width), 0) // HEAD_DIM
    c = lax.broadcasted_iota(jnp.int32, (width, width), 1) // HEAD_DIM
    return jnp.where(r == c, 1.0 / HEAD_DIM, 0.0).astype(F32)


def _proj_body(*refs, kinds, has_rope):
    it = iter(refs)
    x_ref, g_ref = next(it), next(it)
    if has_rope:
        cos, sin1, sin2 = next(it)[...], next(it)[...], next(it)[...]
        gm = _head_mean_matrix(LANES)
    w_refs = [next(it) for _ in kinds]
    gain_refs = [next(it) if kind[0] == "qk" else None for kind in kinds]
    x = x_ref[...]
    tm = x.shape[0]
    ms = jnp.mean(x * x, axis=-1, keepdims=True)
    h = (x * lax.rsqrt(ms + NORM_EPS) * g_ref[...]).astype(BF16)
    for kind, w_ref, gain_ref in zip(kinds, w_refs, gain_refs):
        acc = jnp.dot(h, w_ref[...], preferred_element_type=F32)
        n_blocks = acc.shape[1] // LANES
        o_ref = next(it)
        if kind[0] == "plain":
            o_ref[...] = acc.astype(o_ref.dtype)
        elif kind[0] == "vt":
            for kt in range(tm // ATTN_TILE):
                for c in range(n_blocks):
                    tile = acc[kt * ATTN_TILE:(kt + 1) * ATTN_TILE, c * LANES:(c + 1) * LANES]
                    o_ref[kt, c] = tile.T.astype(o_ref.dtype)
        else:
            _, _, _, scale, transposed, block_mean = kind
            mean_ref = next(it) if block_mean else None
            for c in range(n_blocks):
                cols = slice(c * LANES, (c + 1) * LANES)
                y = acc[:, cols]
                y = y * lax.rsqrt(_mm_sel_r(y * y, gm, 2) + NORM_EPS) * gain_ref[...]
                yr = y * cos + pltpu.roll(y, LANES - ROPE_DIM // 2, 1) * sin1 + pltpu.roll(y, ROPE_DIM // 2, 1) * sin2
                if transposed:
                    o_ref[0, cols, :] = (yr * scale).T.astype(o_ref.dtype)
                else:
                    o_ref[:, cols] = (yr * scale).astype(o_ref.dtype)
                if block_mean:
                    for rb in range(tm // MOBA_BLOCK):
                        blk = yr[rb * MOBA_BLOCK:(rb + 1) * MOBA_BLOCK]
                        mean_ref[rb, :, cols] = jnp.mean(blk, axis=0, keepdims=True)


def rms_proj(x2d, g, ws, kinds, tables=None, seq=None):
    n, d = x2d.shape
    tm = min(ROW_TILE, n)
    has_rope = tables is not None
    nt = seq // tm if has_rope else 1
    row = lambda i: (i, 0)
    const = lambda i: (0, 0)
    in_specs = [pl.BlockSpec((tm, d), row), pl.BlockSpec((1, d), const)]
    args = [x2d, g.reshape(1, d).astype(F32)]
    if has_rope:
        in_specs += [pl.BlockSpec((tm, LANES), lambda i: (i % nt, 0))] * 3
        args += list(tables)
    in_specs += [pl.BlockSpec(w.shape, const) for w in ws]
    args += list(ws)
    out_specs, out_shape = [], []
    for w, kind in zip(ws, kinds):
        nw = w.shape[1]
        if kind[0] == "plain":
            out_specs.append(pl.BlockSpec((tm, nw), row))
            out_shape.append(jax.ShapeDtypeStruct((n, nw), kind[1]))
        elif kind[0] == "vt":
            blk = (tm // ATTN_TILE, nw // LANES, LANES, ATTN_TILE)
            out_specs.append(pl.BlockSpec(blk, lambda i: (i, 0, 0, 0)))
            out_shape.append(jax.ShapeDtypeStruct((n // ATTN_TILE,) + blk[1:], BF16))
        else:
            _, gain, dtype, _, transposed, block_mean = kind
            in_specs.append(pl.BlockSpec((1, LANES), const))
            args.append(jnp.tile(gain.astype(F32), LANES // HEAD_DIM).reshape(1, LANES))
            if transposed:
                out_specs.append(pl.BlockSpec((1, nw, tm), lambda i: (i // nt, 0, i % nt)))
                out_shape.append(jax.ShapeDtypeStruct((n // seq, nw, seq), dtype))
            else:
                out_specs.append(pl.BlockSpec((tm, nw), row))
                out_shape.append(jax.ShapeDtypeStruct((n, nw), dtype))
            if block_mean:
                out_specs.append(pl.BlockSpec((tm // MOBA_BLOCK, 1, nw), lambda i: (i, 0, 0)))
                out_shape.append(jax.ShapeDtypeStruct((n // MOBA_BLOCK, 1, nw), F32))
    static_kinds = tuple(k if k[0] != "qk" else (k[0], None) + tuple(k[2:]) for k in kinds)
    return pl.pallas_call(
        functools.partial(_proj_body, kinds=static_kinds, has_rope=has_rope),
        grid=(n // tm,), in_specs=in_specs, out_specs=out_specs, out_shape=out_shape,
        compiler_params=_cparams(("parallel",)), name="rms_proj",
    )(*args)


def _out_proj_body(x_ref, *refs, n_in):
    o_ref = refs[-1]
    acc = x_ref[...]
    for m_ref, w_ref in zip(refs[:n_in], refs[n_in:2 * n_in]):
        acc = acc + jnp.dot(m_ref[...], w_ref[...], preferred_element_type=F32)
    o_ref[...] = acc


def out_proj(x2d, parts, ws):
    n, d = x2d.shape
    tm = min(ROW_TILE, n)
    in_specs = [pl.BlockSpec((tm, d), lambda i: (i, 0))]
    in_specs += [pl.BlockSpec((tm, p.shape[1]), lambda i: (i, 0)) for p in parts]
    in_specs += [pl.BlockSpec(w.shape, lambda i: (0, 0)) for w in ws]
    return pl.pallas_call(
        functools.partial(_out_proj_body, n_in=len(parts)),
        grid=(n // tm,), in_specs=in_specs, out_specs=pl.BlockSpec((tm, d), lambda i: (i, 0)),
        out_shape=jax.ShapeDtypeStruct((n, d), F32),
        compiler_params=_cparams(("parallel",)), name="out_proj",
    )(x2d, *parts, *ws)


def rope_tables_lanes(seq):
    pos = jnp.arange(seq, dtype=F32)
    inv = 1.0 / (ROPE_THETA ** (jnp.arange(0, ROPE_DIM, 2, dtype=F32) / ROPE_DIM))
    ang = pos[:, None] * inv[None, :]
    cos, sin = jnp.cos(ang), jnp.sin(ang)
    half = ROPE_DIM // 2
    ones = jnp.ones((seq, HEAD_DIM - ROPE_DIM), F32)
    zeros_h = jnp.zeros((seq, half), F32)
    zeros_r = jnp.zeros((seq, HEAD_DIM - ROPE_DIM), F32)
    c = jnp.concatenate([cos, cos, ones], axis=1)
    s1 = jnp.concatenate([-sin, zeros_h, zeros_r], axis=1)
    s2 = jnp.concatenate([zeros_h, sin, zeros_r], axis=1)
    rep = LANES // HEAD_DIM
    return jnp.tile(c, (1, rep)), jnp.tile(s1, (1, rep)), jnp.tile(s2, (1, rep))


def _flash_t(i, k_ref, vt_ref, qm_ref, bufs, m_ref, acc_ref, v_rows, bias_row):
    t = ATTN_TILE
    (s0, s1), (p0, p1), (a0, a1) = bufs
    n_real = 2 * i + 1
    last = jnp.maximum(2 * i - 1, 0)
    ncol = Q_TILE // LANES

    def key_tile(x):
        own = jnp.where(x == 0, 2 * i + 1, 2 * i)
        return jnp.where(x < 2, own, jnp.minimum(x - 2, last))

    def a_part(x, h, s_w):
        kb = jnp.minimum(x - 2, last)
        k_blk = k_ref[0, pl.ds(pl.multiple_of(kb * t, t), t), :]
        s = jnp.dot(k_blk, qm_ref[h], preferred_element_type=F32)
        if bias_row is not None:
            s = s + bias_row(h, kb)
        s_w[h] = s

    def b_part(x, h, c, s_r, p_w, a_w):
        real = x <= n_real
        cs = slice(c * LANES, (c + 1) * LANES)
        s = s_r[h, :, cs]
        m_old = m_ref[h, :, cs]
        m_top = jnp.maximum(m_old, jnp.max(s, axis=0, keepdims=True))
        m_new = jnp.where(real, m_top, m_old)
        m_ref[h, :, cs] = m_new
        a_w[h, :, cs] = jnp.exp2(m_old - m_new)
        p_w[h, :, cs] = jnp.exp2(s - m_top).astype(BF16)

    def c_part(x, h, p_r, a_r):
        v_blk = jnp.concatenate([vt_ref[key_tile(x), 0, v_rows(h), :], jnp.ones((ONES_ROWS, t), BF16)], axis=0)
        v_blk = jnp.where(x <= n_real, v_blk, jnp.zeros_like(v_blk))
        acc_ref[h] = a_r[h] * acc_ref[h] + jnp.dot(v_blk, p_r[h], preferred_element_type=F32)

    def stage_a(x, s_w):
        for h in range(2):
            a_part(x, h, s_w)

    def stage_b(x, s_r, p_w, a_w):
        for h in range(2):
            for c in range(ncol):
                b_part(x, h, c, s_r, p_w, a_w)

    def stage_c(x, p_r, a_r):
        for h in range(2):
            c_part(x, h, p_r, a_r)

    m_ref[...] = jnp.full(m_ref.shape, NEG_INF, F32)
    acc_ref[...] = jnp.zeros(acc_ref.shape, F32)
    kpos = lax.broadcasted_iota(jnp.int32, (t, Q_TILE), 0)
    qpos = lax.broadcasted_iota(jnp.int32, (t, Q_TILE), 1)
    late = qpos >= t
    late_row = lax.broadcasted_iota(jnp.int32, (1, Q_TILE), 1) >= t
    allow_hi = jnp.logical_and(late, kpos <= qpos - t)
    allow_lo = jnp.logical_or(late, kpos <= qpos)
    k_hi = k_ref[0, pl.ds(pl.multiple_of((2 * i + 1) * t, t), t), :]
    k_lo = k_ref[0, pl.ds(pl.multiple_of(2 * i * t, t), t), :]
    for h in range(2):
        s0[h] = jnp.where(allow_hi, jnp.dot(k_hi, qm_ref[h], preferred_element_type=F32), NEG_INF)
        s_lo = jnp.dot(k_lo, qm_ref[h], preferred_element_type=F32)
        if bias_row is not None:
            s_lo = s_lo + jnp.where(late_row, bias_row(h, 2 * i), 0.0)
        s1[h] = jnp.where(allow_lo, s_lo, NEG_INF)
    bufs_of = ((s0, p0, a0), (s1, p1, a1))

    def step(x, k):
        s_w, p_r, a_r = bufs_of[k % 2]
        s_r, p_w, a_w = bufs_of[(k + 1) % 2]
        stage_c(x - 2, p_r, a_r)
        stage_b(x - 1, s_r, p_w, a_w)
        stage_a(x, s_w)

    stage_b(0, s0, p0, a0)
    u_steps = FLASH_UNROLL

    def unrolled(u, carry):
        for k in range(u_steps):
            step(u_steps * u + 2 + k, k)
        return carry

    n_steps = 2 * i + 2
    groups = n_steps // u_steps
    lax.fori_loop(0, groups, unrolled, 0)

    @pl.when(n_steps % u_steps != 0)
    def _():
        for k in range(2):
            step(u_steps * groups + 2 + k, k)


def _flash_scratch(v_rows):
    t, tq = ATTN_TILE, Q_TILE
    return [pltpu.VMEM((2, LANES, tq), BF16),
            pltpu.VMEM((2, t, tq), F32), pltpu.VMEM((2, t, tq), F32),
            pltpu.VMEM((2, t, tq), BF16), pltpu.VMEM((2, t, tq), BF16),
            pltpu.VMEM((2, 1, tq), F32), pltpu.VMEM((2, 1, tq), F32),
            pltpu.VMEM((2, 1, tq), F32), pltpu.VMEM((2, v_rows + ONES_ROWS, tq), F32)]


def _moba_body(qt_ref, k_ref, vt_ref, km_ref, g_ref, o_ref, bias_ref, qm_ref, s0, s1, p0, p1, a0, a1,
               m_ref, acc_ref, *, nb):
    i = pl.program_id(2)
    t = Q_TILE
    bufs = ((s0, s1), (p0, p1), (a0, a1))
    q_t = qt_ref[0]
    dim_lo = lax.broadcasted_iota(jnp.int32, (LANES, 1), 0) < HEAD_DIM
    lo = _lane_lo()
    km = km_ref[0]
    blk = lax.broadcasted_iota(jnp.int32, (nb, t), 0).astype(F32)
    own = (2 * i + (lax.broadcasted_iota(jnp.int32, (1, t), 1) >= MOBA_BLOCK).astype(jnp.int32)).astype(F32)
    for h in range(2):
        qh = jnp.where(dim_lo if h == 0 else jnp.logical_not(dim_lo), q_t, 0.0)
        qm_ref[h] = (qh * (ATTN_SCALE * LOG2E)).astype(BF16)
        kmh = jnp.where(lo if h == 0 else jnp.logical_not(lo), km, 0.0)
        gate = jnp.dot(kmh, q_t, precision=HIGHEST, preferred_element_type=F32)
        gate = jnp.where(blk < own, gate, NEG_INF)
        sel = jnp.zeros((nb, t), F32)
        for _ in range(MOBA_TOPK):
            mx = jnp.max(gate, axis=0, keepdims=True)
            first = jnp.min(jnp.where(gate == mx, blk, float(nb)), axis=0, keepdims=True)
            hit = blk == first
            sel = jnp.where(jnp.logical_and(hit, first < own), 1.0, sel)
            gate = jnp.where(hit, -jnp.inf, gate)
        bias_ref[h] = jnp.where(sel > 0.0, 0.0, NEG_INF)

    _flash_t(i, k_ref, vt_ref, qm_ref, bufs, m_ref, acc_ref,
             v_rows=lambda h: slice(h * HEAD_DIM, (h + 1) * HEAD_DIM),
             bias_row=lambda h, kb: bias_ref[h, pl.ds(kb, 1), :])
    o_t = jnp.concatenate([acc_ref[h, :HEAD_DIM, :] / acc_ref[h, HEAD_DIM:HEAD_DIM + 1, :] for h in range(2)],
                          axis=0)
    o_ref[0] = (o_t.T * _silu(g_ref[0])).astype(o_ref.dtype)


def moba_attention(q_t, k, v_t, kmean, gate):
    b, w, s = q_t.shape
    t, tq = ATTN_TILE, Q_TILE
    nb = s // t
    tile = pl.BlockSpec((1, tq, LANES), lambda bi, p, i: (bi, i, p))
    return pl.pallas_call(
        functools.partial(_moba_body, nb=nb), grid=(b, w // LANES, s // tq),
        in_specs=[pl.BlockSpec((1, LANES, tq), lambda bi, p, i: (bi, p, i)),
                  pl.BlockSpec((1, s, LANES), lambda bi, p, i: (bi, 0, p)),
                  pl.BlockSpec((nb, 1, LANES, t), lambda bi, p, i: (bi, p, 0, 0)),
                  pl.BlockSpec((1, nb, LANES), lambda bi, p, i: (bi, 0, p)), tile],
        out_specs=tile, out_shape=jax.ShapeDtypeStruct((b, s, w), BF16),
        scratch_shapes=[pltpu.VMEM((2, nb, tq), F32)] + _flash_scratch(HEAD_DIM),
        compiler_params=_cparams(("parallel", "parallel", "arbitrary")), name="moba_attention",
    )(q_t, k, v_t, kmean, gate)


def _diff_body(qt_ref, k_ref, vt_ref, g_ref, lqk_ref, sg_ref, o_ref, qm_ref, s0, s1, p0, p1, a0, a1,
               m_ref, acc_ref, *, lam_init):
    i = pl.program_id(2)
    q_t = qt_ref[0]
    dim_lo = lax.broadcasted_iota(jnp.int32, (LANES, 1), 0) < HEAD_DIM
    for h in range(2):
        qm_ref[h] = jnp.where(dim_lo if h == 0 else jnp.logical_not(dim_lo), q_t, 0)
    _flash_t(i, k_ref, vt_ref, qm_ref, ((s0, s1), (p0, p1), (a0, a1)), m_ref, acc_ref,
             v_rows=lambda h: slice(0, LANES), bias_row=None)
    lqk = lqk_ref[...]
    lam = (jnp.exp(jnp.sum(lqk[0:1] * lqk[1:2], axis=1, keepdims=True))
           - jnp.exp(jnp.sum(lqk[2:3] * lqk[3:4], axis=1, keepdims=True)) + lam_init)
    att = [acc_ref[h, :LANES, :] / acc_ref[h, LANES:LANES + 1, :] for h in range(2)]
    o = (att[0] - lam * att[1]).T
    ms = jnp.mean(o * o, axis=1, keepdims=True)
    o = o * lax.rsqrt(ms + NORM_EPS) * sg_ref[...] * (1.0 - lam_init)
    o_ref[0] = (o * _silu(g_ref[0])).astype(o_ref.dtype)


def diff_attention(q_t, k, v_t, gate, lqk, subln_g, lam_init):
    b, w, s = q_t.shape
    t, tq = ATTN_TILE, Q_TILE
    nb = s // t
    tile = pl.BlockSpec((1, tq, LANES), lambda bi, h, i: (bi, i, h))
    return pl.pallas_call(
        functools.partial(_diff_body, lam_init=lam_init), grid=(b, w // LANES, s // tq),
        in_specs=[pl.BlockSpec((1, LANES, tq), lambda bi, h, i: (bi, h, i)),
                  pl.BlockSpec((1, s, LANES), lambda bi, h, i: (bi, 0, h)),
                  pl.BlockSpec((nb, 1, LANES, t), lambda bi, h, i: (bi, h, 0, 0)),
                  tile, pl.BlockSpec((4, HEAD_DIM), lambda bi, h, i: (0, 0)),
                  pl.BlockSpec((1, LANES), lambda bi, h, i: (0, 0))],
        out_specs=tile, out_shape=jax.ShapeDtypeStruct((b, s, w), BF16),
        scratch_shapes=_flash_scratch(LANES),
        compiler_params=_cparams(("parallel", "parallel", "arbitrary")), name="diff_attention",
    )(q_t, k, v_t, gate, lqk, subln_g.reshape(1, LANES).astype(F32))


def _mem_body(q_ref, km_ref, vm_ref, g_ref, qg_ref, kg_ref, o_ref):
    gm = _head_mean_matrix(LANES)
    lo = _lane_lo()
    masks = (lo, jnp.logical_not(lo))
    q = q_ref[0]
    q = q * lax.rsqrt(_mm_sel_r(q * q, gm, 2) + NORM_EPS) * qg_ref[...]
    k = km_ref[0]
    k = (k * lax.rsqrt(_mm_sel_r(k * k, gm, 2) + NORM_EPS) * kg_ref[...]).astype(BF16)
    v = vm_ref[0]
    outs = []
    for h in range(2):
        qh = (jnp.where(masks[h], q, 0.0) * ATTN_SCALE).astype(BF16)
        s = _mm_nt(qh, k)
        p = jnp.exp(s - jnp.max(s, axis=1, keepdims=True))
        outs.append(_mm(p, v) / jnp.sum(p, axis=1, keepdims=True))
    out = jnp.where(lo, outs[0], outs[1])
    o_ref[0] = (out * _silu(g_ref[0])).astype(o_ref.dtype)


def mem_attention(q, km, vm, gate, q_gain, k_gain):
    b, s, w = q.shape
    t = PREP_ROWS
    tile = pl.BlockSpec((1, t, LANES), lambda bi, p, i: (bi, i, p))
    mem = pl.BlockSpec((1, N_MEM, LANES), lambda bi, p, i: (bi, 0, p))
    gain = pl.BlockSpec((1, LANES), lambda bi, p, i: (0, 0))
    rep = LANES // HEAD_DIM
    return pl.pallas_call(
        _mem_body, grid=(b, w // LANES, s // t),
        in_specs=[tile, mem, mem, tile, gain, gain],
        out_specs=tile, out_shape=jax.ShapeDtypeStruct((b, s, w), BF16),
        compiler_params=_cparams(("parallel", "parallel", "parallel")), name="mem_attention",
    )(q, km, vm, gate, jnp.tile(q_gain.astype(F32), rep).reshape(1, LANES),
      jnp.tile(k_gain.astype(F32), rep).reshape(1, LANES))


def _shift(cur, prev8, mu, first):
    rows = lax.broadcasted_iota(jnp.int32, cur.shape, 0)
    before = jnp.where(first, 0.0, prev8[7:8, :])
    prev = jnp.where(rows == 0, before, pltpu.roll(cur, 1, 0))
    return cur + (prev - cur) * mu


def _rwkv_prep_body(r_ref, k_ref, v_ref, lo_ref, rp_ref, kp_ref, vp_ref, lp_ref, mu_ref, mul_ref, w0_ref, w2_ref,
                    a0_ref, a2_ref, kk_ref, ka_ref, rk_ref,
                    ro_ref, lw_ref, ko_ref, vo_ref, kn_ref, ao_ref, bo_ref):
    first = pl.program_id(1) == 0
    mu = mu_ref[...]
    r = _shift(r_ref[0], rp_ref[0], mu[0:1], first)
    k = _shift(k_ref[0], kp_ref[0], mu[1:2], first)
    v = _shift(v_ref[0], vp_ref[0], mu[2:3], first)
    lora = _shift(lo_ref[0], lp_ref[0], mul_ref[...], first)
    z = w0_ref[...] + _mm_16bit(jnp.tanh(lora), w2_ref[...])
    w_log = -(jnp.maximum(-z, 0.0) + jnp.log1p(jnp.exp(-jnp.abs(z)))) - 0.5
    a = jax.nn.sigmoid(a0_ref[...] + _mm_16bit(lora, a2_ref[...]))
    kk = k * kk_ref[...]
    k2 = k * (1.0 + (a - 1.0) * ka_ref[...])
    ones = _head_mean_matrix(B_W) * float(HEAD_DIM)
    kk = kk * lax.rsqrt(jnp.maximum(_mm_sel_r(kk * kk, ones, 2), 1e-24))
    ro_ref[0] = r
    lw_ref[0] = -jnp.exp(w_log)
    ko_ref[0] = k2
    vo_ref[0] = v
    kn_ref[0] = kk
    ao_ref[0] = a
    bo_ref[0] = _mm_sel_r(r * k2 * rk_ref[...], ones, 2) * v


def rwkv_prep(r, k, v, lora, mu, w0, w2, a0, a2, k_k, k_a, r_k):
    b, s, w = r.shape
    t = ROW_TILE
    tile = pl.BlockSpec((1, t, w), lambda bi, i: (bi, i, 0))
    ltile = pl.BlockSpec((1, t, LANES), lambda bi, i: (bi, i, 0))
    prev_idx = lambda bi, i: (bi, jnp.maximum(i * (t // 8) - 1, 0), 0)
    ptile = pl.BlockSpec((1, 8, w), prev_idx)
    pltile = pl.BlockSpec((1, 8, LANES), prev_idx)
    const = lambda shape: pl.BlockSpec(shape, lambda bi, i: (0, 0))
    mu3 = jnp.stack([mu[:w], mu[w:2 * w], mu[2 * w:3 * w]]).astype(F32)
    mul = jnp.zeros((1, LANES), F32).at[0, :2 * B_LORA].set(mu[3 * w:])
    w2p = jnp.zeros((LANES, w), F32).at[:B_LORA].set(w2)
    a2p = jnp.zeros((LANES, w), F32).at[B_LORA:2 * B_LORA].set(a2)
    row = lambda p: p.reshape(1, w).astype(F32)
    return pl.pallas_call(
        _rwkv_prep_body, grid=(b, s // t),
        in_specs=[tile, tile, tile, ltile, ptile, ptile, ptile, pltile, const((3, w)), const((1, LANES)),
                  const((1, w)), const((LANES, w)), const((1, w)), const((LANES, w)), const((1, w)), const((1, w)),
                  const((1, w))],
        out_specs=[tile] * 7, out_shape=[jax.ShapeDtypeStruct((b, s, w), F32)] * 7,
        compiler_params=_cparams(("parallel", "parallel")), name="rwkv_prep",
    )(r, k, v, lora, r, k, v, lora, mu3, mul, row(w0), w2p, row(a0), a2p, row(k_k), row(k_a), row(r_k))


def _tri(n, strict):
    r = lax.broadcasted_iota(jnp.int32, (n, n), 0)
    c = lax.broadcasted_iota(jnp.int32, (n, n), 1)
    return (c < r) if strict else (c <= r)


def _block_diag_mask():
    r = lax.broadcasted_iota(jnp.int32, (LANES, LANES), 0) // HEAD_DIM
    c = lax.broadcasted_iota(jnp.int32, (LANES, LANES), 1) // HEAD_DIM
    return r == c


def _rwkv_scan_body(r_ref, lw_ref, k_ref, v_ref, kn_ref, a_ref, bo_ref, g_ref, lg_ref, lb_ref, o_ref, s_ref, *,
                    nbatch, npair):
    @pl.when(pl.program_id(0) == 0)
    def _():
        s_ref[...] = jnp.zeros_like(s_ref)

    c = CHUNK
    lo = _lane_lo()
    masks = (lo, jnp.logical_not(lo))
    strict = _tri(c, True)
    incl = _tri(c, False)
    ltri = incl.astype(F32)
    eye = (lax.broadcasted_iota(jnp.int32, (c, c), 0) == lax.broadcasted_iota(jnp.int32, (c, c), 1)).astype(F32)
    bd = _block_diag_mask()
    gmean = _head_mean_matrix(LANES)

    def chain(sl, bi, pi):
        cols = slice(pi * LANES, (pi + 1) * LANES)
        idx = bi * npair + pi
        r, lw, k, v, kn, a = (ref[bi, sl, cols] for ref in (r_ref, lw_ref, k_ref, v_ref, kn_ref, a_ref))
        cw = _mm_sel_l(ltri, lw)
        yield
        cl = cw[c - 1:c, :]
        at = -kn * jnp.exp(cw - lw)
        e_neg = jnp.exp(-cw)
        bt = kn * a * e_neg
        kt = k * e_neg
        rt = r * jnp.exp(cw)
        e_end = jnp.exp(cl - cw)
        s0 = s_ref[idx]
        rhs = _mm_nt(at, s0)
        ys0 = _mm_nt(rt, s0)
        yield
        ah = [jnp.where(mk, at, 0.0) for mk in masks]
        rh = [jnp.where(mk, rt, 0.0) for mk in masks]
        n = [jnp.where(strict, _mm_nt(x, bt), 0.0) for x in ah]
        yield
        aak = [jnp.where(strict, _mm_nt(x, kt), 0.0) for x in ah]
        yield
        arb = [jnp.where(incl, _mm_nt(x, bt), 0.0) for x in rh]
        yield
        ark = [jnp.where(incl, _mm_nt(x, kt), 0.0) for x in rh]
        yield
        xs = [rhs + _mm(x, v) for x in aak]
        yv = [_mm(x, v) for x in ark]
        yield
        tinv = [eye + x for x in n]
        p = n
        for _ in range(5):
            p = [_mm(x, x) for x in p]
            yield
            tinv = [x + _mm(x, y) for x, y in zip(tinv, p)]
            yield
        u = jnp.where(lo, _mm(tinv[0], xs[0]), _mm(tinv[1], xs[1]))
        yield
        y = ys0 + jnp.where(lo, _mm(arb[0], u) + yv[0], _mm(arb[1], u) + yv[1])
        upd = _mm_tn(u, kn * a * e_end) + _mm_tn(v, k * e_end)
        yield
        s_ref[idx] = s0 * jnp.exp(cl) + jnp.where(bd, upd, 0.0)
        mean = _mm_sel_r(y, gmean, 2)
        yield
        d = y - mean
        var = _mm_sel_r(d * d, gmean, 2)
        yield
        yn = d * lax.rsqrt(var + RWKV_GN_EPS) * lg_ref[:, cols] + lb_ref[:, cols] + bo_ref[bi, sl, cols]
        o_ref[bi, sl, cols] = (yn * _silu(g_ref[bi, sl, cols])).astype(o_ref.dtype)

    def chunk(ci, carry):
        sl = pl.ds(pl.multiple_of(ci * c, c), c)
        gens = [chain(sl, bi, pi) for bi in range(nbatch) for pi in range(npair)]
        for _ in itertools.zip_longest(*gens):
            pass
        return carry

    lax.fori_loop(0, SCAN_ROWS // c, chunk, 0)


def rwkv_scan(r, lw, k, v, kn, a, bonus, gate, lnx_g, lnx_b):
    b, s, w = r.shape
    t = SCAN_ROWS
    tile = pl.BlockSpec((b, t, w), lambda i: (0, i, 0))
    vec = pl.BlockSpec((1, w), lambda i: (0, 0))
    return pl.pallas_call(
        functools.partial(_rwkv_scan_body, nbatch=b, npair=w // LANES), grid=(s // t,),
        in_specs=[tile] * 8 + [vec, vec],
        out_specs=tile, out_shape=jax.ShapeDtypeStruct((b, s, w), BF16),
        scratch_shapes=[pltpu.VMEM((b * (w // LANES), LANES, LANES), F32)],
        compiler_params=_cparams(("arbitrary",)), name="rwkv_scan",
    )(r, lw, k, v, kn, a, bonus, gate, lnx_g.reshape(1, w).astype(F32), lnx_b.reshape(1, w).astype(F32))


def _hgrn_body(q_ref, f_ref, i_ref, g_ref, lb_ref, gn_ref, o_ref, s_ref, *, nbatch, npair):
    @pl.when(pl.program_id(0) == 0)
    def _():
        s_ref[...] = jnp.zeros_like(s_ref)

    c = CHUNK
    lo = _lane_lo()
    masks = (lo, jnp.logical_not(lo))
    ltri = _tri(c, False).astype(F32)
    bd = _block_diag_mask()
    gmean = _head_mean_matrix(LANES)
    head_ones = gmean * float(HEAD_DIM)
    rows = lax.broadcasted_iota(jnp.int32, (SUB, 1), 0)

    def chain(sl, bi, pi):
        cols = slice(pi * LANES, (pi + 1) * LANES)
        idx = bi * npair + pi
        lb = lb_ref[:, cols]
        log_lb = jnp.log(lb)
        log_1m = jnp.log1p(-lb)
        q, fr, v = q_ref[bi, sl, cols], f_ref[bi, sl, cols], i_ref[bi, sl, cols]
        log_sig = jnp.minimum(fr, 0.0) - jnp.log1p(jnp.exp(-jnp.abs(fr)))
        z = log_1m + log_sig
        hi = jnp.maximum(log_lb, z)
        log_f = hi + jnp.log1p(jnp.exp(-jnp.abs(log_lb - z)))
        k = (1.0 - lb) * jax.nn.sigmoid(-fr)
        bc = _mm_sel_l(ltri, log_f)
        yield
        bl = bc[c - 1:c, :]
        s0 = s_ref[idx]
        outs = []
        for sb in range(c // SUB):
            r0 = sb * SUB
            qs, bs, ks, vs = (x[r0:r0 + SUB] for x in (q, bc, k, v))
            o = _mm_nt(qs * jnp.exp(bs), s0)
            if sb > 0:
                ref = bc[r0 - 1:r0, :]
                qh = qs * jnp.exp(bs - ref)
                kh = k[:r0] * jnp.exp(ref - bc[:r0])
                vh = v[:r0]
                sc = [_mm_nt(jnp.where(mk, qh, 0.0), kh) for mk in masks]
                yield
                o = o + jnp.where(lo, _mm(sc[0], vh), _mm(sc[1], vh))
            pair = jnp.concatenate(
                [qs * ks[si:si + 1, :] * jnp.exp(jnp.where(rows >= si, bs - bs[si:si + 1, :], NEG_INF))
                 for si in range(SUB)], axis=0)
            score = _mm_sel_r(pair, head_ones, 2)
            yield
            for si in range(SUB):
                o = o + score[si * SUB:(si + 1) * SUB] * vs[si:si + 1, :]
            outs.append(o)
        od = jnp.concatenate(outs, axis=0)
        s_ref[idx] = s0 * jnp.exp(bl) + jnp.where(bd, _mm_tn(v, k * jnp.exp(bl - bc)), 0.0)
        ms = _mm_sel_r(od * od, gmean, 2)
        yield
        on = od * lax.rsqrt(ms + NORM_EPS) * gn_ref[...]
        o_ref[bi, sl, cols] = (on * _silu(g_ref[bi, sl, cols])).astype(o_ref.dtype)

    def chunk(ci, carry):
        sl = pl.ds(pl.multiple_of(ci * c, c), c)
        gens = [chain(sl, bi, pi) for bi in range(nbatch) for pi in range(npair)]
        for _ in itertools.zip_longest(*gens):
            pass
        return carry

    lax.fori_loop(0, SCAN_ROWS // c, chunk, 0)


def hgrn2(q, f, iv, gate, lb, gn_g):
    b, s, w = q.shape
    t = SCAN_ROWS
    tile = pl.BlockSpec((b, t, w), lambda i: (0, i, 0))
    rep = LANES // HEAD_DIM
    return pl.pallas_call(
        functools.partial(_hgrn_body, nbatch=b, npair=w // LANES), grid=(s // t,),
        in_specs=[tile] * 4 + [pl.BlockSpec((1, w), lambda i: (0, 0)), pl.BlockSpec((1, LANES), lambda i: (0, 0))],
        out_specs=tile, out_shape=jax.ShapeDtypeStruct((b, s, w), BF16),
        scratch_shapes=[pltpu.VMEM((b * (w // LANES), LANES, LANES), F32)],
        compiler_params=_cparams(("arbitrary",)), name="hgrn2",
    )(q, f, iv, gate, lb.reshape(1, w).astype(F32), jnp.tile(gn_g.astype(F32), rep).reshape(1, LANES))


def _memory_kv(memf, b, g, w_kv):
    wb = w_kv.astype(BF16)
    km, vm = rms_proj(memf, g, [wb[:, :M_W], wb[:, M_W:]], [("plain", F32), ("plain", BF16)])
    return km.reshape(b, N_MEM, M_W), vm.reshape(b, N_MEM, M_W)


def _even_layer(xf, b, s, km, vm, tables, ln_g, w_in, w_out, a_qn_g, a_kn_g, m_qn_g, m_kn_g,
                mu, w0, w2, a0, a2, k_k, k_a, r_k, lnx_g, lnx_b):
    wb = w_in.astype(BF16)
    edges = [0]
    for width in (A_W, A_W, A_W, A_W, B_W, B_W, B_W, 2 * B_LORA, B_W, M_W, M_W):
        edges.append(edges[-1] + width)
    ws = [wb[:, edges[n]:edges[n + 1]] for n in range(11)]
    ws[7] = jnp.pad(ws[7], ((0, 0), (0, LANES - 2 * B_LORA)))
    f32 = ("plain", F32)
    kinds = [("qk", a_qn_g, F32, 1.0, True, False), ("qk", a_kn_g, BF16, 1.0, False, True), ("vt",),
             f32, f32, f32, f32, f32, f32, f32, f32]
    q_t, k, kmean, v_t, ga, rr, rk, rv, lora, gb, qm, gm = rms_proj(xf, ln_g, ws, kinds, tables, s)
    sh = lambda t: t.reshape(b, s, t.shape[-1])
    oa = moba_attention(q_t, sh(k), v_t, kmean.reshape(b, s // MOBA_BLOCK, A_W), sh(ga))
    pre = rwkv_prep(sh(rr), sh(rk), sh(rv), sh(lora), mu, w0, w2, a0, a2, k_k, k_a, r_k.reshape(-1))
    ob = rwkv_scan(*pre, sh(gb), lnx_g, lnx_b)
    om = mem_attention(sh(qm), km, vm, sh(gm), m_qn_g, m_kn_g)
    wo = w_out.astype(BF16)
    fl = lambda t: t.reshape(b * s, t.shape[-1])
    return out_proj(xf, [fl(oa), fl(ob), fl(om)], [wo[:A_W], wo[A_W:A_W + B_W], wo[A_W + B_W:]])


def _odd_layer(xf, b, s, km, vm, tables, li, lb, ln_g, w_in, w_out, c_qn_g, c_kn_g, lqk, subln_g, d_gn_g,
               m_qn_g, m_kn_g):
    wb = w_in.astype(BF16)
    edges = [0]
    for width in (C_W, C_W, C_W, C_W, D_W, D_W, D_W, D_W, M_W, M_W):
        edges.append(edges[-1] + width)
    ws = [wb[:, edges[n]:edges[n + 1]] for n in range(10)]
    f32 = ("plain", F32)
    kinds = [("qk", c_qn_g, BF16, ATTN_SCALE * LOG2E, True, False), ("qk", c_kn_g, BF16, 1.0, False, False),
             ("vt",), f32, f32, f32, f32, f32, f32, f32]
    q_t, k, v_t, gc, qd, fd, idd, gd, qm, gm = rms_proj(xf, ln_g, ws, kinds, tables, s)
    sh = lambda t: t.reshape(b, s, t.shape[-1])
    lam_init = 0.8 - 0.6 * math.exp(-0.3 * li)
    oc = diff_attention(q_t, sh(k), v_t, sh(gc), lqk, subln_g, lam_init)
    od = hgrn2(sh(qd), sh(fd), sh(idd), sh(gd), lb, d_gn_g)
    om = mem_attention(sh(qm), km, vm, sh(gm), m_qn_g, m_kn_g)
    wo = w_out.astype(BF16)
    fl = lambda t: t.reshape(b * s, t.shape[-1])
    return out_proj(xf, [fl(oc), fl(od), fl(om)], [wo[:C_W], wo[C_W:C_W + D_W], wo[C_W + D_W:]])


def kernel(x, mem, ln_g, mem_ln_g, w_mem_kv, m_qn_g, m_kn_g, e_w_in, e_w_out, a_qn_g, a_kn_g, b_mu, b_w0, b_w2, b_a0, b_a2, b_k_k, b_k_a, b_r_k, b_lnx_g, b_lnx_b, o_w_in, o_w_out, c_qn_g, c_kn_g, c_lq1, c_lk1, c_lq2, c_lk2, c_subln_g, d_lb, d_gn_g):
    b, s, d = x.shape
    depth = ln_g.shape[0]
    tables = rope_tables_lanes(s)
    lbs = jax.nn.softmax(d_lb.astype(F32), axis=0)
    lbs = jnp.cumsum(lbs, axis=0) - lbs[0:1]
    xf = x.reshape(b * s, d)
    memf = mem.reshape(b * N_MEM, d)
    for li in range(depth):
        j = li // 2
        km, vm = _memory_kv(memf, b, mem_ln_g[li], w_mem_kv[li])
        if li % 2 == 0:
            xf = _even_layer(xf, b, s, km, vm, tables, ln_g[li], e_w_in[j], e_w_out[j], a_qn_g[j], a_kn_g[j],
                             m_qn_g[li], m_kn_g[li], b_mu[j], b_w0[j], b_w2[j], b_a0[j], b_a2[j], b_k_k[j],
                             b_k_a[j], b_r_k[j], b_lnx_g[j], b_lnx_b[j])
        else:
            lqk = jnp.stack([c_lq1[j], c_lk1[j], c_lq2[j], c_lk2[j]]).astype(F32)
            xf = _odd_layer(xf, b, s, km, vm, tables, li, jnp.maximum(lbs[j], 0.0), ln_g[li], o_w_in[j],
                            o_w_out[j], c_qn_g[j], c_kn_g[j], lqk, c_subln_g[j], d_gn_g[j], m_qn_g[li], m_kn_g[li])
    return xf.reshape(b, s, d)
```

```python
import functools
import itertools
import math

import jax
import jax.numpy as jnp
from jax import lax
from jax.experimental import pallas as pl
from jax.experimental.pallas import tpu as pltpu

F32 = jnp.float32
BF16 = jnp.bfloat16
HIGHEST = lax.Precision.HIGHEST

N_MEM = 256
HEAD_DIM = 64
ROPE_THETA = 500000.0
ROPE_DIM = HEAD_DIM // 4
NORM_EPS = 1e-6
NEG_INF = -1e30
A_HEADS = 6
MOBA_BLOCK = 256
MOBA_TOPK = 3
B_HEADS = 6
B_LORA = 32
RWKV_GN_EPS = 64e-5
C_HEADS = 4
D_HEADS = 4
M_HEADS = 4
A_W = A_HEADS * HEAD_DIM
B_W = B_HEADS * HEAD_DIM
C_W = C_HEADS * 2 * HEAD_DIM
D_W = D_HEADS * HEAD_DIM
M_W = M_HEADS * HEAD_DIM
ATTN_SCALE = HEAD_DIM ** -0.5

LANES = 128
VMEM_LIMIT = 48 * 1024 * 1024

ROW_TILE = 512
OUT_ROWS = 1024
MEM_ROWS = 1024
LOG2E = math.log2(math.e)
ONES_ROWS = 16
ATTN_TILE = 256
Q_TILE = 2 * ATTN_TILE
FLASH_UNROLL = 4
CHUNK = 64
SUB = 16
SCAN_ROWS = 512

_NT = (((1,), (1,)), ((), ()))
_TN = (((0,), (0,)), ((), ()))


def _cparams(sem):
    return pltpu.CompilerParams(dimension_semantics=sem, vmem_limit_bytes=VMEM_LIMIT)


def _mm(a, b):
    return jnp.dot(a.astype(BF16), b.astype(BF16), preferred_element_type=F32)


def _mm_nt(a, b):
    return lax.dot_general(a.astype(BF16), b.astype(BF16), _NT, preferred_element_type=F32)


def _mm_tn(a, b):
    return lax.dot_general(a.astype(BF16), b.astype(BF16), _TN, preferred_element_type=F32)


def _split(x, terms):
    parts = []
    for _ in range(terms - 1):
        hi = x.astype(BF16)
        parts.append(hi)
        x = x - hi.astype(F32)
    parts.append(x.astype(BF16))
    return parts[::-1]


def _mm_sel_r(x, sel, terms=3):
    sb = sel.astype(BF16)
    return sum(jnp.dot(part, sb, preferred_element_type=F32) for part in _split(x, terms))


def _mm_sel_l(sel, x, terms=3):
    sb = sel.astype(BF16)
    return sum(jnp.dot(sb, part, preferred_element_type=F32) for part in _split(x, terms))


def _mm_16bit(a, b):
    a_lo, a_hi = _split(a, 2)
    b_lo, b_hi = _split(b, 2)
    return (jnp.dot(a_lo, b_hi, preferred_element_type=F32) + jnp.dot(a_hi, b_lo, preferred_element_type=F32)
            + jnp.dot(a_hi, b_hi, preferred_element_type=F32))


def _silu(x):
    return x * jax.nn.sigmoid(x)


def _lane_lo(width=LANES):
    lane = lax.broadcasted_iota(jnp.int32, (1, width), 1)
    return (lane % LANES) < HEAD_DIM


def _head_mean_matrix(width):
    r = lax.broadcasted_iota(jnp.int32, (width, width), 0) // HEAD_DIM
    c = lax.broadcasted_iota(jnp.int32, (width, width), 1) // HEAD_DIM
    return jnp.where(r == c, 1.0 / HEAD_DIM, 0.0).astype(F32)


def _proj_body(*refs, kinds, has_rope):
    it = iter(refs)
    x_ref, g_ref = next(it), next(it)
    if has_rope:
        cos, sin1, sin2 = next(it)[...], next(it)[...], next(it)[...]
        gm = _head_mean_matrix(LANES)
    w_refs = [next(it) for _ in kinds]
    gain_refs = [next(it) if kind[0] == "qk" else None for kind in kinds]
    x = x_ref[...]
    tm = x.shape[0]
    ms = jnp.mean(x * x, axis=-1, keepdims=True)
    h = (x * lax.rsqrt(ms + NORM_EPS) * g_ref[...]).astype(BF16)
    for kind, w_ref, gain_ref in zip(kinds, w_refs, gain_refs):
        acc = jnp.dot(h, w_ref[...], preferred_element_type=F32)
        n_blocks = acc.shape[1] // LANES
        o_ref = next(it)
        if kind[0] == "plain":
            o_ref[...] = acc.astype(o_ref.dtype)
        elif kind[0] == "vt":
            for kt in range(tm // ATTN_TILE):
                for c in range(n_blocks):
                    tile = acc[kt * ATTN_TILE:(kt + 1) * ATTN_TILE, c * LANES:(c + 1) * LANES]
                    o_ref[kt, c] = tile.T.astype(o_ref.dtype)
        else:
            _, _, _, scale, transposed, block_mean = kind
            mean_ref = next(it) if block_mean else None
            for c in range(n_blocks):
                cols = slice(c * LANES, (c + 1) * LANES)
                y = acc[:, cols]
                y = y * lax.rsqrt(_mm_sel_r(y * y, gm, 2) + NORM_EPS) * gain_ref[...]
                yr = y * cos + pltpu.roll(y, LANES - ROPE_DIM // 2, 1) * sin1 + pltpu.roll(y, ROPE_DIM // 2, 1) * sin2
                if transposed:
                    o_ref[0, cols, :] = (yr * scale).T.astype(o_ref.dtype)
                else:
                    o_ref[:, cols] = (yr * scale).astype(o_ref.dtype)
                if block_mean:
                    for rb in range(tm // MOBA_BLOCK):
                        blk = yr[rb * MOBA_BLOCK:(rb + 1) * MOBA_BLOCK]
                        mean_ref[rb, :, cols] = jnp.mean(blk, axis=0, keepdims=True)


def rms_proj(x2d, g, ws, kinds, tables=None, seq=None):
    n, d = x2d.shape
    tm = min(ROW_TILE, n)
    has_rope = tables is not None
    nt = seq // tm if has_rope else 1
    row = lambda i: (i, 0)
    const = lambda i: (0, 0)
    in_specs = [pl.BlockSpec((tm, d), row), pl.BlockSpec((1, d), const)]
    args = [x2d, g.reshape(1, d).astype(F32)]
    if has_rope:
        in_specs += [pl.BlockSpec((tm, LANES), lambda i: (i % nt, 0))] * 3
        args += list(tables)
    in_specs += [pl.BlockSpec(w.shape, const) for w in ws]
    args += list(ws)
    out_specs, out_shape = [], []
    for w, kind in zip(ws, kinds):
        nw = w.shape[1]
        if kind[0] == "plain":
            out_specs.append(pl.BlockSpec((tm, nw), row))
            out_shape.append(jax.ShapeDtypeStruct((n, nw), kind[1]))
        elif kind[0] == "vt":
            blk = (tm // ATTN_TILE, nw // LANES, LANES, ATTN_TILE)
            out_specs.append(pl.BlockSpec(blk, lambda i: (i, 0, 0, 0)))
            out_shape.append(jax.ShapeDtypeStruct((n // ATTN_TILE,) + blk[1:], BF16))
        else:
            _, gain, dtype, _, transposed, block_mean = kind
            in_specs.append(pl.BlockSpec((1, LANES), const))
            args.append(jnp.tile(gain.astype(F32), LANES // HEAD_DIM).reshape(1, LANES))
            if transposed:
                out_specs.append(pl.BlockSpec((1, nw, tm), lambda i: (i // nt, 0, i % nt)))
                out_shape.append(jax.ShapeDtypeStruct((n // seq, nw, seq), dtype))
            else:
                out_specs.append(pl.BlockSpec((tm, nw), row))
                out_shape.append(jax.ShapeDtypeStruct((n, nw), dtype))
            if block_mean:
                out_specs.append(pl.BlockSpec((tm // MOBA_BLOCK, 1, nw), lambda i: (i, 0, 0)))
                out_shape.append(jax.ShapeDtypeStruct((n // MOBA_BLOCK, 1, nw), F32))
    static_kinds = tuple(k if k[0] != "qk" else (k[0], None) + tuple(k[2:]) for k in kinds)
    return pl.pallas_call(
        functools.partial(_proj_body, kinds=static_kinds, has_rope=has_rope),
        grid=(n // tm,), in_specs=in_specs, out_specs=out_specs, out_shape=out_shape,
        compiler_params=_cparams(("parallel",)), name="rms_proj",
    )(*args)


def _out_proj_body(x_ref, *refs, n_in):
    o_ref = refs[-1]
    acc = x_ref[...]
    for m_ref, w_ref in zip(refs[:n_in], refs[n_in:2 * n_in]):
        acc = acc + jnp.dot(m_ref[...], w_ref[...], preferred_element_type=F32)
    o_ref[...] = acc


def out_proj(x2d, parts, ws):
    n, d = x2d.shape
    tm = min(OUT_ROWS, n)
    in_specs = [pl.BlockSpec((tm, d), lambda i: (i, 0))]
    in_specs += [pl.BlockSpec((tm, p.shape[1]), lambda i: (i, 0)) for p in parts]
    in_specs += [pl.BlockSpec(w.shape, lambda i: (0, 0)) for w in ws]
    return pl.pallas_call(
        functools.partial(_out_proj_body, n_in=len(parts)),
        grid=(n // tm,), in_specs=in_specs, out_specs=pl.BlockSpec((tm, d), lambda i: (i, 0)),
        out_shape=jax.ShapeDtypeStruct((n, d), F32),
        compiler_params=_cparams(("parallel",)), name="out_proj",
    )(x2d, *parts, *ws)


def rope_tables_lanes(seq):
    pos = jnp.arange(seq, dtype=F32)
    inv = 1.0 / (ROPE_THETA ** (jnp.arange(0, ROPE_DIM, 2, dtype=F32) / ROPE_DIM))
    ang = pos[:, None] * inv[None, :]
    cos, sin = jnp.cos(ang), jnp.sin(ang)
    half = ROPE_DIM // 2
    ones = jnp.ones((seq, HEAD_DIM - ROPE_DIM), F32)
    zeros_h = jnp.zeros((seq, half), F32)
    zeros_r = jnp.zeros((seq, HEAD_DIM - ROPE_DIM), F32)
    c = jnp.concatenate([cos, cos, ones], axis=1)
    s1 = jnp.concatenate([-sin, zeros_h, zeros_r], axis=1)
    s2 = jnp.concatenate([zeros_h, sin, zeros_r], axis=1)
    rep = LANES // HEAD_DIM
    return jnp.tile(c, (1, rep)), jnp.tile(s1, (1, rep)), jnp.tile(s2, (1, rep))


def _flash_t(i, k_ref, vt_ref, qm_ref, bufs, m_ref, acc_ref, v_rows, bias_row):
    t = ATTN_TILE
    (s0, s1), (p0, p1), (a0, a1) = bufs
    n_real = 2 * i + 1
    last = jnp.maximum(2 * i - 1, 0)
    ncol = Q_TILE // LANES

    def key_tile(x):
        own = jnp.where(x == 0, 2 * i + 1, 2 * i)
        return jnp.where(x < 2, own, jnp.minimum(x - 2, last))

    def a_part(x, h, s_w):
        kb = jnp.minimum(x - 2, last)
        k_blk = k_ref[0, pl.ds(pl.multiple_of(kb * t, t), t), :]
        s = jnp.dot(k_blk, qm_ref[h], preferred_element_type=F32)
        if bias_row is not None:
            s = s + bias_row(h, kb)
        s_w[h] = s

    def b_part(x, h, c, s_r, p_w, a_w):
        real = x <= n_real
        cs = slice(c * LANES, (c + 1) * LANES)
        s = s_r[h, :, cs]
        m_old = m_ref[h, :, cs]
        m_top = jnp.maximum(m_old, jnp.max(s, axis=0, keepdims=True))
        m_new = jnp.where(real, m_top, m_old)
        m_ref[h, :, cs] = m_new
        a_w[h, :, cs] = jnp.exp2(m_old - m_new)
        p_w[h, :, cs] = jnp.exp2(s - m_top).astype(BF16)

    def c_part(x, h, p_r, a_r):
        v_blk = jnp.concatenate([vt_ref[key_tile(x), 0, v_rows(h), :], jnp.ones((ONES_ROWS, t), BF16)], axis=0)
        v_blk = jnp.where(x <= n_real, v_blk, jnp.zeros_like(v_blk))
        acc_ref[h] = a_r[h] * acc_ref[h] + jnp.dot(v_blk, p_r[h], preferred_element_type=F32)

    def stage_a(x, s_w):
        for h in range(2):
            a_part(x, h, s_w)

    def stage_b(x, s_r, p_w, a_w):
        for h in range(2):
            for c in range(ncol):
                b_part(x, h, c, s_r, p_w, a_w)

    def stage_c(x, p_r, a_r):
        for h in range(2):
            c_part(x, h, p_r, a_r)

    m_ref[...] = jnp.full(m_ref.shape, NEG_INF, F32)
    acc_ref[...] = jnp.zeros(acc_ref.shape, F32)
    kpos = lax.broadcasted_iota(jnp.int32, (t, Q_TILE), 0)
    qpos = lax.broadcasted_iota(jnp.int32, (t, Q_TILE), 1)
    late = qpos >= t
    late_row = lax.broadcasted_iota(jnp.int32, (1, Q_TILE), 1) >= t
    allow_hi = jnp.logical_and(late, kpos <= qpos - t)
    allow_lo = jnp.logical_or(late, kpos <= qpos)
    k_hi = k_ref[0, pl.ds(pl.multiple_of((2 * i + 1) * t, t), t), :]
    k_lo = k_ref[0, pl.ds(pl.multiple_of(2 * i * t, t), t), :]
    for h in range(2):
        s0[h] = jnp.where(allow_hi, jnp.dot(k_hi, qm_ref[h], preferred_element_type=F32), NEG_INF)
        s_lo = jnp.dot(k_lo, qm_ref[h], preferred_element_type=F32)
        if bias_row is not None:
            s_lo = s_lo + jnp.where(late_row, bias_row(h, 2 * i), 0.0)
        s1[h] = jnp.where(allow_lo, s_lo, NEG_INF)
    bufs_of = ((s0, p0, a0), (s1, p1, a1))

    def step(x, k):
        s_w, p_r, a_r = bufs_of[k % 2]
        s_r, p_w, a_w = bufs_of[(k + 1) % 2]
        stage_c(x - 2, p_r, a_r)
        stage_b(x - 1, s_r, p_w, a_w)
        stage_a(x, s_w)

    stage_b(0, s0, p0, a0)
    u_steps = FLASH_UNROLL

    def unrolled(u, carry):
        for k in range(u_steps):
            step(u_steps * u + 2 + k, k)
        return carry

    n_steps = 2 * i + 2
    groups = n_steps // u_steps
    lax.fori_loop(0, groups, unrolled, 0)

    @pl.when(n_steps % u_steps != 0)
    def _():
        for k in range(2):
            step(u_steps * groups + 2 + k, k)


def _flash_scratch(v_rows):
    t, tq = ATTN_TILE, Q_TILE
    return [pltpu.VMEM((2, LANES, tq), BF16),
            pltpu.VMEM((2, t, tq), F32), pltpu.VMEM((2, t, tq), F32),
            pltpu.VMEM((2, t, tq), BF16), pltpu.VMEM((2, t, tq), BF16),
            pltpu.VMEM((2, 1, tq), F32), pltpu.VMEM((2, 1, tq), F32),
            pltpu.VMEM((2, 1, tq), F32), pltpu.VMEM((2, v_rows + ONES_ROWS, tq), F32)]


def _moba_body(qt_ref, k_ref, vt_ref, km_ref, g_ref, o_ref, bias_ref, qm_ref, s0, s1, p0, p1, a0, a1,
               m_ref, acc_ref, *, nb):
    i = pl.program_id(2)
    t = Q_TILE
    bufs = ((s0, s1), (p0, p1), (a0, a1))
    q_t = qt_ref[0]
    dim_lo = lax.broadcasted_iota(jnp.int32, (LANES, 1), 0) < HEAD_DIM
    lo = _lane_lo()
    km = km_ref[0]
    blk = lax.broadcasted_iota(jnp.int32, (nb, t), 0).astype(F32)
    own = (2 * i + (lax.broadcasted_iota(jnp.int32, (1, t), 1) >= MOBA_BLOCK).astype(jnp.int32)).astype(F32)
    for h in range(2):
        qh = jnp.where(dim_lo if h == 0 else jnp.logical_not(dim_lo), q_t, 0.0)
        qm_ref[h] = (qh * (ATTN_SCALE * LOG2E)).astype(BF16)
        kmh = jnp.where(lo if h == 0 else jnp.logical_not(lo), km, 0.0)
        gate = jnp.dot(kmh, q_t, precision=HIGHEST, preferred_element_type=F32)
        gate = jnp.where(blk < own, gate, NEG_INF)
        sel = jnp.zeros((nb, t), F32)
        for _ in range(MOBA_TOPK):
            mx = jnp.max(gate, axis=0, keepdims=True)
            first = jnp.min(jnp.where(gate == mx, blk, float(nb)), axis=0, keepdims=True)
            hit = blk == first
            sel = jnp.where(jnp.logical_and(hit, first < own), 1.0, sel)
            gate = jnp.where(hit, -jnp.inf, gate)
        bias_ref[h] = jnp.where(sel > 0.0, 0.0, NEG_INF)

    _flash_t(i, k_ref, vt_ref, qm_ref, bufs, m_ref, acc_ref,
             v_rows=lambda h: slice(h * HEAD_DIM, (h + 1) * HEAD_DIM),
             bias_row=lambda h, kb: bias_ref[h, pl.ds(kb, 1), :])
    o_t = jnp.concatenate([acc_ref[h, :HEAD_DIM, :] / acc_ref[h, HEAD_DIM:HEAD_DIM + 1, :] for h in range(2)],
                          axis=0)
    o_ref[0] = (o_t.T * _silu(g_ref[0])).astype(o_ref.dtype)


def moba_attention(q_t, k, v_t, kmean, gate):
    b, w, s = q_t.shape
    t, tq = ATTN_TILE, Q_TILE
    nb = s // t
    tile = pl.BlockSpec((1, tq, LANES), lambda bi, p, i: (bi, i, p))
    return pl.pallas_call(
        functools.partial(_moba_body, nb=nb), grid=(b, w // LANES, s // tq),
        in_specs=[pl.BlockSpec((1, LANES, tq), lambda bi, p, i: (bi, p, i)),
                  pl.BlockSpec((1, s, LANES), lambda bi, p, i: (bi, 0, p)),
                  pl.BlockSpec((nb, 1, LANES, t), lambda bi, p, i: (bi, p, 0, 0)),
                  pl.BlockSpec((1, nb, LANES), lambda bi, p, i: (bi, 0, p)), tile],
        out_specs=tile, out_shape=jax.ShapeDtypeStruct((b, s, w), BF16),
        scratch_shapes=[pltpu.VMEM((2, nb, tq), F32)] + _flash_scratch(HEAD_DIM),
        compiler_params=_cparams(("parallel", "parallel", "arbitrary")), name="moba_attention",
    )(q_t, k, v_t, kmean, gate)


def _diff_body(qt_ref, k_ref, vt_ref, g_ref, lqk_ref, sg_ref, o_ref, qm_ref, s0, s1, p0, p1, a0, a1,
               m_ref, acc_ref, *, lam_init):
    i = pl.program_id(2)
    q_t = qt_ref[0]
    dim_lo = lax.broadcasted_iota(jnp.int32, (LANES, 1), 0) < HEAD_DIM
    for h in range(2):
        qm_ref[h] = jnp.where(dim_lo if h == 0 else jnp.logical_not(dim_lo), q_t, 0)
    _flash_t(i, k_ref, vt_ref, qm_ref, ((s0, s1), (p0, p1), (a0, a1)), m_ref, acc_ref,
             v_rows=lambda h: slice(0, LANES), bias_row=None)
    lqk = lqk_ref[...]
    lam = (jnp.exp(jnp.sum(lqk[0:1] * lqk[1:2], axis=1, keepdims=True))
           - jnp.exp(jnp.sum(lqk[2:3] * lqk[3:4], axis=1, keepdims=True)) + lam_init)
    att = [acc_ref[h, :LANES, :] / acc_ref[h, LANES:LANES + 1, :] for h in range(2)]
    o = (att[0] - lam * att[1]).T
    ms = jnp.mean(o * o, axis=1, keepdims=True)
    o = o * lax.rsqrt(ms + NORM_EPS) * sg_ref[...] * (1.0 - lam_init)
    o_ref[0] = (o * _silu(g_ref[0])).astype(o_ref.dtype)


def diff_attention(q_t, k, v_t, gate, lqk, subln_g, lam_init):
    b, w, s = q_t.shape
    t, tq = ATTN_TILE, Q_TILE
    nb = s // t
    tile = pl.BlockSpec((1, tq, LANES), lambda bi, h, i: (bi, i, h))
    return pl.pallas_call(
        functools.partial(_diff_body, lam_init=lam_init), grid=(b, w // LANES, s // tq),
        in_specs=[pl.BlockSpec((1, LANES, tq), lambda bi, h, i: (bi, h, i)),
                  pl.BlockSpec((1, s, LANES), lambda bi, h, i: (bi, 0, h)),
                  pl.BlockSpec((nb, 1, LANES, t), lambda bi, h, i: (bi, h, 0, 0)),
                  tile, pl.BlockSpec((4, HEAD_DIM), lambda bi, h, i: (0, 0)),
                  pl.BlockSpec((1, LANES), lambda bi, h, i: (0, 0))],
        out_specs=tile, out_shape=jax.ShapeDtypeStruct((b, s, w), BF16),
        scratch_shapes=_flash_scratch(LANES),
        compiler_params=_cparams(("parallel", "parallel", "arbitrary")), name="diff_attention",
    )(q_t, k, v_t, gate, lqk, subln_g.reshape(1, LANES).astype(F32))


def _mem_body(q_ref, km_ref, vm_ref, g_ref, qg_ref, kg_ref, o_ref):
    gm = _head_mean_matrix(LANES)
    lo = _lane_lo()
    masks = (lo, jnp.logical_not(lo))
    q = q_ref[0]
    q = q * lax.rsqrt(_mm_sel_r(q * q, gm, 2) + NORM_EPS) * qg_ref[...]
    k = km_ref[0]
    k = (k * lax.rsqrt(_mm_sel_r(k * k, gm, 2) + NORM_EPS) * kg_ref[...]).astype(BF16)
    v = vm_ref[0]
    outs = []
    for h in range(2):
        qh = (jnp.where(masks[h], q, 0.0) * ATTN_SCALE).astype(BF16)
        s = _mm_nt(qh, k)
        p = jnp.exp(s - jnp.max(s, axis=1, keepdims=True))
        outs.append(_mm(p, v) / jnp.sum(p, axis=1, keepdims=True))
    out = jnp.where(lo, outs[0], outs[1])
    o_ref[0] = (out * _silu(g_ref[0])).astype(o_ref.dtype)


def mem_attention(q, km, vm, gate, q_gain, k_gain):
    b, s, w = q.shape
    t = MEM_ROWS
    tile = pl.BlockSpec((1, t, LANES), lambda bi, p, i: (bi, i, p))
    mem = pl.BlockSpec((1, N_MEM, LANES), lambda bi, p, i: (bi, 0, p))
    gain = pl.BlockSpec((1, LANES), lambda bi, p, i: (0, 0))
    rep = LANES // HEAD_DIM
    return pl.pallas_call(
        _mem_body, grid=(b, w // LANES, s // t),
        in_specs=[tile, mem, mem, tile, gain, gain],
        out_specs=tile, out_shape=jax.ShapeDtypeStruct((b, s, w), BF16),
        compiler_params=_cparams(("parallel", "parallel", "parallel")), name="mem_attention",
    )(q, km, vm, gate, jnp.tile(q_gain.astype(F32), rep).reshape(1, LANES),
      jnp.tile(k_gain.astype(F32), rep).reshape(1, LANES))


def _shift(cur, prev8, mu, first):
    rows = lax.broadcasted_iota(jnp.int32, cur.shape, 0)
    before = jnp.where(first, 0.0, prev8[7:8, :])
    prev = jnp.where(rows == 0, before, pltpu.roll(cur, 1, 0))
    return cur + (prev - cur) * mu


def _rwkv_prep_body(r_ref, k_ref, v_ref, lo_ref, rp_ref, kp_ref, vp_ref, lp_ref, mu_ref, mul_ref, w0_ref, w2_ref,
                    a0_ref, a2_ref, kk_ref, ka_ref, rk_ref,
                    ro_ref, lw_ref, ko_ref, vo_ref, kn_ref, ao_ref, bo_ref):
    first = pl.program_id(1) == 0
    mu = mu_ref[...]
    r = _shift(r_ref[0], rp_ref[0], mu[0:1], first)
    k = _shift(k_ref[0], kp_ref[0], mu[1:2], first)
    v = _shift(v_ref[0], vp_ref[0], mu[2:3], first)
    lora = _shift(lo_ref[0], lp_ref[0], mul_ref[...], first)
    z = w0_ref[...] + _mm_16bit(jnp.tanh(lora), w2_ref[...])
    w_log = -(jnp.maximum(-z, 0.0) + jnp.log1p(jnp.exp(-jnp.abs(z)))) - 0.5
    a = jax.nn.sigmoid(a0_ref[...] + _mm_16bit(lora, a2_ref[...]))
    kk = k * kk_ref[...]
    k2 = k * (1.0 + (a - 1.0) * ka_ref[...])
    ones = _head_mean_matrix(B_W) * float(HEAD_DIM)
    kk = kk * lax.rsqrt(jnp.maximum(_mm_sel_r(kk * kk, ones, 2), 1e-24))
    ro_ref[0] = r
    lw_ref[0] = -jnp.exp(w_log)
    ko_ref[0] = k2
    vo_ref[0] = v
    kn_ref[0] = kk
    ao_ref[0] = a
    bo_ref[0] = _mm_sel_r(r * k2 * rk_ref[...], ones, 2) * v


def rwkv_prep(r, k, v, lora, mu, w0, w2, a0, a2, k_k, k_a, r_k):
    b, s, w = r.shape
    t = ROW_TILE
    tile = pl.BlockSpec((1, t, w), lambda bi, i: (bi, i, 0))
    ltile = pl.BlockSpec((1, t, LANES), lambda bi, i: (bi, i, 0))
    prev_idx = lambda bi, i: (bi, jnp.maximum(i * (t // 8) - 1, 0), 0)
    ptile = pl.BlockSpec((1, 8, w), prev_idx)
    pltile = pl.BlockSpec((1, 8, LANES), prev_idx)
    const = lambda shape: pl.BlockSpec(shape, lambda bi, i: (0, 0))
    mu3 = jnp.stack([mu[:w], mu[w:2 * w], mu[2 * w:3 * w]]).astype(F32)
    mul = jnp.zeros((1, LANES), F32).at[0, :2 * B_LORA].set(mu[3 * w:])
    w2p = jnp.zeros((LANES, w), F32).at[:B_LORA].set(w2)
    a2p = jnp.zeros((LANES, w), F32).at[B_LORA:2 * B_LORA].set(a2)
    row = lambda p: p.reshape(1, w).astype(F32)
    return pl.pallas_call(
        _rwkv_prep_body, grid=(b, s // t),
        in_specs=[tile, tile, tile, ltile, ptile, ptile, ptile, pltile, const((3, w)), const((1, LANES)),
                  const((1, w)), const((LANES, w)), const((1, w)), const((LANES, w)), const((1, w)), const((1, w)),
                  const((1, w))],
        out_specs=[tile] * 7, out_shape=[jax.ShapeDtypeStruct((b, s, w), F32)] * 7,
        compiler_params=_cparams(("parallel", "parallel")), name="rwkv_prep",
    )(r, k, v, lora, r, k, v, lora, mu3, mul, row(w0), w2p, row(a0), a2p, row(k_k), row(k_a), row(r_k))


def _tri(n, strict):
    r = lax.broadcasted_iota(jnp.int32, (n, n), 0)
    c = lax.broadcasted_iota(jnp.int32, (n, n), 1)
    return (c < r) if strict else (c <= r)


def _block_diag_mask():
    r = lax.broadcasted_iota(jnp.int32, (LANES, LANES), 0) // HEAD_DIM
    c = lax.broadcasted_iota(jnp.int32, (LANES, LANES), 1) // HEAD_DIM
    return r == c


def _rwkv_scan_body(r_ref, lw_ref, k_ref, v_ref, kn_ref, a_ref, bo_ref, g_ref, lg_ref, lb_ref, o_ref, s_ref, *,
                    nbatch, npair):
    @pl.when(pl.program_id(0) == 0)
    def _():
        s_ref[...] = jnp.zeros_like(s_ref)

    c = CHUNK
    lo = _lane_lo()
    masks = (lo, jnp.logical_not(lo))
    strict = _tri(c, True)
    incl = _tri(c, False)
    ltri = incl.astype(F32)
    eye = (lax.broadcasted_iota(jnp.int32, (c, c), 0) == lax.broadcasted_iota(jnp.int32, (c, c), 1)).astype(F32)
    bd = _block_diag_mask()
    gmean = _head_mean_matrix(LANES)

    def chain(sl, bi, pi):
        cols = slice(pi * LANES, (pi + 1) * LANES)
        idx = bi * npair + pi
        r, lw, k, v, kn, a = (ref[bi, sl, cols] for ref in (r_ref, lw_ref, k_ref, v_ref, kn_ref, a_ref))
        cw = _mm_sel_l(ltri, lw)
        yield
        cl = cw[c - 1:c, :]
        at = -kn * jnp.exp(cw - lw)
        e_neg = jnp.exp(-cw)
        bt = kn * a * e_neg
        kt = k * e_neg
        rt = r * jnp.exp(cw)
        e_end = jnp.exp(cl - cw)
        s0 = s_ref[idx]
        rhs = _mm_nt(at, s0)
        ys0 = _mm_nt(rt, s0)
        yield
        ah = [jnp.where(mk, at, 0.0) for mk in masks]
        rh = [jnp.where(mk, rt, 0.0) for mk in masks]
        n = [jnp.where(strict, _mm_nt(x, bt), 0.0) for x in ah]
        yield
        aak = [jnp.where(strict, _mm_nt(x, kt), 0.0) for x in ah]
        yield
        arb = [jnp.where(incl, _mm_nt(x, bt), 0.0) for x in rh]
        yield
        ark = [jnp.where(incl, _mm_nt(x, kt), 0.0) for x in rh]
        yield
        xs = [rhs + _mm(x, v) for x in aak]
        yv = [_mm(x, v) for x in ark]
        yield
        tinv = [eye + x for x in n]
        p = n
        for _ in range(5):
            p = [_mm(x, x) for x in p]
            yield
            tinv = [x + _mm(x, y) for x, y in zip(tinv, p)]
            yield
        u = jnp.where(lo, _mm(tinv[0], xs[0]), _mm(tinv[1], xs[1]))
        yield
        y = ys0 + jnp.where(lo, _mm(arb[0], u) + yv[0], _mm(arb[1], u) + yv[1])
        upd = _mm_tn(u, kn * a * e_end) + _mm_tn(v, k * e_end)
        yield
        s_ref[idx] = s0 * jnp.exp(cl) + jnp.where(bd, upd, 0.0)
        mean = _mm_sel_r(y, gmean, 2)
        yield
        d = y - mean
        var = _mm_sel_r(d * d, gmean, 2)
        yield
        yn = d * lax.rsqrt(var + RWKV_GN_EPS) * lg_ref[:, cols] + lb_ref[:, cols] + bo_ref[bi, sl, cols]
        o_ref[bi, sl, cols] = (yn * _silu(g_ref[bi, sl, cols])).astype(o_ref.dtype)

    def chunk(ci, carry):
        sl = pl.ds(pl.multiple_of(ci * c, c), c)
        gens = [chain(sl, bi, pi) for bi in range(nbatch) for pi in range(npair)]
        for _ in itertools.zip_longest(*gens):
            pass
        return carry

    lax.fori_loop(0, SCAN_ROWS // c, chunk, 0)


def rwkv_scan(r, lw, k, v, kn, a, bonus, gate, lnx_g, lnx_b):
    b, s, w = r.shape
    t = SCAN_ROWS
    tile = pl.BlockSpec((b, t, w), lambda i: (0, i, 0))
    vec = pl.BlockSpec((1, w), lambda i: (0, 0))
    return pl.pallas_call(
        functools.partial(_rwkv_scan_body, nbatch=b, npair=w // LANES), grid=(s // t,),
        in_specs=[tile] * 8 + [vec, vec],
        out_specs=tile, out_shape=jax.ShapeDtypeStruct((b, s, w), BF16),
        scratch_shapes=[pltpu.VMEM((b * (w // LANES), LANES, LANES), F32)],
        compiler_params=_cparams(("arbitrary",)), name="rwkv_scan",
    )(r, lw, k, v, kn, a, bonus, gate, lnx_g.reshape(1, w).astype(F32), lnx_b.reshape(1, w).astype(F32))


def _hgrn_body(q_ref, f_ref, i_ref, g_ref, lb_ref, gn_ref, o_ref, s_ref, *, nbatch, npair):
    @pl.when(pl.program_id(0) == 0)
    def _():
        s_ref[...] = jnp.zeros_like(s_ref)

    c = CHUNK
    lo = _lane_lo()
    masks = (lo, jnp.logical_not(lo))
    ltri = _tri(c, False).astype(F32)
    bd = _block_diag_mask()
    gmean = _head_mean_matrix(LANES)
    head_ones = gmean * float(HEAD_DIM)
    rows = lax.broadcasted_iota(jnp.int32, (SUB, 1), 0)

    def chain(sl, bi, pi):
        cols = slice(pi * LANES, (pi + 1) * LANES)
        idx = bi * npair + pi
        lb = lb_ref[:, cols]
        log_lb = jnp.log(lb)
        log_1m = jnp.log1p(-lb)
        q, fr, v = q_ref[bi, sl, cols], f_ref[bi, sl, cols], i_ref[bi, sl, cols]
        log_sig = jnp.minimum(fr, 0.0) - jnp.log1p(jnp.exp(-jnp.abs(fr)))
        z = log_1m + log_sig
        hi = jnp.maximum(log_lb, z)
        log_f = hi + jnp.log1p(jnp.exp(-jnp.abs(log_lb - z)))
        k = (1.0 - lb) * jax.nn.sigmoid(-fr)
        bc = _mm_sel_l(ltri, log_f)
        yield
        bl = bc[c - 1:c, :]
        s0 = s_ref[idx]
        outs = []
        for sb in range(c // SUB):
            r0 = sb * SUB
            qs, bs, ks, vs = (x[r0:r0 + SUB] for x in (q, bc, k, v))
            o = _mm_nt(qs * jnp.exp(bs), s0)
            if sb > 0:
                ref = bc[r0 - 1:r0, :]
                qh = qs * jnp.exp(bs - ref)
                kh = k[:r0] * jnp.exp(ref - bc[:r0])
                vh = v[:r0]
                sc = [_mm_nt(jnp.where(mk, qh, 0.0), kh) for mk in masks]
                yield
                o = o + jnp.where(lo, _mm(sc[0], vh), _mm(sc[1], vh))
            pair = jnp.concatenate(
                [qs * ks[si:si + 1, :] * jnp.exp(jnp.where(rows >= si, bs - bs[si:si + 1, :], NEG_INF))
                 for si in range(SUB)], axis=0)
            score = _mm_sel_r(pair, head_ones, 2)
            yield
            for si in range(SUB):
                o = o + score[si * SUB:(si + 1) * SUB] * vs[si:si + 1, :]
            outs.append(o)
        od = jnp.concatenate(outs, axis=0)
        s_ref[idx] = s0 * jnp.exp(bl) + jnp.where(bd, _mm_tn(v, k * jnp.exp(bl - bc)), 0.0)
        ms = _mm_sel_r(od * od, gmean, 2)
        yield
        on = od * lax.rsqrt(ms + NORM_EPS) * gn_ref[...]
        o_ref[bi, sl, cols] = (on * _silu(g_ref[bi, sl, cols])).astype(o_ref.dtype)

    def chunk(ci, carry):
        sl = pl.ds(pl.multiple_of(ci * c, c), c)
        gens = [chain(sl, bi, pi) for bi in range(nbatch) for pi in range(npair)]
        for _ in itertools.zip_longest(*gens):
            pass
        return carry

    lax.fori_loop(0, SCAN_ROWS // c, chunk, 0)


def hgrn2(q, f, iv, gate, lb, gn_g):
    b, s, w = q.shape
    t = SCAN_ROWS
    tile = pl.BlockSpec((b, t, w), lambda i: (0, i, 0))
    rep = LANES // HEAD_DIM
    return pl.pallas_call(
        functools.partial(_hgrn_body, nbatch=b, npair=w // LANES), grid=(s // t,),
        in_specs=[tile] * 4 + [pl.BlockSpec((1, w), lambda i: (0, 0)), pl.BlockSpec((1, LANES), lambda i: (0, 0))],
        out_specs=tile, out_shape=jax.ShapeDtypeStruct((b, s, w), BF16),
        scratch_shapes=[pltpu.VMEM((b * (w // LANES), LANES, LANES), F32)],
        compiler_params=_cparams(("arbitrary",)), name="hgrn2",
    )(q, f, iv, gate, lb.reshape(1, w).astype(F32), jnp.tile(gn_g.astype(F32), rep).reshape(1, LANES))


def _memory_kv(memf, b, g, w_kv):
    wb = w_kv.astype(BF16)
    km, vm = rms_proj(memf, g, [wb[:, :M_W], wb[:, M_W:]], [("plain", F32), ("plain", BF16)])
    return km.reshape(b, N_MEM, M_W), vm.reshape(b, N_MEM, M_W)


def _even_layer(xf, b, s, km, vm, tables, ln_g, w_in, w_out, a_qn_g, a_kn_g, m_qn_g, m_kn_g,
                mu, w0, w2, a0, a2, k_k, k_a, r_k, lnx_g, lnx_b):
    wb = w_in.astype(BF16)
    edges = [0]
    for width in (A_W, A_W, A_W, A_W, B_W, B_W, B_W, 2 * B_LORA, B_W, M_W, M_W):
        edges.append(edges[-1] + width)
    ws = [wb[:, edges[n]:edges[n + 1]] for n in range(11)]
    ws[7] = jnp.pad(ws[7], ((0, 0), (0, LANES - 2 * B_LORA)))
    f32 = ("plain", F32)
    kinds = [("qk", a_qn_g, F32, 1.0, True, False), ("qk", a_kn_g, BF16, 1.0, False, True), ("vt",),
             f32, f32, f32, f32, f32, f32, f32, f32]
    q_t, k, kmean, v_t, ga, rr, rk, rv, lora, gb, qm, gm = rms_proj(xf, ln_g, ws, kinds, tables, s)
    sh = lambda t: t.reshape(b, s, t.shape[-1])
    oa = moba_attention(q_t, sh(k), v_t, kmean.reshape(b, s // MOBA_BLOCK, A_W), sh(ga))
    pre = rwkv_prep(sh(rr), sh(rk), sh(rv), sh(lora), mu, w0, w2, a0, a2, k_k, k_a, r_k.reshape(-1))
    ob = rwkv_scan(*pre, sh(gb), lnx_g, lnx_b)
    om = mem_attention(sh(qm), km, vm, sh(gm), m_qn_g, m_kn_g)
    wo = w_out.astype(BF16)
    fl = lambda t: t.reshape(b * s, t.shape[-1])
    return out_proj(xf, [fl(oa), fl(ob), fl(om)], [wo[:A_W], wo[A_W:A_W + B_W], wo[A_W + B_W:]])


def _odd_layer(xf, b, s, km, vm, tables, li, lb, ln_g, w_in, w_out, c_qn_g, c_kn_g, lqk, subln_g, d_gn_g,
               m_qn_g, m_kn_g):
    wb = w_in.astype(BF16)
    edges = [0]
    for width in (C_W, C_W, C_W, C_W, D_W, D_W, D_W, D_W, M_W, M_W):
        edges.append(edges[-1] + width)
    ws = [wb[:, edges[n]:edges[n + 1]] for n in range(10)]
    f32 = ("plain", F32)
    kinds = [("qk", c_qn_g, BF16, ATTN_SCALE * LOG2E, True, False), ("qk", c_kn_g, BF16, 1.0, False, False),
             ("vt",), f32, f32, f32, f32, f32, f32, f32]
    q_t, k, v_t, gc, qd, fd, idd, gd, qm, gm = rms_proj(xf, ln_g, ws, kinds, tables, s)
    sh = lambda t: t.reshape(b, s, t.shape[-1])
    lam_init = 0.8 - 0.6 * math.exp(-0.3 * li)
    oc = diff_attention(q_t, sh(k), v_t, sh(gc), lqk, subln_g, lam_init)
    od = hgrn2(sh(qd), sh(fd), sh(idd), sh(gd), lb, d_gn_g)
    om = mem_attention(sh(qm), km, vm, sh(gm), m_qn_g, m_kn_g)
    wo = w_out.astype(BF16)
    fl = lambda t: t.reshape(b * s, t.shape[-1])
    return out_proj(xf, [fl(oc), fl(od), fl(om)], [wo[:C_W], wo[C_W:C_W + D_W], wo[C_W + D_W:]])


def kernel(x, mem, ln_g, mem_ln_g, w_mem_kv, m_qn_g, m_kn_g, e_w_in, e_w_out, a_qn_g, a_kn_g, b_mu, b_w0, b_w2, b_a0, b_a2, b_k_k, b_k_a, b_r_k, b_lnx_g, b_lnx_b, o_w_in, o_w_out, c_qn_g, c_kn_g, c_lq1, c_lk1, c_lq2, c_lk2, c_subln_g, d_lb, d_gn_g):
    b, s, d = x.shape
    depth = ln_g.shape[0]
    tables = rope_tables_lanes(s)
    lbs = jax.nn.softmax(d_lb.astype(F32), axis=0)
    lbs = jnp.cumsum(lbs, axis=0) - lbs[0:1]
    xf = x.reshape(b * s, d)
    memf = mem.reshape(b * N_MEM, d)
    for li in range(depth):
        j = li // 2
        km, vm = _memory_kv(memf, b, mem_ln_g[li], w_mem_kv[li])
        if li % 2 == 0:
            xf = _even_layer(xf, b, s, km, vm, tables, ln_g[li], e_w_in[j], e_w_out[j], a_qn_g[j], a_kn_g[j],
                             m_qn_g[li], m_kn_g[li], b_mu[j], b_w0[j], b_w2[j], b_a0[j], b_a2[j], b_k_k[j],
                             b_k_a[j], b_r_k[j], b_lnx_g[j], b_lnx_b[j])
        else:
            lqk = jnp.stack([c_lq1[j], c_lk1[j], c_lq2[j], c_lk2[j]]).astype(F32)
            xf = _odd_layer(xf, b, s, km, vm, tables, li, jnp.maximum(lbs[j], 0.0), ln_g[li], o_w_in[j],
                            o_w_out[j], c_qn_g[j], c_kn_g[j], lqk, c_subln_g[j], d_gn_g[j], m_qn_g[li], m_kn_g[li])
    return xf.reshape(b, s, d)
```

```python
import functools
import itertools
import math

import jax
import jax.numpy as jnp
from jax import lax
from jax.experimental import pallas as pl
from jax.experimental.pallas import tpu as pltpu

F32 = jnp.float32
BF16 = jnp.bfloat16
HIGHEST = lax.Precision.HIGHEST

N_MEM = 256
HEAD_DIM = 64
ROPE_THETA = 500000.0
ROPE_DIM = HEAD_DIM // 4
NORM_EPS = 1e-6
NEG_INF = -1e30
A_HEADS = 6
MOBA_BLOCK = 256
MOBA_TOPK = 3
B_HEADS = 6
B_LORA = 32
RWKV_GN_EPS = 64e-5
C_HEADS = 4
D_HEADS = 4
M_HEADS = 4
A_W = A_HEADS * HEAD_DIM
B_W = B_HEADS * HEAD_DIM
C_W = C_HEADS * 2 * HEAD_DIM
D_W = D_HEADS * HEAD_DIM
M_W = M_HEADS * HEAD_DIM
ATTN_SCALE = HEAD_DIM ** -0.5

LANES = 128
VMEM_LIMIT = 48 * 1024 * 1024

ROW_TILE = 512
OUT_ROWS = 1024
MEM_ROWS = 1024
LOG2E = math.log2(math.e)
ONES_ROWS = 16
ATTN_TILE = 256
Q_TILE = 2 * ATTN_TILE
FLASH_UNROLL = 8
CHUNK = 64
SUB = 16
SCAN_ROWS = 512

_NT = (((1,), (1,)), ((), ()))
_TN = (((0,), (0,)), ((), ()))


def _cparams(sem):
    return pltpu.CompilerParams(dimension_semantics=sem, vmem_limit_bytes=VMEM_LIMIT)


def _mm(a, b):
    return jnp.dot(a.astype(BF16), b.astype(BF16), preferred_element_type=F32)


def _mm_nt(a, b):
    return lax.dot_general(a.astype(BF16), b.astype(BF16), _NT, preferred_element_type=F32)


def _mm_tn(a, b):
    return lax.dot_general(a.astype(BF16), b.astype(BF16), _TN, preferred_element_type=F32)


def _split(x, terms):
    parts = []
    for _ in range(terms - 1):
        hi = x.astype(BF16)
        parts.append(hi)
        x = x - hi.astype(F32)
    parts.append(x.astype(BF16))
    return parts[::-1]


def _mm_sel_r(x, sel, terms=3):
    sb = sel.astype(BF16)
    return sum(jnp.dot(part, sb, preferred_element_type=F32) for part in _split(x, terms))


def _mm_sel_l(sel, x, terms=3):
    sb = sel.astype(BF16)
    return sum(jnp.dot(sb, part, preferred_element_type=F32) for part in _split(x, terms))


def _mm_16bit(a, b):
    a_lo, a_hi = _split(a, 2)
    b_lo, b_hi = _split(b, 2)
    return (jnp.dot(a_lo, b_hi, preferred_element_type=F32) + jnp.dot(a_hi, b_lo, preferred_element_type=F32)
            + jnp.dot(a_hi, b_hi, preferred_element_type=F32))


def _silu(x):
    return x * jax.nn.sigmoid(x)


def _lane_lo(width=LANES):
    lane = lax.broadcasted_iota(jnp.int32, (1, width), 1)
    return (lane % LANES) < HEAD_DIM


def _head_mean_matrix(width):
    r = lax.broadcasted_iota(jnp.int32, (width, width), 0) // HEAD_DIM
    c = lax.broadcasted_iota(jnp.int32, (width, width), 1) // HEAD_DIM
    return jnp.where(r == c, 1.0 / HEAD_DIM, 0.0).astype(F32)


def _proj_body(*refs, kinds, has_rope):
    it = iter(refs)
    x_ref, g_ref = next(it), next(it)
    if has_rope:
        cos, sin1, sin2 = next(it)[...], next(it)[...], next(it)[...]
        gm = _head_mean_matrix(LANES)
    w_refs = [next(it) for _ in kinds]
    gain_refs = [next(it) if kind[0] == "qk" else None for kind in kinds]
    x = x_ref[...]
    tm = x.shape[0]
    ms = jnp.mean(x * x, axis=-1, keepdims=True)
    h = (x * lax.rsqrt(ms + NORM_EPS) * g_ref[...]).astype(BF16)
    for kind, w_ref, gain_ref in zip(kinds, w_refs, gain_refs):
        acc = jnp.dot(h, w_ref[...], preferred_element_type=F32)
        n_blocks = acc.shape[1] // LANES
        o_ref = next(it)
        if kind[0] == "plain":
            o_ref[...] = acc.astype(o_ref.dtype)
        elif kind[0] == "vt":
            for kt in range(tm // ATTN_TILE):
                for c in range(n_blocks):
                    tile = acc[kt * ATTN_TILE:(kt + 1) * ATTN_TILE, c * LANES:(c + 1) * LANES]
                    o_ref[kt, c] = tile.T.astype(o_ref.dtype)
        else:
            _, _, _, scale, transposed, block_mean = kind
            mean_ref = next(it) if block_mean else None
            for c in range(n_blocks):
                cols = slice(c * LANES, (c + 1) * LANES)
                y = acc[:, cols]
                y = y * lax.rsqrt(_mm_sel_r(y * y, gm, 2) + NORM_EPS) * gain_ref[...]
                yr = y * cos + pltpu.roll(y, LANES - ROPE_DIM // 2, 1) * sin1 + pltpu.roll(y, ROPE_DIM // 2, 1) * sin2
                if transposed:
                    o_ref[0, cols, :] = (yr * scale).T.astype(o_ref.dtype)
                else:
                    o_ref[:, cols] = (yr * scale).astype(o_ref.dtype)
                if block_mean:
                    for rb in range(tm // MOBA_BLOCK):
                        blk = yr[rb * MOBA_BLOCK:(rb + 1) * MOBA_BLOCK]
                        mean_ref[rb, :, cols] = jnp.mean(blk, axis=0, keepdims=True)


def rms_proj(x2d, g, ws, kinds, tables=None, seq=None):
    n, d = x2d.shape
    tm = min(ROW_TILE, n)
    has_rope = tables is not None
    nt = seq // tm if has_rope else 1
    row = lambda i: (i, 0)
    const = lambda i: (0, 0)
    in_specs = [pl.BlockSpec((tm, d), row), pl.BlockSpec((1, d), const)]
    args = [x2d, g.reshape(1, d).astype(F32)]
    if has_rope:
        in_specs += [pl.BlockSpec((tm, LANES), lambda i: (i % nt, 0))] * 3
        args += list(tables)
    in_specs += [pl.BlockSpec(w.shape, const) for w in ws]
    args += list(ws)
    out_specs, out_shape = [], []
    for w, kind in zip(ws, kinds):
        nw = w.shape[1]
        if kind[0] == "plain":
            out_specs.append(pl.BlockSpec((tm, nw), row))
            out_shape.append(jax.ShapeDtypeStruct((n, nw), kind[1]))
        elif kind[0] == "vt":
            blk = (tm // ATTN_TILE, nw // LANES, LANES, ATTN_TILE)
            out_specs.append(pl.BlockSpec(blk, lambda i: (i, 0, 0, 0)))
            out_shape.append(jax.ShapeDtypeStruct((n // ATTN_TILE,) + blk[1:], BF16))
        else:
            _, gain, dtype, _, transposed, block_mean = kind
            in_specs.append(pl.BlockSpec((1, LANES), const))
            args.append(jnp.tile(gain.astype(F32), LANES // HEAD_DIM).reshape(1, LANES))
            if transposed:
                out_specs.append(pl.BlockSpec((1, nw, tm), lambda i: (i // nt, 0, i % nt)))
                out_shape.append(jax.ShapeDtypeStruct((n // seq, nw, seq), dtype))
            else:
                out_specs.append(pl.BlockSpec((tm, nw), row))
                out_shape.append(jax.ShapeDtypeStruct((n, nw), dtype))
            if block_mean:
                out_specs.append(pl.BlockSpec((tm // MOBA_BLOCK, 1, nw), lambda i: (i, 0, 0)))
                out_shape.append(jax.ShapeDtypeStruct((n // MOBA_BLOCK, 1, nw), F32))
    static_kinds = tuple(k if k[0] != "qk" else (k[0], None) + tuple(k[2:]) for k in kinds)
    return pl.pallas_call(
        functools.partial(_proj_body, kinds=static_kinds, has_rope=has_rope),
        grid=(n // tm,), in_specs=in_specs, out_specs=out_specs, out_shape=out_shape,
        compiler_params=_cparams(("parallel",)), name="rms_proj",
    )(*args)


def _out_proj_body(x_ref, *refs, n_in):
    o_ref = refs[-1]
    acc = x_ref[...]
    for m_ref, w_ref in zip(refs[:n_in], refs[n_in:2 * n_in]):
        acc = acc + jnp.dot(m_ref[...], w_ref[...], preferred_element_type=F32)
    o_ref[...] = acc


def out_proj(x2d, parts, ws):
    n, d = x2d.shape
    tm = min(OUT_ROWS, n)
    in_specs = [pl.BlockSpec((tm, d), lambda i: (i, 0))]
    in_specs += [pl.BlockSpec((tm, p.shape[1]), lambda i: (i, 0)) for p in parts]
    in_specs += [pl.BlockSpec(w.shape, lambda i: (0, 0)) for w in ws]
    return pl.pallas_call(
        functools.partial(_out_proj_body, n_in=len(parts)),
        grid=(n // tm,), in_specs=in_specs, out_specs=pl.BlockSpec((tm, d), lambda i: (i, 0)),
        out_shape=jax.ShapeDtypeStruct((n, d), F32),
        compiler_params=_cparams(("parallel",)), name="out_proj",
    )(x2d, *parts, *ws)


def rope_tables_lanes(seq):
    pos = jnp.arange(seq, dtype=F32)
    inv = 1.0 / (ROPE_THETA ** (jnp.arange(0, ROPE_DIM, 2, dtype=F32) / ROPE_DIM))
    ang = pos[:, None] * inv[None, :]
    cos, sin = jnp.cos(ang), jnp.sin(ang)
    half = ROPE_DIM // 2
    ones = jnp.ones((seq, HEAD_DIM - ROPE_DIM), F32)
    zeros_h = jnp.zeros((seq, half), F32)
    zeros_r = jnp.zeros((seq, HEAD_DIM - ROPE_DIM), F32)
    c = jnp.concatenate([cos, cos, ones], axis=1)
    s1 = jnp.concatenate([-sin, zeros_h, zeros_r], axis=1)
    s2 = jnp.concatenate([zeros_h, sin, zeros_r], axis=1)
    rep = LANES // HEAD_DIM
    return jnp.tile(c, (1, rep)), jnp.tile(s1, (1, rep)), jnp.tile(s2, (1, rep))


def _flash_t(i, k_ref, vt_ref, qm_ref, bufs, m_ref, acc_ref, v_rows, bias_row):
    t = ATTN_TILE
    (s0, s1), (p0, p1), (a0, a1) = bufs
    n_real = 2 * i + 1
    last = jnp.maximum(2 * i - 1, 0)
    ncol = Q_TILE // LANES

    def key_tile(x):
        own = jnp.where(x == 0, 2 * i + 1, 2 * i)
        return jnp.where(x < 2, own, jnp.minimum(x - 2, last))

    def a_part(x, h, s_w):
        kb = jnp.minimum(x - 2, last)
        k_blk = k_ref[0, pl.ds(pl.multiple_of(kb * t, t), t), :]
        s = jnp.dot(k_blk, qm_ref[h], preferred_element_type=F32)
        if bias_row is not None:
            s = s + bias_row(h, kb)
        s_w[h] = s

    def b_part(x, h, c, s_r, p_w, a_w):
        real = x <= n_real
        cs = slice(c * LANES, (c + 1) * LANES)
        s = s_r[h, :, cs]
        m_old = m_ref[h, :, cs]
        m_top = jnp.maximum(m_old, jnp.max(s, axis=0, keepdims=True))
        m_new = jnp.where(real, m_top, m_old)
        m_ref[h, :, cs] = m_new
        a_w[h, :, cs] = jnp.exp2(m_old - m_new)
        p_w[h, :, cs] = jnp.exp2(s - m_top).astype(BF16)

    def c_part(x, h, p_r, a_r):
        v_blk = jnp.concatenate([vt_ref[key_tile(x), 0, v_rows(h), :], jnp.ones((ONES_ROWS, t), BF16)], axis=0)
        v_blk = jnp.where(x <= n_real, v_blk, jnp.zeros_like(v_blk))
        acc_ref[h] = a_r[h] * acc_ref[h] + jnp.dot(v_blk, p_r[h], preferred_element_type=F32)

    def stage_a(x, s_w):
        for h in range(2):
            a_part(x, h, s_w)

    def stage_b(x, s_r, p_w, a_w):
        for h in range(2):
            for c in range(ncol):
                b_part(x, h, c, s_r, p_w, a_w)

    def stage_c(x, p_r, a_r):
        for h in range(2):
            c_part(x, h, p_r, a_r)

    m_ref[...] = jnp.full(m_ref.shape, NEG_INF, F32)
    acc_ref[...] = jnp.zeros(acc_ref.shape, F32)
    kpos = lax.broadcasted_iota(jnp.int32, (t, Q_TILE), 0)
    qpos = lax.broadcasted_iota(jnp.int32, (t, Q_TILE), 1)
    late = qpos >= t
    late_row = lax.broadcasted_iota(jnp.int32, (1, Q_TILE), 1) >= t
    allow_hi = jnp.logical_and(late, kpos <= qpos - t)
    allow_lo = jnp.logical_or(late, kpos <= qpos)
    k_hi = k_ref[0, pl.ds(pl.multiple_of((2 * i + 1) * t, t), t), :]
    k_lo = k_ref[0, pl.ds(pl.multiple_of(2 * i * t, t), t), :]
    for h in range(2):
        s0[h] = jnp.where(allow_hi, jnp.dot(k_hi, qm_ref[h], preferred_element_type=F32), NEG_INF)
        s_lo = jnp.dot(k_lo, qm_ref[h], preferred_element_type=F32)
        if bias_row is not None:
            s_lo = s_lo + jnp.where(late_row, bias_row(h, 2 * i), 0.0)
        s1[h] = jnp.where(allow_lo, s_lo, NEG_INF)
    bufs_of = ((s0, p0, a0), (s1, p1, a1))

    def step(x, k):
        s_w, p_r, a_r = bufs_of[k % 2]
        s_r, p_w, a_w = bufs_of[(k + 1) % 2]
        stage_c(x - 2, p_r, a_r)
        stage_b(x - 1, s_r, p_w, a_w)
        stage_a(x, s_w)

    stage_b(0, s0, p0, a0)
    u_steps = FLASH_UNROLL

    def unrolled(u, carry):
        for k in range(u_steps):
            step(u_steps * u + 2 + k, k)
        return carry

    n_steps = 2 * i + 2
    groups = n_steps // u_steps
    lax.fori_loop(0, groups, unrolled, 0)
    first = u_steps * groups + 2
    chunk = u_steps // 2
    while chunk >= 2:
        def tail(first=first, chunk=chunk):
            for k in range(chunk):
                step(first + k, k)
        pl.when((n_steps & chunk) != 0)(tail)
        first = first + (n_steps & chunk)
        chunk //= 2


def _flash_scratch(v_rows):
    t, tq = ATTN_TILE, Q_TILE
    return [pltpu.VMEM((2, LANES, tq), BF16),
            pltpu.VMEM((2, t, tq), F32), pltpu.VMEM((2, t, tq), F32),
            pltpu.VMEM((2, t, tq), BF16), pltpu.VMEM((2, t, tq), BF16),
            pltpu.VMEM((2, 1, tq), F32), pltpu.VMEM((2, 1, tq), F32),
            pltpu.VMEM((2, 1, tq), F32), pltpu.VMEM((2, v_rows + ONES_ROWS, tq), F32)]


def _moba_body(qt_ref, k_ref, vt_ref, km_ref, g_ref, o_ref, bias_ref, qm_ref, s0, s1, p0, p1, a0, a1,
               m_ref, acc_ref, *, nb):
    i = pl.program_id(2)
    t = Q_TILE
    bufs = ((s0, s1), (p0, p1), (a0, a1))
    q_t = qt_ref[0]
    dim_lo = lax.broadcasted_iota(jnp.int32, (LANES, 1), 0) < HEAD_DIM
    lo = _lane_lo()
    km = km_ref[0]
    blk = lax.broadcasted_iota(jnp.int32, (nb, t), 0).astype(F32)
    own = (2 * i + (lax.broadcasted_iota(jnp.int32, (1, t), 1) >= MOBA_BLOCK).astype(jnp.int32)).astype(F32)
    for h in range(2):
        qh = jnp.where(dim_lo if h == 0 else jnp.logical_not(dim_lo), q_t, 0.0)
        qm_ref[h] = (qh * (ATTN_SCALE * LOG2E)).astype(BF16)
        kmh = jnp.where(lo if h == 0 else jnp.logical_not(lo), km, 0.0)
        gate = jnp.dot(kmh, q_t, precision=HIGHEST, preferred_element_type=F32)
        gate = jnp.where(blk < own, gate, NEG_INF)
        sel = jnp.zeros((nb, t), F32)
        for _ in range(MOBA_TOPK):
            mx = jnp.max(gate, axis=0, keepdims=True)
            first = jnp.min(jnp.where(gate == mx, blk, float(nb)), axis=0, keepdims=True)
            hit = blk == first
            sel = jnp.where(jnp.logical_and(hit, first < own), 1.0, sel)
            gate = jnp.where(hit, -jnp.inf, gate)
        bias_ref[h] = jnp.where(sel > 0.0, 0.0, NEG_INF)

    _flash_t(i, k_ref, vt_ref, qm_ref, bufs, m_ref, acc_ref,
             v_rows=lambda h: slice(h * HEAD_DIM, (h + 1) * HEAD_DIM),
             bias_row=lambda h, kb: bias_ref[h, pl.ds(kb, 1), :])
    o_t = jnp.concatenate([acc_ref[h, :HEAD_DIM, :] / acc_ref[h, HEAD_DIM:HEAD_DIM + 1, :] for h in range(2)],
                          axis=0)
    o_ref[0] = (o_t.T * _silu(g_ref[0])).astype(o_ref.dtype)


def moba_attention(q_t, k, v_t, kmean, gate):
    b, w, s = q_t.shape
    t, tq = ATTN_TILE, Q_TILE
    nb = s // t
    tile = pl.BlockSpec((1, tq, LANES), lambda bi, p, i: (bi, i, p))
    return pl.pallas_call(
        functools.partial(_moba_body, nb=nb), grid=(b, w // LANES, s // tq),
        in_specs=[pl.BlockSpec((1, LANES, tq), lambda bi, p, i: (bi, p, i)),
                  pl.BlockSpec((1, s, LANES), lambda bi, p, i: (bi, 0, p)),
                  pl.BlockSpec((nb, 1, LANES, t), lambda bi, p, i: (bi, p, 0, 0)),
                  pl.BlockSpec((1, nb, LANES), lambda bi, p, i: (bi, 0, p)), tile],
        out_specs=tile, out_shape=jax.ShapeDtypeStruct((b, s, w), BF16),
        scratch_shapes=[pltpu.VMEM((2, nb, tq), F32)] + _flash_scratch(HEAD_DIM),
        compiler_params=_cparams(("parallel", "parallel", "arbitrary")), name="moba_attention",
    )(q_t, k, v_t, kmean, gate)


def _diff_body(qt_ref, k_ref, vt_ref, g_ref, lqk_ref, sg_ref, o_ref, qm_ref, s0, s1, p0, p1, a0, a1,
               m_ref, acc_ref, *, lam_init):
    i = pl.program_id(2)
    q_t = qt_ref[0]
    dim_lo = lax.broadcasted_iota(jnp.int32, (LANES, 1), 0) < HEAD_DIM
    for h in range(2):
        qm_ref[h] = jnp.where(dim_lo if h == 0 else jnp.logical_not(dim_lo), q_t, 0)
    _flash_t(i, k_ref, vt_ref, qm_ref, ((s0, s1), (p0, p1), (a0, a1)), m_ref, acc_ref,
             v_rows=lambda h: slice(0, LANES), bias_row=None)
    lqk = lqk_ref[...]
    lam = (jnp.exp(jnp.sum(lqk[0:1] * lqk[1:2], axis=1, keepdims=True))
           - jnp.exp(jnp.sum(lqk[2:3] * lqk[3:4], axis=1, keepdims=True)) + lam_init)
    att = [acc_ref[h, :LANES, :] / acc_ref[h, LANES:LANES + 1, :] for h in range(2)]
    o = (att[0] - lam * att[1]).T
    ms = jnp.mean(o * o, axis=1, keepdims=True)
    o = o * lax.rsqrt(ms + NORM_EPS) * sg_ref[...] * (1.0 - lam_init)
    o_ref[0] = (o * _silu(g_ref[0])).astype(o_ref.dtype)


def diff_attention(q_t, k, v_t, gate, lqk, subln_g, lam_init):
    b, w, s = q_t.shape
    t, tq = ATTN_TILE, Q_TILE
    nb = s // t
    tile = pl.BlockSpec((1, tq, LANES), lambda bi, h, i: (bi, i, h))
    return pl.pallas_call(
        functools.partial(_diff_body, lam_init=lam_init), grid=(b, w // LANES, s // tq),
        in_specs=[pl.BlockSpec((1, LANES, tq), lambda bi, h, i: (bi, h, i)),
                  pl.BlockSpec((1, s, LANES), lambda bi, h, i: (bi, 0, h)),
                  pl.BlockSpec((nb, 1, LANES, t), lambda bi, h, i: (bi, h, 0, 0)),
                  tile, pl.BlockSpec((4, HEAD_DIM), lambda bi, h, i: (0, 0)),
                  pl.BlockSpec((1, LANES), lambda bi, h, i: (0, 0))],
        out_specs=tile, out_shape=jax.ShapeDtypeStruct((b, s, w), BF16),
        scratch_shapes=_flash_scratch(LANES),
        compiler_params=_cparams(("parallel", "parallel", "arbitrary")), name="diff_attention",
    )(q_t, k, v_t, gate, lqk, subln_g.reshape(1, LANES).astype(F32))


def _mem_body(q_ref, km_ref, vm_ref, g_ref, qg_ref, kg_ref, o_ref):
    gm = _head_mean_matrix(LANES)
    lo = _lane_lo()
    masks = (lo, jnp.logical_not(lo))
    q = q_ref[0]
    q = q * lax.rsqrt(_mm_sel_r(q * q, gm, 2) + NORM_EPS) * qg_ref[...]
    k = km_ref[0]
    k = (k * lax.rsqrt(_mm_sel_r(k * k, gm, 2) + NORM_EPS) * kg_ref[...]).astype(BF16)
    v = vm_ref[0]
    outs = []
    for h in range(2):
        qh = (jnp.where(masks[h], q, 0.0) * ATTN_SCALE).astype(BF16)
        s = _mm_nt(qh, k)
        p = jnp.exp(s - jnp.max(s, axis=1, keepdims=True))
        outs.append(_mm(p, v) / jnp.sum(p, axis=1, keepdims=True))
    out = jnp.where(lo, outs[0], outs[1])
    o_ref[0] = (out * _silu(g_ref[0])).astype(o_ref.dtype)


def mem_attention(q, km, vm, gate, q_gain, k_gain):
    b, s, w = q.shape
    t = MEM_ROWS
    tile = pl.BlockSpec((1, t, LANES), lambda bi, p, i: (bi, i, p))
    mem = pl.BlockSpec((1, N_MEM, LANES), lambda bi, p, i: (bi, 0, p))
    gain = pl.BlockSpec((1, LANES), lambda bi, p, i: (0, 0))
    rep = LANES // HEAD_DIM
    return pl.pallas_call(
        _mem_body, grid=(b, w // LANES, s // t),
        in_specs=[tile, mem, mem, tile, gain, gain],
        out_specs=tile, out_shape=jax.ShapeDtypeStruct((b, s, w), BF16),
        compiler_params=_cparams(("parallel", "parallel", "parallel")), name="mem_attention",
    )(q, km, vm, gate, jnp.tile(q_gain.astype(F32), rep).reshape(1, LANES),
      jnp.tile(k_gain.astype(F32), rep).reshape(1, LANES))


def _shift(cur, prev8, mu, first):
    rows = lax.broadcasted_iota(jnp.int32, cur.shape, 0)
    before = jnp.where(first, 0.0, prev8[7:8, :])
    prev = jnp.where(rows == 0, before, pltpu.roll(cur, 1, 0))
    return cur + (prev - cur) * mu


def _rwkv_prep_body(r_ref, k_ref, v_ref, lo_ref, rp_ref, kp_ref, vp_ref, lp_ref, mu_ref, mul_ref, w0_ref, w2_ref,
                    a0_ref, a2_ref, kk_ref, ka_ref, rk_ref,
                    ro_ref, lw_ref, ko_ref, vo_ref, kn_ref, ao_ref, bo_ref):
    first = pl.program_id(1) == 0
    mu = mu_ref[...]
    r = _shift(r_ref[0], rp_ref[0], mu[0:1], first)
    k = _shift(k_ref[0], kp_ref[0], mu[1:2], first)
    v = _shift(v_ref[0], vp_ref[0], mu[2:3], first)
    lora = _shift(lo_ref[0], lp_ref[0], mul_ref[...], first)
    z = w0_ref[...] + _mm_16bit(jnp.tanh(lora), w2_ref[...])
    w_log = -(jnp.maximum(-z, 0.0) + jnp.log1p(jnp.exp(-jnp.abs(z)))) - 0.5
    a = jax.nn.sigmoid(a0_ref[...] + _mm_16bit(lora, a2_ref[...]))
    kk = k * kk_ref[...]
    k2 = k * (1.0 + (a - 1.0) * ka_ref[...])
    ones = _head_mean_matrix(B_W) * float(HEAD_DIM)
    kk = kk * lax.rsqrt(jnp.maximum(_mm_sel_r(kk * kk, ones, 2), 1e-24))
    ro_ref[0] = r
    lw_ref[0] = -jnp.exp(w_log)
    ko_ref[0] = k2
    vo_ref[0] = v
    kn_ref[0] = kk
    ao_ref[0] = a
    bo_ref[0] = _mm_sel_r(r * k2 * rk_ref[...], ones, 2) * v


def rwkv_prep(r, k, v, lora, mu, w0, w2, a0, a2, k_k, k_a, r_k):
    b, s, w = r.shape
    t = ROW_TILE
    tile = pl.BlockSpec((1, t, w), lambda bi, i: (bi, i, 0))
    ltile = pl.BlockSpec((1, t, LANES), lambda bi, i: (bi, i, 0))
    prev_idx = lambda bi, i: (bi, jnp.maximum(i * (t // 8) - 1, 0), 0)
    ptile = pl.BlockSpec((1, 8, w), prev_idx)
    pltile = pl.BlockSpec((1, 8, LANES), prev_idx)
    const = lambda shape: pl.BlockSpec(shape, lambda bi, i: (0, 0))
    mu3 = jnp.stack([mu[:w], mu[w:2 * w], mu[2 * w:3 * w]]).astype(F32)
    mul = jnp.zeros((1, LANES), F32).at[0, :2 * B_LORA].set(mu[3 * w:])
    w2p = jnp.zeros((LANES, w), F32).at[:B_LORA].set(w2)
    a2p = jnp.zeros((LANES, w), F32).at[B_LORA:2 * B_LORA].set(a2)
    row = lambda p: p.reshape(1, w).astype(F32)
    return pl.pallas_call(
        _rwkv_prep_body, grid=(b, s // t),
        in_specs=[tile, tile, tile, ltile, ptile, ptile, ptile, pltile, const((3, w)), const((1, LANES)),
                  const((1, w)), const((LANES, w)), const((1, w)), const((LANES, w)), const((1, w)), const((1, w)),
                  const((1, w))],
        out_specs=[tile] * 7, out_shape=[jax.ShapeDtypeStruct((b, s, w), F32)] * 7,
        compiler_params=_cparams(("parallel", "parallel")), name="rwkv_prep",
    )(r, k, v, lora, r, k, v, lora, mu3, mul, row(w0), w2p, row(a0), a2p, row(k_k), row(k_a), row(r_k))


def _tri(n, strict):
    r = lax.broadcasted_iota(jnp.int32, (n, n), 0)
    c = lax.broadcasted_iota(jnp.int32, (n, n), 1)
    return (c < r) if strict else (c <= r)


def _block_diag_mask():
    r = lax.broadcasted_iota(jnp.int32, (LANES, LANES), 0) // HEAD_DIM
    c = lax.broadcasted_iota(jnp.int32, (LANES, LANES), 1) // HEAD_DIM
    return r == c


def _rwkv_scan_body(r_ref, lw_ref, k_ref, v_ref, kn_ref, a_ref, bo_ref, g_ref, lg_ref, lb_ref, o_ref, s_ref, *,
                    nbatch, npair):
    @pl.when(pl.program_id(0) == 0)
    def _():
        s_ref[...] = jnp.zeros_like(s_ref)

    c = CHUNK
    lo = _lane_lo()
    masks = (lo, jnp.logical_not(lo))
    strict = _tri(c, True)
    incl = _tri(c, False)
    ltri = incl.astype(F32)
    eye = (lax.broadcasted_iota(jnp.int32, (c, c), 0) == lax.broadcasted_iota(jnp.int32, (c, c), 1)).astype(F32)
    bd = _block_diag_mask()
    gmean = _head_mean_matrix(LANES)

    def chain(sl, bi, pi):
        cols = slice(pi * LANES, (pi + 1) * LANES)
        idx = bi * npair + pi
        r, lw, k, v, kn, a = (ref[bi, sl, cols] for ref in (r_ref, lw_ref, k_ref, v_ref, kn_ref, a_ref))
        cw = _mm_sel_l(ltri, lw)
        yield
        cl = cw[c - 1:c, :]
        at = -kn * jnp.exp(cw - lw)
        e_neg = jnp.exp(-cw)
        bt = kn * a * e_neg
        kt = k * e_neg
        rt = r * jnp.exp(cw)
        e_end = jnp.exp(cl - cw)
        s0 = s_ref[idx]
        rhs = _mm_nt(at, s0)
        ys0 = _mm_nt(rt, s0)
        yield
        ah = [jnp.where(mk, at, 0.0) for mk in masks]
        rh = [jnp.where(mk, rt, 0.0) for mk in masks]
        n = [jnp.where(strict, _mm_nt(x, bt), 0.0) for x in ah]
        yield
        aak = [jnp.where(strict, _mm_nt(x, kt), 0.0) for x in ah]
        yield
        arb = [jnp.where(incl, _mm_nt(x, bt), 0.0) for x in rh]
        yield
        ark = [jnp.where(incl, _mm_nt(x, kt), 0.0) for x in rh]
        yield
        xs = [rhs + _mm(x, v) for x in aak]
        yv = [_mm(x, v) for x in ark]
        yield
        tinv = [eye + x for x in n]
        p = n
        for _ in range(5):
            p = [_mm(x, x) for x in p]
            yield
            tinv = [x + _mm(x, y) for x, y in zip(tinv, p)]
            yield
        u = jnp.where(lo, _mm(tinv[0], xs[0]), _mm(tinv[1], xs[1]))
        yield
        y = ys0 + jnp.where(lo, _mm(arb[0], u) + yv[0], _mm(arb[1], u) + yv[1])
        upd = _mm_tn(u, kn * a * e_end) + _mm_tn(v, k * e_end)
        yield
        s_ref[idx] = s0 * jnp.exp(cl) + jnp.where(bd, upd, 0.0)
        mean = _mm_sel_r(y, gmean, 2)
        yield
        d = y - mean
        var = _mm_sel_r(d * d, gmean, 2)
        yield
        yn = d * lax.rsqrt(var + RWKV_GN_EPS) * lg_ref[:, cols] + lb_ref[:, cols] + bo_ref[bi, sl, cols]
        o_ref[bi, sl, cols] = (yn * _silu(g_ref[bi, sl, cols])).astype(o_ref.dtype)

    def chunk(ci, carry):
        sl = pl.ds(pl.multiple_of(ci * c, c), c)
        gens = [chain(sl, bi, pi) for bi in range(nbatch) for pi in range(npair)]
        for _ in itertools.zip_longest(*gens):
            pass
        return carry

    lax.fori_loop(0, SCAN_ROWS // c, chunk, 0)


def rwkv_scan(r, lw, k, v, kn, a, bonus, gate, lnx_g, lnx_b):
    b, s, w = r.shape
    t = SCAN_ROWS
    tile = pl.BlockSpec((b, t, w), lambda i: (0, i, 0))
    vec = pl.BlockSpec((1, w), lambda i: (0, 0))
    return pl.pallas_call(
        functools.partial(_rwkv_scan_body, nbatch=b, npair=w // LANES), grid=(s // t,),
        in_specs=[tile] * 8 + [vec, vec],
        out_specs=tile, out_shape=jax.ShapeDtypeStruct((b, s, w), BF16),
        scratch_shapes=[pltpu.VMEM((b * (w // LANES), LANES, LANES), F32)],
        compiler_params=_cparams(("arbitrary",)), name="rwkv_scan",
    )(r, lw, k, v, kn, a, bonus, gate, lnx_g.reshape(1, w).astype(F32), lnx_b.reshape(1, w).astype(F32))


def _hgrn_body(q_ref, f_ref, i_ref, g_ref, lb_ref, gn_ref, o_ref, s_ref, *, nbatch, npair):
    @pl.when(pl.program_id(0) == 0)
    def _():
        s_ref[...] = jnp.zeros_like(s_ref)

    c = CHUNK
    lo = _lane_lo()
    masks = (lo, jnp.logical_not(lo))
    ltri = _tri(c, False).astype(F32)
    bd = _block_diag_mask()
    gmean = _head_mean_matrix(LANES)
    head_ones = gmean * float(HEAD_DIM)
    rows = lax.broadcasted_iota(jnp.int32, (SUB, 1), 0)

    def chain(sl, bi, pi):
        cols = slice(pi * LANES, (pi + 1) * LANES)
        idx = bi * npair + pi
        lb = lb_ref[:, cols]
        log_lb = jnp.log(lb)
        log_1m = jnp.log1p(-lb)
        q, fr, v = q_ref[bi, sl, cols], f_ref[bi, sl, cols], i_ref[bi, sl, cols]
        log_sig = jnp.minimum(fr, 0.0) - jnp.log1p(jnp.exp(-jnp.abs(fr)))
        z = log_1m + log_sig
        hi = jnp.maximum(log_lb, z)
        log_f = hi + jnp.log1p(jnp.exp(-jnp.abs(log_lb - z)))
        k = (1.0 - lb) * jax.nn.sigmoid(-fr)
        bc = _mm_sel_l(ltri, log_f)
        yield
        bl = bc[c - 1:c, :]
        s0 = s_ref[idx]
        outs = []
        for sb in range(c // SUB):
            r0 = sb * SUB
            qs, bs, ks, vs = (x[r0:r0 + SUB] for x in (q, bc, k, v))
            o = _mm_nt(qs * jnp.exp(bs), s0)
            if sb > 0:
                ref = bc[r0 - 1:r0, :]
                qh = qs * jnp.exp(bs - ref)
                kh = k[:r0] * jnp.exp(ref - bc[:r0])
                vh = v[:r0]
                sc = [_mm_nt(jnp.where(mk, qh, 0.0), kh) for mk in masks]
                yield
                o = o + jnp.where(lo, _mm(sc[0], vh), _mm(sc[1], vh))
            pair = jnp.concatenate(
                [qs * ks[si:si + 1, :] * jnp.exp(jnp.where(rows >= si, bs - bs[si:si + 1, :], NEG_INF))
                 for si in range(SUB)], axis=0)
            score = _mm_sel_r(pair, head_ones, 2)
            yield
            for si in range(SUB):
                o = o + score[si * SUB:(si + 1) * SUB] * vs[si:si + 1, :]
            outs.append(o)
        od = jnp.concatenate(outs, axis=0)
        s_ref[idx] = s0 * jnp.exp(bl) + jnp.where(bd, _mm_tn(v, k * jnp.exp(bl - bc)), 0.0)
        ms = _mm_sel_r(od * od, gmean, 2)
        yield
        on = od * lax.rsqrt(ms + NORM_EPS) * gn_ref[...]
        o_ref[bi, sl, cols] = (on * _silu(g_ref[bi, sl, cols])).astype(o_ref.dtype)

    def chunk(ci, carry):
        sl = pl.ds(pl.multiple_of(ci * c, c), c)
        gens = [chain(sl, bi, pi) for bi in range(nbatch) for pi in range(npair)]
        for _ in itertools.zip_longest(*gens):
            pass
        return carry

    lax.fori_loop(0, SCAN_ROWS // c, chunk, 0)


def hgrn2(q, f, iv, gate, lb, gn_g):
    b, s, w = q.shape
    t = SCAN_ROWS
    tile = pl.BlockSpec((b, t, w), lambda i: (0, i, 0))
    rep = LANES // HEAD_DIM
    return pl.pallas_call(
        functools.partial(_hgrn_body, nbatch=b, npair=w // LANES), grid=(s // t,),
        in_specs=[tile] * 4 + [pl.BlockSpec((1, w), lambda i: (0, 0)), pl.BlockSpec((1, LANES), lambda i: (0, 0))],
        out_specs=tile, out_shape=jax.ShapeDtypeStruct((b, s, w), BF16),
        scratch_shapes=[pltpu.VMEM((b * (w // LANES), LANES, LANES), F32)],
        compiler_params=_cparams(("arbitrary",)), name="hgrn2",
    )(q, f, iv, gate, lb.reshape(1, w).astype(F32), jnp.tile(gn_g.astype(F32), rep).reshape(1, LANES))


def _memory_kv(memf, b, g, w_kv):
    wb = w_kv.astype(BF16)
    km, vm = rms_proj(memf, g, [wb[:, :M_W], wb[:, M_W:]], [("plain", F32), ("plain", BF16)])
    return km.reshape(b, N_MEM, M_W), vm.reshape(b, N_MEM, M_W)


def _even_layer(xf, b, s, km, vm, tables, ln_g, w_in, w_out, a_qn_g, a_kn_g, m_qn_g, m_kn_g,
                mu, w0, w2, a0, a2, k_k, k_a, r_k, lnx_g, lnx_b):
    wb = w_in.astype(BF16)
    edges = [0]
    for width in (A_W, A_W, A_W, A_W, B_W, B_W, B_W, 2 * B_LORA, B_W, M_W, M_W):
        edges.append(edges[-1] + width)
    ws = [wb[:, edges[n]:edges[n + 1]] for n in range(11)]
    ws[7] = jnp.pad(ws[7], ((0, 0), (0, LANES - 2 * B_LORA)))
    f32 = ("plain", F32)
    kinds = [("qk", a_qn_g, F32, 1.0, True, False), ("qk", a_kn_g, BF16, 1.0, False, True), ("vt",),
             f32, f32, f32, f32, f32, f32, f32, f32]
    q_t, k, kmean, v_t, ga, rr, rk, rv, lora, gb, qm, gm = rms_proj(xf, ln_g, ws, kinds, tables, s)
    sh = lambda t: t.reshape(b, s, t.shape[-1])
    oa = moba_attention(q_t, sh(k), v_t, kmean.reshape(b, s // MOBA_BLOCK, A_W), sh(ga))
    pre = rwkv_prep(sh(rr), sh(rk), sh(rv), sh(lora), mu, w0, w2, a0, a2, k_k, k_a, r_k.reshape(-1))
    ob = rwkv_scan(*pre, sh(gb), lnx_g, lnx_b)
    om = mem_attention(sh(qm), km, vm, sh(gm), m_qn_g, m_kn_g)
    wo = w_out.astype(BF16)
    fl = lambda t: t.reshape(b * s, t.shape[-1])
    return out_proj(xf, [fl(oa), fl(ob), fl(om)], [wo[:A_W], wo[A_W:A_W + B_W], wo[A_W + B_W:]])


def _odd_layer(xf, b, s, km, vm, tables, li, lb, ln_g, w_in, w_out, c_qn_g, c_kn_g, lqk, subln_g, d_gn_g,
               m_qn_g, m_kn_g):
    wb = w_in.astype(BF16)
    edges = [0]
    for width in (C_W, C_W, C_W, C_W, D_W, D_W, D_W, D_W, M_W, M_W):
        edges.append(edges[-1] + width)
    ws = [wb[:, edges[n]:edges[n + 1]] for n in range(10)]
    f32 = ("plain", F32)
    kinds = [("qk", c_qn_g, BF16, ATTN_SCALE * LOG2E, True, False), ("qk", c_kn_g, BF16, 1.0, False, False),
             ("vt",), f32, f32, f32, f32, f32, f32, f32]
    q_t, k, v_t, gc, qd, fd, idd, gd, qm, gm = rms_proj(xf, ln_g, ws, kinds, tables, s)
    sh = lambda t: t.reshape(b, s, t.shape[-1])
    lam_init = 0.8 - 0.6 * math.exp(-0.3 * li)
    oc = diff_attention(q_t, sh(k), v_t, sh(gc), lqk, subln_g, lam_init)
    od = hgrn2(sh(qd), sh(fd), sh(idd), sh(gd), lb, d_gn_g)
    om = mem_attention(sh(qm), km, vm, sh(gm), m_qn_g, m_kn_g)
    wo = w_out.astype(BF16)
    fl = lambda t: t.reshape(b * s, t.shape[-1])
    return out_proj(xf, [fl(oc), fl(od), fl(om)], [wo[:C_W], wo[C_W:C_W + D_W], wo[C_W + D_W:]])


def kernel(x, mem, ln_g, mem_ln_g, w_mem_kv, m_qn_g, m_kn_g, e_w_in, e_w_out, a_qn_g, a_kn_g, b_mu, b_w0, b_w2, b_a0, b_a2, b_k_k, b_k_a, b_r_k, b_lnx_g, b_lnx_b, o_w_in, o_w_out, c_qn_g, c_kn_g, c_lq1, c_lk1, c_lq2, c_lk2, c_subln_g, d_lb, d_gn_g):
    b, s, d = x.shape
    depth = ln_g.shape[0]
    tables = rope_tables_lanes(s)
    lbs = jax.nn.softmax(d_lb.astype(F32), axis=0)
    lbs = jnp.cumsum(lbs, axis=0) - lbs[0:1]
    xf = x.reshape(b * s, d)
    memf = mem.reshape(b * N_MEM, d)
    for li in range(depth):
        j = li // 2
        km, vm = _memory_kv(memf, b, mem_ln_g[li], w_mem_kv[li])
        if li % 2 == 0:
            xf = _even_layer(xf, b, s, km, vm, tables, ln_g[li], e_w_in[j], e_w_out[j], a_qn_g[j], a_kn_g[j],
                             m_qn_g[li], m_kn_g[li], b_mu[j], b_w0[j], b_w2[j], b_a0[j], b_a2[j], b_k_k[j],
                             b_k_a[j], b_r_k[j], b_lnx_g[j], b_lnx_b[j])
        else:
            lqk = jnp.stack([c_lq1[j], c_lk1[j], c_lq2[j], c_lk2[j]]).astype(F32)
            xf = _odd_layer(xf, b, s, km, vm, tables, li, jnp.maximum(lbs[j], 0.0), ln_g[li], o_w_in[j],
                            o_w_out[j], c_qn_g[j], c_kn_g[j], lqk, c_subln_g[j], d_gn_g[j], m_qn_g[li], m_kn_g[li])
    return xf.reshape(b, s, d)
```

```python
import functools
import itertools
import math

import jax
import jax.numpy as jnp
from jax import lax
from jax.experimental import pallas as pl
from jax.experimental.pallas import tpu as pltpu

F32 = jnp.float32
BF16 = jnp.bfloat16
HIGHEST = lax.Precision.HIGHEST

N_MEM = 256
HEAD_DIM = 64
ROPE_THETA = 500000.0
ROPE_DIM = HEAD_DIM // 4
NORM_EPS = 1e-6
NEG_INF = -1e30
A_HEADS = 6
MOBA_BLOCK = 256
MOBA_TOPK = 3
B_HEADS = 6
B_LORA = 32
RWKV_GN_EPS = 64e-5
C_HEADS = 4
D_HEADS = 4
M_HEADS = 4
A_W = A_HEADS * HEAD_DIM
B_W = B_HEADS * HEAD_DIM
C_W = C_HEADS * 2 * HEAD_DIM
D_W = D_HEADS * HEAD_DIM
M_W = M_HEADS * HEAD_DIM
ATTN_SCALE = HEAD_DIM ** -0.5

LANES = 128
VMEM_LIMIT = 48 * 1024 * 1024

ROW_TILE = 512
OUT_ROWS = 1024
MEM_ROWS = 1024
LOG2E = math.log2(math.e)
ONES_ROWS = 16
ATTN_TILE = 256
Q_TILE = 2 * ATTN_TILE
FLASH_UNROLL = 16
CHUNK = 64
SUB = 16
SCAN_ROWS = 512

_NT = (((1,), (1,)), ((), ()))
_TN = (((0,), (0,)), ((), ()))


def _cparams(sem):
    return pltpu.CompilerParams(dimension_semantics=sem, vmem_limit_bytes=VMEM_LIMIT)


def _mm(a, b):
    return jnp.dot(a.astype(BF16), b.astype(BF16), preferred_element_type=F32)


def _mm_nt(a, b):
    return lax.dot_general(a.astype(BF16), b.astype(BF16), _NT, preferred_element_type=F32)


def _mm_tn(a, b):
    return lax.dot_general(a.astype(BF16), b.astype(BF16), _TN, preferred_element_type=F32)


def _split(x, terms):
    parts = []
    for _ in range(terms - 1):
        hi = x.astype(BF16)
        parts.append(hi)
        x = x - hi.astype(F32)
    parts.append(x.astype(BF16))
    return parts[::-1]


def _mm_sel_r(x, sel, terms=3):
    sb = sel.astype(BF16)
    return sum(jnp.dot(part, sb, preferred_element_type=F32) for part in _split(x, terms))


def _mm_sel_l(sel, x, terms=3):
    sb = sel.astype(BF16)
    return sum(jnp.dot(sb, part, preferred_element_type=F32) for part in _split(x, terms))


def _mm_16bit(a, b):
    a_lo, a_hi = _split(a, 2)
    b_lo, b_hi = _split(b, 2)
    return (jnp.dot(a_lo, b_hi, preferred_element_type=F32) + jnp.dot(a_hi, b_lo, preferred_element_type=F32)
            + jnp.dot(a_hi, b_hi, preferred_element_type=F32))


def _silu(x):
    return x * jax.nn.sigmoid(x)


def _lane_lo(width=LANES):
    lane = lax.broadcasted_iota(jnp.int32, (1, width), 1)
    return (lane % LANES) < HEAD_DIM


def _head_mean_matrix(width):
    r = lax.broadcasted_iota(jnp.int32, (width, width), 0) // HEAD_DIM
    c = lax.broadcasted_iota(jnp.int32, (width, width), 1) // HEAD_DIM
    return jnp.where(r == c, 1.0 / HEAD_DIM, 0.0).astype(F32)


def _proj_body(*refs, kinds, has_rope):
    it = iter(refs)
    x_ref, g_ref = next(it), next(it)
    if has_rope:
        cos, sin1, sin2 = next(it)[...], next(it)[...], next(it)[...]
        gm = _head_mean_matrix(LANES)
    w_refs = [next(it) for _ in kinds]
    gain_refs = [next(it) if kind[0] == "qk" else None for kind in kinds]
    x = x_ref[...]
    tm = x.shape[0]
    ms = jnp.mean(x * x, axis=-1, keepdims=True)
    h = (x * lax.rsqrt(ms + NORM_EPS) * g_ref[...]).astype(BF16)
    for kind, w_ref, gain_ref in zip(kinds, w_refs, gain_refs):
        acc = jnp.dot(h, w_ref[...], preferred_element_type=F32)
        n_blocks = acc.shape[1] // LANES
        o_ref = next(it)
        if kind[0] == "plain":
            o_ref[...] = acc.astype(o_ref.dtype)
        elif kind[0] == "vt":
            for kt in range(tm // ATTN_TILE):
                for c in range(n_blocks):
                    tile = acc[kt * ATTN_TILE:(kt + 1) * ATTN_TILE, c * LANES:(c + 1) * LANES]
                    o_ref[kt, c] = tile.T.astype(o_ref.dtype)
        else:
            _, _, _, scale, transposed, block_mean = kind
            mean_ref = next(it) if block_mean else None
            for c in range(n_blocks):
                cols = slice(c * LANES, (c + 1) * LANES)
                y = acc[:, cols]
                y = y * lax.rsqrt(_mm_sel_r(y * y, gm, 2) + NORM_EPS) * gain_ref[...]
                yr = y * cos + pltpu.roll(y, LANES - ROPE_DIM // 2, 1) * sin1 + pltpu.roll(y, ROPE_DIM // 2, 1) * sin2
                if transposed:
                    o_ref[0, cols, :] = (yr * scale).T.astype(o_ref.dtype)
                else:
                    o_ref[:, cols] = (yr * scale).astype(o_ref.dtype)
                if block_mean:
                    for rb in range(tm // MOBA_BLOCK):
                        blk = yr[rb * MOBA_BLOCK:(rb + 1) * MOBA_BLOCK]
                        mean_ref[rb, :, cols] = jnp.mean(blk, axis=0, keepdims=True)


def rms_proj(x2d, g, ws, kinds, tables=None, seq=None):
    n, d = x2d.shape
    tm = min(ROW_TILE, n)
    has_rope = tables is not None
    nt = seq // tm if has_rope else 1
    row = lambda i: (i, 0)
    const = lambda i: (0, 0)
    in_specs = [pl.BlockSpec((tm, d), row), pl.BlockSpec((1, d), const)]
    args = [x2d, g.reshape(1, d).astype(F32)]
    if has_rope:
        in_specs += [pl.BlockSpec((tm, LANES), lambda i: (i % nt, 0))] * 3
        args += list(tables)
    in_specs += [pl.BlockSpec(w.shape, const) for w in ws]
    args += list(ws)
    out_specs, out_shape = [], []
    for w, kind in zip(ws, kinds):
        nw = w.shape[1]
        if kind[0] == "plain":
            out_specs.append(pl.BlockSpec((tm, nw), row))
            out_shape.append(jax.ShapeDtypeStruct((n, nw), kind[1]))
        elif kind[0] == "vt":
            blk = (tm // ATTN_TILE, nw // LANES, LANES, ATTN_TILE)
            out_specs.append(pl.BlockSpec(blk, lambda i: (i, 0, 0, 0)))
            out_shape.append(jax.ShapeDtypeStruct((n // ATTN_TILE,) + blk[1:], BF16))
        else:
            _, gain, dtype, _, transposed, block_mean = kind
            in_specs.append(pl.BlockSpec((1, LANES), const))
            args.append(jnp.tile(gain.astype(F32), LANES // HEAD_DIM).reshape(1, LANES))
            if transposed:
                out_specs.append(pl.BlockSpec((1, nw, tm), lambda i: (i // nt, 0, i % nt)))
                out_shape.append(jax.ShapeDtypeStruct((n // seq, nw, seq), dtype))
            else:
                out_specs.append(pl.BlockSpec((tm, nw), row))
                out_shape.append(jax.ShapeDtypeStruct((n, nw), dtype))
            if block_mean:
                out_specs.append(pl.BlockSpec((tm // MOBA_BLOCK, 1, nw), lambda i: (i, 0, 0)))
                out_shape.append(jax.ShapeDtypeStruct((n // MOBA_BLOCK, 1, nw), F32))
    static_kinds = tuple(k if k[0] != "qk" else (k[0], None) + tuple(k[2:]) for k in kinds)
    return pl.pallas_call(
        functools.partial(_proj_body, kinds=static_kinds, has_rope=has_rope),
        grid=(n // tm,), in_specs=in_specs, out_specs=out_specs, out_shape=out_shape,
        compiler_params=_cparams(("parallel",)), name="rms_proj",
    )(*args)


def _out_proj_body(x_ref, *refs, n_in):
    o_ref = refs[-1]
    acc = x_ref[...]
    for m_ref, w_ref in zip(refs[:n_in], refs[n_in:2 * n_in]):
        acc = acc + jnp.dot(m_ref[...], w_ref[...], preferred_element_type=F32)
    o_ref[...] = acc


def out_proj(x2d, parts, ws):
    n, d = x2d.shape
    tm = min(OUT_ROWS, n)
    in_specs = [pl.BlockSpec((tm, d), lambda i: (i, 0))]
    in_specs += [pl.BlockSpec((tm, p.shape[1]), lambda i: (i, 0)) for p in parts]
    in_specs += [pl.BlockSpec(w.shape, lambda i: (0, 0)) for w in ws]
    return pl.pallas_call(
        functools.partial(_out_proj_body, n_in=len(parts)),
        grid=(n // tm,), in_specs=in_specs, out_specs=pl.BlockSpec((tm, d), lambda i: (i, 0)),
        out_shape=jax.ShapeDtypeStruct((n, d), F32),
        compiler_params=_cparams(("parallel",)), name="out_proj",
    )(x2d, *parts, *ws)


def rope_tables_lanes(seq):
    pos = jnp.arange(seq, dtype=F32)
    inv = 1.0 / (ROPE_THETA ** (jnp.arange(0, ROPE_DIM, 2, dtype=F32) / ROPE_DIM))
    ang = pos[:, None] * inv[None, :]
    cos, sin = jnp.cos(ang), jnp.sin(ang)
    half = ROPE_DIM // 2
    ones = jnp.ones((seq, HEAD_DIM - ROPE_DIM), F32)
    zeros_h = jnp.zeros((seq, half), F32)
    zeros_r = jnp.zeros((seq, HEAD_DIM - ROPE_DIM), F32)
    c = jnp.concatenate([cos, cos, ones], axis=1)
    s1 = jnp.concatenate([-sin, zeros_h, zeros_r], axis=1)
    s2 = jnp.concatenate([zeros_h, sin, zeros_r], axis=1)
    rep = LANES // HEAD_DIM
    return jnp.tile(c, (1, rep)), jnp.tile(s1, (1, rep)), jnp.tile(s2, (1, rep))


def _flash_t(i, k_ref, vt_ref, qm_ref, bufs, m_ref, acc_ref, v_rows, bias_row):
    t = ATTN_TILE
    (s0, s1), (p0, p1), (a0, a1) = bufs
    n_real = 2 * i + 1
    last = jnp.maximum(2 * i - 1, 0)
    ncol = Q_TILE // LANES

    def key_tile(x):
        own = jnp.where(x == 0, 2 * i + 1, 2 * i)
        return jnp.where(x < 2, own, jnp.minimum(x - 2, last))

    def a_part(x, h, s_w):
        kb = jnp.minimum(x - 2, last)
        k_blk = k_ref[0, pl.ds(pl.multiple_of(kb * t, t), t), :]
        s = jnp.dot(k_blk, qm_ref[h], preferred_element_type=F32)
        if bias_row is not None:
            s = s + bias_row(h, kb)
        s_w[h] = s

    def b_part(x, h, c, s_r, p_w, a_w):
        real = x <= n_real
        cs = slice(c * LANES, (c + 1) * LANES)
        s = s_r[h, :, cs]
        m_old = m_ref[h, :, cs]
        m_top = jnp.maximum(m_old, jnp.max(s, axis=0, keepdims=True))
        m_new = jnp.where(real, m_top, m_old)
        m_ref[h, :, cs] = m_new
        a_w[h, :, cs] = jnp.exp2(m_old - m_new)
        p_w[h, :, cs] = jnp.exp2(s - m_top).astype(BF16)

    def c_part(x, h, p_r, a_r):
        v_blk = jnp.concatenate([vt_ref[key_tile(x), 0, v_rows(h), :], jnp.ones((ONES_ROWS, t), BF16)], axis=0)
        v_blk = jnp.where(x <= n_real, v_blk, jnp.zeros_like(v_blk))
        acc_ref[h] = a_r[h] * acc_ref[h] + jnp.dot(v_blk, p_r[h], preferred_element_type=F32)

    def stage_a(x, s_w):
        for h in range(2):
            a_part(x, h, s_w)

    def stage_b(x, s_r, p_w, a_w):
        for h in range(2):
            for c in range(ncol):
                b_part(x, h, c, s_r, p_w, a_w)

    def stage_c(x, p_r, a_r):
        for h in range(2):
            c_part(x, h, p_r, a_r)

    m_ref[...] = jnp.full(m_ref.shape, NEG_INF, F32)
    acc_ref[...] = jnp.zeros(acc_ref.shape, F32)
    kpos = lax.broadcasted_iota(jnp.int32, (t, Q_TILE), 0)
    qpos = lax.broadcasted_iota(jnp.int32, (t, Q_TILE), 1)
    late = qpos >= t
    late_row = lax.broadcasted_iota(jnp.int32, (1, Q_TILE), 1) >= t
    allow_hi = jnp.logical_and(late, kpos <= qpos - t)
    allow_lo = jnp.logical_or(late, kpos <= qpos)
    k_hi = k_ref[0, pl.ds(pl.multiple_of((2 * i + 1) * t, t), t), :]
    k_lo = k_ref[0, pl.ds(pl.multiple_of(2 * i * t, t), t), :]
    for h in range(2):
        s0[h] = jnp.where(allow_hi, jnp.dot(k_hi, qm_ref[h], preferred_element_type=F32), NEG_INF)
        s_lo = jnp.dot(k_lo, qm_ref[h], preferred_element_type=F32)
        if bias_row is not None:
            s_lo = s_lo + jnp.where(late_row, bias_row(h, 2 * i), 0.0)
        s1[h] = jnp.where(allow_lo, s_lo, NEG_INF)
    bufs_of = ((s0, p0, a0), (s1, p1, a1))

    def step(x, k):
        s_w, p_r, a_r = bufs_of[k % 2]
        s_r, p_w, a_w = bufs_of[(k + 1) % 2]
        stage_c(x - 2, p_r, a_r)
        stage_b(x - 1, s_r, p_w, a_w)
        stage_a(x, s_w)

    stage_b(0, s0, p0, a0)
    u_steps = FLASH_UNROLL

    def unrolled(u, carry):
        for k in range(u_steps):
            step(u_steps * u + 2 + k, k)
        return carry

    n_steps = 2 * i + 2
    groups = n_steps // u_steps
    lax.fori_loop(0, groups, unrolled, 0)
    first = u_steps * groups + 2
    chunk = u_steps // 2
    while chunk >= 2:
        def tail(first=first, chunk=chunk):
            for k in range(chunk):
                step(first + k, k)
        pl.when((n_steps & chunk) != 0)(tail)
        first = first + (n_steps & chunk)
        chunk //= 2


def _flash_scratch(v_rows):
    t, tq = ATTN_TILE, Q_TILE
    return [pltpu.VMEM((2, LANES, tq), BF16),
            pltpu.VMEM((2, t, tq), F32), pltpu.VMEM((2, t, tq), F32),
            pltpu.VMEM((2, t, tq), BF16), pltpu.VMEM((2, t, tq), BF16),
            pltpu.VMEM((2, 1, tq), F32), pltpu.VMEM((2, 1, tq), F32),
            pltpu.VMEM((2, 1, tq), F32), pltpu.VMEM((2, v_rows + ONES_ROWS, tq), F32)]


def _moba_body(qt_ref, k_ref, vt_ref, km_ref, g_ref, o_ref, bias_ref, qm_ref, s0, s1, p0, p1, a0, a1,
               m_ref, acc_ref, *, nb):
    i = pl.program_id(2)
    t = Q_TILE
    bufs = ((s0, s1), (p0, p1), (a0, a1))
    q_t = qt_ref[0]
    dim_lo = lax.broadcasted_iota(jnp.int32, (LANES, 1), 0) < HEAD_DIM
    lo = _lane_lo()
    km = km_ref[0]
    blk = lax.broadcasted_iota(jnp.int32, (nb, t), 0).astype(F32)
    own = (2 * i + (lax.broadcasted_iota(jnp.int32, (1, t), 1) >= MOBA_BLOCK).astype(jnp.int32)).astype(F32)
    for h in range(2):
        qh = jnp.where(dim_lo if h == 0 else jnp.logical_not(dim_lo), q_t, 0.0)
        qm_ref[h] = (qh * (ATTN_SCALE * LOG2E)).astype(BF16)
        kmh = jnp.where(lo if h == 0 else jnp.logical_not(lo), km, 0.0)
        gate = jnp.dot(kmh, q_t, precision=HIGHEST, preferred_element_type=F32)
        gate = jnp.where(blk < own, gate, NEG_INF)
        sel = jnp.zeros((nb, t), F32)
        for _ in range(MOBA_TOPK):
            mx = jnp.max(gate, axis=0, keepdims=True)
            first = jnp.min(jnp.where(gate == mx, blk, float(nb)), axis=0, keepdims=True)
            hit = blk == first
            sel = jnp.where(jnp.logical_and(hit, first < own), 1.0, sel)
            gate = jnp.where(hit, -jnp.inf, gate)
        bias_ref[h] = jnp.where(sel > 0.0, 0.0, NEG_INF)

    _flash_t(i, k_ref, vt_ref, qm_ref, bufs, m_ref, acc_ref,
             v_rows=lambda h: slice(h * HEAD_DIM, (h + 1) * HEAD_DIM),
             bias_row=lambda h, kb: bias_ref[h, pl.ds(kb, 1), :])
    o_t = jnp.concatenate([acc_ref[h, :HEAD_DIM, :] / acc_ref[h, HEAD_DIM:HEAD_DIM + 1, :] for h in range(2)],
                          axis=0)
    o_ref[0] = (o_t.T * _silu(g_ref[0])).astype(o_ref.dtype)


def moba_attention(q_t, k, v_t, kmean, gate):
    b, w, s = q_t.shape
    t, tq = ATTN_TILE, Q_TILE
    nb = s // t
    tile = pl.BlockSpec((1, tq, LANES), lambda bi, p, i: (bi, i, p))
    return pl.pallas_call(
        functools.partial(_moba_body, nb=nb), grid=(b, w // LANES, s // tq),
        in_specs=[pl.BlockSpec((1, LANES, tq), lambda bi, p, i: (bi, p, i)),
                  pl.BlockSpec((1, s, LANES), lambda bi, p, i: (bi, 0, p)),
                  pl.BlockSpec((nb, 1, LANES, t), lambda bi, p, i: (bi, p, 0, 0)),
                  pl.BlockSpec((1, nb, LANES), lambda bi, p, i: (bi, 0, p)), tile],
        out_specs=tile, out_shape=jax.ShapeDtypeStruct((b, s, w), BF16),
        scratch_shapes=[pltpu.VMEM((2, nb, tq), F32)] + _flash_scratch(HEAD_DIM),
        compiler_params=_cparams(("parallel", "parallel", "arbitrary")), name="moba_attention",
    )(q_t, k, v_t, kmean, gate)


def _diff_body(qt_ref, k_ref, vt_ref, g_ref, lqk_ref, sg_ref, o_ref, qm_ref, s0, s1, p0, p1, a0, a1,
               m_ref, acc_ref, *, lam_init):
    i = pl.program_id(2)
    q_t = qt_ref[0]
    dim_lo = lax.broadcasted_iota(jnp.int32, (LANES, 1), 0) < HEAD_DIM
    for h in range(2):
        qm_ref[h] = jnp.where(dim_lo if h == 0 else jnp.logical_not(dim_lo), q_t, 0)
    _flash_t(i, k_ref, vt_ref, qm_ref, ((s0, s1), (p0, p1), (a0, a1)), m_ref, acc_ref,
             v_rows=lambda h: slice(0, LANES), bias_row=None)
    lqk = lqk_ref[...]
    lam = (jnp.exp(jnp.sum(lqk[0:1] * lqk[1:2], axis=1, keepdims=True))
           - jnp.exp(jnp.sum(lqk[2:3] * lqk[3:4], axis=1, keepdims=True)) + lam_init)
    att = [acc_ref[h, :LANES, :] / acc_ref[h, LANES:LANES + 1, :] for h in range(2)]
    o = (att[0] - lam * att[1]).T
    ms = jnp.mean(o * o, axis=1, keepdims=True)
    o = o * lax.rsqrt(ms + NORM_EPS) * sg_ref[...] * (1.0 - lam_init)
    o_ref[0] = (o * _silu(g_ref[0])).astype(o_ref.dtype)


def diff_attention(q_t, k, v_t, gate, lqk, subln_g, lam_init):
    b, w, s = q_t.shape
    t, tq = ATTN_TILE, Q_TILE
    nb = s // t
    tile = pl.BlockSpec((1, tq, LANES), lambda bi, h, i: (bi, i, h))
    return pl.pallas_call(
        functools.partial(_diff_body, lam_init=lam_init), grid=(b, w // LANES, s // tq),
        in_specs=[pl.BlockSpec((1, LANES, tq), lambda bi, h, i: (bi, h, i)),
                  pl.BlockSpec((1, s, LANES), lambda bi, h, i: (bi, 0, h)),
                  pl.BlockSpec((nb, 1, LANES, t), lambda bi, h, i: (bi, h, 0, 0)),
                  tile, pl.BlockSpec((4, HEAD_DIM), lambda bi, h, i: (0, 0)),
                  pl.BlockSpec((1, LANES), lambda bi, h, i: (0, 0))],
        out_specs=tile, out_shape=jax.ShapeDtypeStruct((b, s, w), BF16),
        scratch_shapes=_flash_scratch(LANES),
        compiler_params=_cparams(("parallel", "parallel", "arbitrary")), name="diff_attention",
    )(q_t, k, v_t, gate, lqk, subln_g.reshape(1, LANES).astype(F32))


def _mem_body(q_ref, km_ref, vm_ref, g_ref, qg_ref, kg_ref, o_ref):
    gm = _head_mean_matrix(LANES)
    lo = _lane_lo()
    masks = (lo, jnp.logical_not(lo))
    q = q_ref[0]
    q = q * lax.rsqrt(_mm_sel_r(q * q, gm, 2) + NORM_EPS) * qg_ref[...]
    k = km_ref[0]
    k = (k * lax.rsqrt(_mm_sel_r(k * k, gm, 2) + NORM_EPS) * kg_ref[...]).astype(BF16)
    v = vm_ref[0]
    outs = []
    for h in range(2):
        qh = (jnp.where(masks[h], q, 0.0) * ATTN_SCALE).astype(BF16)
        s = _mm_nt(qh, k)
        p = jnp.exp(s - jnp.max(s, axis=1, keepdims=True))
        outs.append(_mm(p, v) / jnp.sum(p, axis=1, keepdims=True))
    out = jnp.where(lo, outs[0], outs[1])
    o_ref[0] = (out * _silu(g_ref[0])).astype(o_ref.dtype)


def mem_attention(q, km, vm, gate, q_gain, k_gain):
    b, s, w = q.shape
    t = MEM_ROWS
    tile = pl.BlockSpec((1, t, LANES), lambda bi, p, i: (bi, i, p))
    mem = pl.BlockSpec((1, N_MEM, LANES), lambda bi, p, i: (bi, 0, p))
    gain = pl.BlockSpec((1, LANES), lambda bi, p, i: (0, 0))
    rep = LANES // HEAD_DIM
    return pl.pallas_call(
        _mem_body, grid=(b, w // LANES, s // t),
        in_specs=[tile, mem, mem, tile, gain, gain],
        out_specs=tile, out_shape=jax.ShapeDtypeStruct((b, s, w), BF16),
        compiler_params=_cparams(("parallel", "parallel", "parallel")), name="mem_attention",
    )(q, km, vm, gate, jnp.tile(q_gain.astype(F32), rep).reshape(1, LANES),
      jnp.tile(k_gain.astype(F32), rep).reshape(1, LANES))


def _shift(cur, prev8, mu, first):
    rows = lax.broadcasted_iota(jnp.int32, cur.shape, 0)
    before = jnp.where(first, 0.0, prev8[7:8, :])
    prev = jnp.where(rows == 0, before, pltpu.roll(cur, 1, 0))
    return cur + (prev - cur) * mu


def _rwkv_prep_body(r_ref, k_ref, v_ref, lo_ref, rp_ref, kp_ref, vp_ref, lp_ref, mu_ref, mul_ref, w0_ref, w2_ref,
                    a0_ref, a2_ref, kk_ref, ka_ref, rk_ref,
                    ro_ref, lw_ref, ko_ref, vo_ref, kn_ref, ao_ref, bo_ref):
    first = pl.program_id(1) == 0
    mu = mu_ref[...]
    r = _shift(r_ref[0], rp_ref[0], mu[0:1], first)
    k = _shift(k_ref[0], kp_ref[0], mu[1:2], first)
    v = _shift(v_ref[0], vp_ref[0], mu[2:3], first)
    lora = _shift(lo_ref[0], lp_ref[0], mul_ref[...], first)
    z = w0_ref[...] + _mm_16bit(jnp.tanh(lora), w2_ref[...])
    w_log = -(jnp.maximum(-z, 0.0) + jnp.log1p(jnp.exp(-jnp.abs(z)))) - 0.5
    a = jax.nn.sigmoid(a0_ref[...] + _mm_16bit(lora, a2_ref[...]))
    kk = k * kk_ref[...]
    k2 = k * (1.0 + (a - 1.0) * ka_ref[...])
    ones = _head_mean_matrix(B_W) * float(HEAD_DIM)
    kk = kk * lax.rsqrt(jnp.maximum(_mm_sel_r(kk * kk, ones, 2), 1e-24))
    ro_ref[0] = r
    lw_ref[0] = -jnp.exp(w_log)
    ko_ref[0] = k2
    vo_ref[0] = v
    kn_ref[0] = kk
    ao_ref[0] = a
    bo_ref[0] = _mm_sel_r(r * k2 * rk_ref[...], ones, 2) * v


def rwkv_prep(r, k, v, lora, mu, w0, w2, a0, a2, k_k, k_a, r_k):
    b, s, w = r.shape
    t = ROW_TILE
    tile = pl.BlockSpec((1, t, w), lambda bi, i: (bi, i, 0))
    ltile = pl.BlockSpec((1, t, LANES), lambda bi, i: (bi, i, 0))
    prev_idx = lambda bi, i: (bi, jnp.maximum(i * (t // 8) - 1, 0), 0)
    ptile = pl.BlockSpec((1, 8, w), prev_idx)
    pltile = pl.BlockSpec((1, 8, LANES), prev_idx)
    const = lambda shape: pl.BlockSpec(shape, lambda bi, i: (0, 0))
    mu3 = jnp.stack([mu[:w], mu[w:2 * w], mu[2 * w:3 * w]]).astype(F32)
    mul = jnp.zeros((1, LANES), F32).at[0, :2 * B_LORA].set(mu[3 * w:])
    w2p = jnp.zeros((LANES, w), F32).at[:B_LORA].set(w2)
    a2p = jnp.zeros((LANES, w), F32).at[B_LORA:2 * B_LORA].set(a2)
    row = lambda p: p.reshape(1, w).astype(F32)
    return pl.pallas_call(
        _rwkv_prep_body, grid=(b, s // t),
        in_specs=[tile, tile, tile, ltile, ptile, ptile, ptile, pltile, const((3, w)), const((1, LANES)),
                  const((1, w)), const((LANES, w)), const((1, w)), const((LANES, w)), const((1, w)), const((1, w)),
                  const((1, w))],
        out_specs=[tile] * 7, out_shape=[jax.ShapeDtypeStruct((b, s, w), F32)] * 7,
        compiler_params=_cparams(("parallel", "parallel")), name="rwkv_prep",
    )(r, k, v, lora, r, k, v, lora, mu3, mul, row(w0), w2p, row(a0), a2p, row(k_k), row(k_a), row(r_k))


def _tri(n, strict):
    r = lax.broadcasted_iota(jnp.int32, (n, n), 0)
    c = lax.broadcasted_iota(jnp.int32, (n, n), 1)
    return (c < r) if strict else (c <= r)


def _block_diag_mask():
    r = lax.broadcasted_iota(jnp.int32, (LANES, LANES), 0) // HEAD_DIM
    c = lax.broadcasted_iota(jnp.int32, (LANES, LANES), 1) // HEAD_DIM
    return r == c


def _rwkv_scan_body(r_ref, lw_ref, k_ref, v_ref, kn_ref, a_ref, bo_ref, g_ref, lg_ref, lb_ref, o_ref, s_ref, *,
                    nbatch, npair):
    @pl.when(pl.program_id(0) == 0)
    def _():
        s_ref[...] = jnp.zeros_like(s_ref)

    c = CHUNK
    lo = _lane_lo()
    masks = (lo, jnp.logical_not(lo))
    strict = _tri(c, True)
    incl = _tri(c, False)
    ltri = incl.astype(F32)
    eye = (lax.broadcasted_iota(jnp.int32, (c, c), 0) == lax.broadcasted_iota(jnp.int32, (c, c), 1)).astype(F32)
    bd = _block_diag_mask()
    gmean = _head_mean_matrix(LANES)

    def chain(sl, bi, pi):
        cols = slice(pi * LANES, (pi + 1) * LANES)
        idx = bi * npair + pi
        r, lw, k, v, kn, a = (ref[bi, sl, cols] for ref in (r_ref, lw_ref, k_ref, v_ref, kn_ref, a_ref))
        cw = _mm_sel_l(ltri, lw)
        yield
        cl = cw[c - 1:c, :]
        at = -kn * jnp.exp(cw - lw)
        e_neg = jnp.exp(-cw)
        bt = kn * a * e_neg
        kt = k * e_neg
        rt = r * jnp.exp(cw)
        e_end = jnp.exp(cl - cw)
        s0 = s_ref[idx]
        rhs = _mm_nt(at, s0)
        ys0 = _mm_nt(rt, s0)
        yield
        ah = [jnp.where(mk, at, 0.0) for mk in masks]
        rh = [jnp.where(mk, rt, 0.0) for mk in masks]
        n = [jnp.where(strict, _mm_nt(x, bt), 0.0) for x in ah]
        yield
        aak = [jnp.where(strict, _mm_nt(x, kt), 0.0) for x in ah]
        yield
        arb = [jnp.where(incl, _mm_nt(x, bt), 0.0) for x in rh]
        yield
        ark = [jnp.where(incl, _mm_nt(x, kt), 0.0) for x in rh]
        yield
        xs = [rhs + _mm(x, v) for x in aak]
        yv = [_mm(x, v) for x in ark]
        yield
        tinv = [eye + x for x in n]
        p = n
        for _ in range(5):
            p = [_mm(x, x) for x in p]
            yield
            tinv = [x + _mm(x, y) for x, y in zip(tinv, p)]
            yield
        u = jnp.where(lo, _mm(tinv[0], xs[0]), _mm(tinv[1], xs[1]))
        yield
        y = ys0 + jnp.where(lo, _mm(arb[0], u) + yv[0], _mm(arb[1], u) + yv[1])
        upd = _mm_tn(u, kn * a * e_end) + _mm_tn(v, k * e_end)
        yield
        s_ref[idx] = s0 * jnp.exp(cl) + jnp.where(bd, upd, 0.0)
        mean = _mm_sel_r(y, gmean, 2)
        yield
        d = y - mean
        var = _mm_sel_r(d * d, gmean, 2)
        yield
        yn = d * lax.rsqrt(var + RWKV_GN_EPS) * lg_ref[:, cols] + lb_ref[:, cols] + bo_ref[bi, sl, cols]
        o_ref[bi, sl, cols] = (yn * _silu(g_ref[bi, sl, cols])).astype(o_ref.dtype)

    def chunk(ci, carry):
        sl = pl.ds(pl.multiple_of(ci * c, c), c)
        gens = [chain(sl, bi, pi) for bi in range(nbatch) for pi in range(npair)]
        for _ in itertools.zip_longest(*gens):
            pass
        return carry

    lax.fori_loop(0, SCAN_ROWS // c, chunk, 0)


def rwkv_scan(r, lw, k, v, kn, a, bonus, gate, lnx_g, lnx_b):
    b, s, w = r.shape
    t = SCAN_ROWS
    tile = pl.BlockSpec((b, t, w), lambda i: (0, i, 0))
    vec = pl.BlockSpec((1, w), lambda i: (0, 0))
    return pl.pallas_call(
        functools.partial(_rwkv_scan_body, nbatch=b, npair=w // LANES), grid=(s // t,),
        in_specs=[tile] * 8 + [vec, vec],
        out_specs=tile, out_shape=jax.ShapeDtypeStruct((b, s, w), BF16),
        scratch_shapes=[pltpu.VMEM((b * (w // LANES), LANES, LANES), F32)],
        compiler_params=_cparams(("arbitrary",)), name="rwkv_scan",
    )(r, lw, k, v, kn, a, bonus, gate, lnx_g.reshape(1, w).astype(F32), lnx_b.reshape(1, w).astype(F32))


def _hgrn_body(q_ref, f_ref, i_ref, g_ref, lb_ref, gn_ref, o_ref, s_ref, *, nbatch, npair):
    @pl.when(pl.program_id(0) == 0)
    def _():
        s_ref[...] = jnp.zeros_like(s_ref)

    c = CHUNK
    lo = _lane_lo()
    masks = (lo, jnp.logical_not(lo))
    ltri = _tri(c, False).astype(F32)
    bd = _block_diag_mask()
    gmean = _head_mean_matrix(LANES)
    head_ones = gmean * float(HEAD_DIM)
    rows = lax.broadcasted_iota(jnp.int32, (SUB, 1), 0)

    def chain(sl, bi, pi):
        cols = slice(pi * LANES, (pi + 1) * LANES)
        idx = bi * npair + pi
        lb = lb_ref[:, cols]
        log_lb = jnp.log(lb)
        log_1m = jnp.log1p(-lb)
        q, fr, v = q_ref[bi, sl, cols], f_ref[bi, sl, cols], i_ref[bi, sl, cols]
        log_sig = jnp.minimum(fr, 0.0) - jnp.log1p(jnp.exp(-jnp.abs(fr)))
        z = log_1m + log_sig
        hi = jnp.maximum(log_lb, z)
        log_f = hi + jnp.log1p(jnp.exp(-jnp.abs(log_lb - z)))
        k = (1.0 - lb) * jax.nn.sigmoid(-fr)
        bc = _mm_sel_l(ltri, log_f)
        yield
        bl = bc[c - 1:c, :]
        s0 = s_ref[idx]
        outs = []
        for sb in range(c // SUB):
            r0 = sb * SUB
            qs, bs, ks, vs = (x[r0:r0 + SUB] for x in (q, bc, k, v))
            o = _mm_nt(qs * jnp.exp(bs), s0)
            if sb > 0:
                ref = bc[r0 - 1:r0, :]
                qh = qs * jnp.exp(bs - ref)
                kh = k[:r0] * jnp.exp(ref - bc[:r0])
                vh = v[:r0]
                sc = [_mm_nt(jnp.where(mk, qh, 0.0), kh) for mk in masks]
                yield
                o = o + jnp.where(lo, _mm(sc[0], vh), _mm(sc[1], vh))
            pair = jnp.concatenate(
                [qs * ks[si:si + 1, :] * jnp.exp(jnp.where(rows >= si, bs - bs[si:si + 1, :], NEG_INF))
                 for si in range(SUB)], axis=0)
            score = _mm_sel_r(pair, head_ones, 2)
            yield
            for si in range(SUB):
                o = o + score[si * SUB:(si + 1) * SUB] * vs[si:si + 1, :]
            outs.append(o)
        od = jnp.concatenate(outs, axis=0)
        s_ref[idx] = s0 * jnp.exp(bl) + jnp.where(bd, _mm_tn(v, k * jnp.exp(bl - bc)), 0.0)
        ms = _mm_sel_r(od * od, gmean, 2)
        yield
        on = od * lax.rsqrt(ms + NORM_EPS) * gn_ref[...]
        o_ref[bi, sl, cols] = (on * _silu(g_ref[bi, sl, cols])).astype(o_ref.dtype)

    def chunk(ci, carry):
        sl = pl.ds(pl.multiple_of(ci * c, c), c)
        gens = [chain(sl, bi, pi) for bi in range(nbatch) for pi in range(npair)]
        for _ in itertools.zip_longest(*gens):
            pass
        return carry

    lax.fori_loop(0, SCAN_ROWS // c, chunk, 0)


def hgrn2(q, f, iv, gate, lb, gn_g):
    b, s, w = q.shape
    t = SCAN_ROWS
    tile = pl.BlockSpec((b, t, w), lambda i: (0, i, 0))
    rep = LANES // HEAD_DIM
    return pl.pallas_call(
        functools.partial(_hgrn_body, nbatch=b, npair=w // LANES), grid=(s // t,),
        in_specs=[tile] * 4 + [pl.BlockSpec((1, w), lambda i: (0, 0)), pl.BlockSpec((1, LANES), lambda i: (0, 0))],
        out_specs=tile, out_shape=jax.ShapeDtypeStruct((b, s, w), BF16),
        scratch_shapes=[pltpu.VMEM((b * (w // LANES), LANES, LANES), F32)],
        compiler_params=_cparams(("arbitrary",)), name="hgrn2",
    )(q, f, iv, gate, lb.reshape(1, w).astype(F32), jnp.tile(gn_g.astype(F32), rep).reshape(1, LANES))


def _memory_kv(memf, b, g, w_kv):
    wb = w_kv.astype(BF16)
    km, vm = rms_proj(memf, g, [wb[:, :M_W], wb[:, M_W:]], [("plain", F32), ("plain", BF16)])
    return km.reshape(b, N_MEM, M_W), vm.reshape(b, N_MEM, M_W)


def _even_layer(xf, b, s, km, vm, tables, ln_g, w_in, w_out, a_qn_g, a_kn_g, m_qn_g, m_kn_g,
                mu, w0, w2, a0, a2, k_k, k_a, r_k, lnx_g, lnx_b):
    wb = w_in.astype(BF16)
    edges = [0]
    for width in (A_W, A_W, A_W, A_W, B_W, B_W, B_W, 2 * B_LORA, B_W, M_W, M_W):
        edges.append(edges[-1] + width)
    ws = [wb[:, edges[n]:edges[n + 1]] for n in range(11)]
    ws[7] = jnp.pad(ws[7], ((0, 0), (0, LANES - 2 * B_LORA)))
    f32 = ("plain", F32)
    kinds = [("qk", a_qn_g, F32, 1.0, True, False), ("qk", a_kn_g, BF16, 1.0, False, True), ("vt",),
             f32, f32, f32, f32, f32, f32, f32, f32]
    q_t, k, kmean, v_t, ga, rr, rk, rv, lora, gb, qm, gm = rms_proj(xf, ln_g, ws, kinds, tables, s)
    sh = lambda t: t.reshape(b, s, t.shape[-1])
    oa = moba_attention(q_t, sh(k), v_t, kmean.reshape(b, s // MOBA_BLOCK, A_W), sh(ga))
    pre = rwkv_prep(sh(rr), sh(rk), sh(rv), sh(lora), mu, w0, w2, a0, a2, k_k, k_a, r_k.reshape(-1))
    ob = rwkv_scan(*pre, sh(gb), lnx_g, lnx_b)
    om = mem_attention(sh(qm), km, vm, sh(gm), m_qn_g, m_kn_g)
    wo = w_out.astype(BF16)
    fl = lambda t: t.reshape(b * s, t.shape[-1])
    return out_proj(xf, [fl(oa), fl(ob), fl(om)], [wo[:A_W], wo[A_W:A_W + B_W], wo[A_W + B_W:]])


def _odd_layer(xf, b, s, km, vm, tables, li, lb, ln_g, w_in, w_out, c_qn_g, c_kn_g, lqk, subln_g, d_gn_g,
               m_qn_g, m_kn_g):
    wb = w_in.astype(BF16)
    edges = [0]
    for width in (C_W, C_W, C_W, C_W, D_W, D_W, D_W, D_W, M_W, M_W):
        edges.append(edges[-1] + width)
    ws = [wb[:, edges[n]:edges[n + 1]] for n in range(10)]
    f32 = ("plain", F32)
    kinds = [("qk", c_qn_g, BF16, ATTN_SCALE * LOG2E, True, False), ("qk", c_kn_g, BF16, 1.0, False, False),
             ("vt",), f32, f32, f32, f32, f32, f32, f32]
    q_t, k, v_t, gc, qd, fd, idd, gd, qm, gm = rms_proj(xf, ln_g, ws, kinds, tables, s)
    sh = lambda t: t.reshape(b, s, t.shape[-1])
    lam_init = 0.8 - 0.6 * math.exp(-0.3 * li)
    oc = diff_attention(q_t, sh(k), v_t, sh(gc), lqk, subln_g, lam_init)
    od = hgrn2(sh(qd), sh(fd), sh(idd), sh(gd), lb, d_gn_g)
    om = mem_attention(sh(qm), km, vm, sh(gm), m_qn_g, m_kn_g)
    wo = w_out.astype(BF16)
    fl = lambda t: t.reshape(b * s, t.shape[-1])
    return out_proj(xf, [fl(oc), fl(od), fl(om)], [wo[:C_W], wo[C_W:C_W + D_W], wo[C_W + D_W:]])


def kernel(x, mem, ln_g, mem_ln_g, w_mem_kv, m_qn_g, m_kn_g, e_w_in, e_w_out, a_qn_g, a_kn_g, b_mu, b_w0, b_w2, b_a0, b_a2, b_k_k, b_k_a, b_r_k, b_lnx_g, b_lnx_b, o_w_in, o_w_out, c_qn_g, c_kn_g, c_lq1, c_lk1, c_lq2, c_lk2, c_subln_g, d_lb, d_gn_g):
    b, s, d = x.shape
    depth = ln_g.shape[0]
    tables = rope_tables_lanes(s)
    lbs = jax.nn.softmax(d_lb.astype(F32), axis=0)
    lbs = jnp.cumsum(lbs, axis=0) - lbs[0:1]
    xf = x.reshape(b * s, d)
    memf = mem.reshape(b * N_MEM, d)
    for li in range(depth):
        j = li // 2
        km, vm = _memory_kv(memf, b, mem_ln_g[li], w_mem_kv[li])
        if li % 2 == 0:
            xf = _even_layer(xf, b, s, km, vm, tables, ln_g[li], e_w_in[j], e_w_out[j], a_qn_g[j], a_kn_g[j],
                             m_qn_g[li], m_kn_g[li], b_mu[j], b_w0[j], b_w2[j], b_a0[j], b_a2[j], b_k_k[j],
                             b_k_a[j], b_r_k[j], b_lnx_g[j], b_lnx_b[j])
        else:
            lqk = jnp.stack([c_lq1[j], c_lk1[j], c_lq2[j], c_lk2[j]]).astype(F32)
            xf = _odd_layer(xf, b, s, km, vm, tables, li, jnp.maximum(lbs[j], 0.0), ln_g[li], o_w_in[j],
                            o_w_out[j], c_qn_g[j], c_kn_g[j], lqk, c_subln_g[j], d_gn_g[j], m_qn_g[li], m_kn_g[li])
    return xf.reshape(b, s, d)
```
